```python
import numpy as np
import jax, jax.numpy as jnp
from jax import lax

D_MODEL = 1024
BATCH = 16
SEQ = 2048
DEPTH = 2

HEAD_DIM = 64
NSA_HEADS = 8
NSA_KV_GROUPS = 2
NSA_GROUP_SIZE = NSA_HEADS // NSA_KV_GROUPS
CMP_BLOCK = 32
CMP_STRIDE = 16
CMP_HIDDEN = 128
SEL_BLOCK = 64
SEL_TOPK = 8
WINDOW = 512
NSA_Q_BLOCK = 64
FORCED_SCORE = 1e9
DIFF_HEADS = 4
DIFF_Q_BLOCK = 128
MOE_GROUPS = 4
EXPERTS_PER_GROUP = 4
N_EXPERTS = MOE_GROUPS * EXPERTS_PER_GROUP
MOE_TOPK = 2
EXPERT_FF = D_MODEL // 8

RMS_EPS = 1e-6
SUBLN_EPS = 1e-5
NEG_INF = -1e30
N_ALIBI_HEADS = NSA_HEADS + DIFF_HEADS

NSA_Q_COLS = NSA_HEADS * HEAD_DIM
NSA_KV_COLS = NSA_KV_GROUPS * HEAD_DIM
NSA_GATE_COLS = 3 * NSA_HEADS
DIFF_QK_COLS = DIFF_HEADS * 2 * HEAD_DIM
DIFF_V_COLS = DIFF_HEADS * 2 * HEAD_DIM
IN_COLS = NSA_Q_COLS + 6 * NSA_KV_COLS + NSA_GATE_COLS + 2 * DIFF_QK_COLS + DIFF_V_COLS + 2 * D_MODEL

kernel_name = 'hybrid_nsa_diffattn_hier_moe'


def rms_norm(x, g, eps=RMS_EPS):
    xf = x.astype(jnp.float32)
    y = xf * lax.rsqrt(jnp.mean(xf * xf, axis=-1, keepdims=True) + eps)
    return (y * g.astype(jnp.float32)).astype(x.dtype)


def alibi_slopes():
    s = 2.0 ** (-8.0 * np.arange(1, N_ALIBI_HEADS + 1) / N_ALIBI_HEADS)
    return jnp.asarray(s, dtype=jnp.float32)


def masked_softmax(s, mask):
    return jax.nn.softmax(jnp.where(mask, s, NEG_INF), axis=-1)


def split_points():
    sizes = [NSA_Q_COLS] + [NSA_KV_COLS] * 6 + [NSA_GATE_COLS, DIFF_QK_COLS, DIFF_QK_COLS, DIFF_V_COLS, D_MODEL, D_MODEL]
    return [int(v) for v in np.cumsum(sizes)[:-1]]


def compress_tokens(kv, pe, w1, b1, w2, b2):
    B, T, G, HD = kv.shape
    nc = (T - CMP_BLOCK) // CMP_STRIDE + 1
    idx = np.arange(nc)[:, None] * CMP_STRIDE + np.arange(CMP_BLOCK)[None, :]
    blk = kv[:, idx] + pe[:, None, :]
    blk = jnp.swapaxes(blk, 2, 3).reshape(B, nc, G, CMP_BLOCK * HD)
    h = jax.nn.gelu(blk @ w1 + b1)
    return h @ w2 + b2


def cmp_to_sel_matrix(nc, nb):
    c0 = np.arange(nc)[:, None] * CMP_STRIDE
    s0 = np.arange(nb)[None, :] * SEL_BLOCK
    ov = np.maximum(0, np.minimum(c0 + CMP_BLOCK, s0 + SEL_BLOCK) - np.maximum(c0, s0))
    return jnp.asarray(ov / CMP_BLOCK, dtype=jnp.float32)


def nsa_attention(q, kc, vc, ks, vs, kw, vw, gate_logits, slopes):
    B, T = q.shape[0], q.shape[1]
    dt = q.dtype
    G, Hg, HD, QB = NSA_KV_GROUPS, NSA_GROUP_SIZE, HEAD_DIM, NSA_Q_BLOCK
    nc = kc.shape[1]
    nb = T // SEL_BLOCK
    k_sel = min(SEL_TOPK, nb)
    n_chunks = T // QB
    scale = HD ** -0.5
    sl = slopes.reshape(G, Hg)[None, :, :, None, None]
    cmp_end = jnp.arange(nc) * CMP_STRIDE + CMP_BLOCK - 1
    a_sel = cmp_to_sel_matrix(nc, nb)
    blk_start = jnp.arange(nb) * SEL_BLOCK
    blk_ids = jnp.arange(nb)
    ks_b = ks.reshape(B, nb, SEL_BLOCK, G, HD).transpose(0, 3, 1, 2, 4)
    vs_b = vs.reshape(B, nb, SEL_BLOCK, G, HD).transpose(0, 3, 1, 2, 4)
    kw_p = jnp.pad(kw, ((0, 0), (WINDOW, 0), (0, 0), (0, 0)))
    vw_p = jnp.pad(vw, ((0, 0), (WINDOW, 0), (0, 0), (0, 0)))
    bi = jnp.arange(B)[:, None, None, None]
    gi = jnp.arange(G)[None, :, None, None]
    qs = q.reshape(B, n_chunks, QB, G, Hg, HD).swapaxes(0, 1)
    gs = jax.nn.sigmoid(gate_logits.astype(jnp.float32)).astype(dt)
    gs = gs.reshape(B, n_chunks, QB, G, Hg, 3).swapaxes(0, 1)
    starts = jnp.arange(n_chunks, dtype=jnp.int32) * QB

    def chunk(args):
        qc, gc, start = args
        t = start + jnp.arange(QB)
        s = jnp.einsum('bqghd,bngd->bghqn', qc, kc).astype(jnp.float32) * scale
        dist = (t[:, None] - cmp_end[None, :]).astype(jnp.float32)
        valid = cmp_end[None, :] <= t[:, None]
        p_cmp = masked_softmax(s - sl * dist, valid) * jnp.any(valid, axis=-1)[:, None].astype(jnp.float32)
        o_cmp = jnp.einsum('bghqn,bngd->bqghd', p_cmp.astype(dt), vc)
        imp = jnp.einsum('bghqn,nm->bgqm', p_cmp, a_sel)
        cur = t // SEL_BLOCK
        forced = (blk_ids[None, :] == 0) | (blk_ids[None, :] == cur[:, None]) | (blk_ids[None, :] == cur[:, None] - 1)
        imp = jnp.where(forced, FORCED_SCORE, imp)
        imp = jnp.where(blk_start[None, :] <= t[:, None], imp, -1.0)
        _, idx = lax.top_k(imp, k_sel)
        kg = ks_b[bi, gi, idx].reshape(B, G, QB, k_sel * SEL_BLOCK, HD)
        vg = vs_b[bi, gi, idx].reshape(B, G, QB, k_sel * SEL_BLOCK, HD)
        tok = (idx[..., None] * SEL_BLOCK + jnp.arange(SEL_BLOCK)).reshape(B, G, QB, k_sel * SEL_BLOCK)
        s = jnp.einsum('bqghd,bgqld->bghql', qc, kg).astype(jnp.float32) * scale
        dist = (t[:, None] - tok).astype(jnp.float32)[:, :, None]
        p_sel = masked_softmax(s - sl * dist, (tok <= t[:, None])[:, :, None])
        o_sel = jnp.einsum('bghql,bgqld->bqghd', p_sel.astype(dt), vg)
        kwc = lax.dynamic_slice_in_dim(kw_p, start, QB + WINDOW, axis=1)
        vwc = lax.dynamic_slice_in_dim(vw_p, start, QB + WINDOW, axis=1)
        kpos = start - WINDOW + jnp.arange(QB + WINDOW)
        s = jnp.einsum('bqghd,bkgd->bghqk', qc, kwc).astype(jnp.float32) * scale
        d_i = t[:, None] - kpos[None, :]
        valid = (d_i >= 0) & (d_i < WINDOW) & (kpos[None, :] >= 0)
        p_win = masked_softmax(s - sl * d_i.astype(jnp.float32), valid)
        o_win = jnp.einsum('bghqk,bkgd->bqghd', p_win.astype(dt), vwc)
        return gc[..., 0:1] * o_cmp + gc[..., 1:2] * o_sel + gc[..., 2:3] * o_win

    out = lax.map(chunk, (qs, gs, starts))
    return out.swapaxes(0, 1).reshape(B, T, NSA_HEADS * HEAD_DIM)


def diff_attention(q, k, v, lam, lam_init, subln_g, slopes):
    B, T = q.shape[0], q.shape[1]
    dt = q.dtype
    QB = DIFF_Q_BLOCK
    n_chunks = T // QB
    scale = HEAD_DIM ** -0.5
    sl = slopes[None, :, None, None, None]
    kpos = jnp.arange(T)
    qs = q.reshape(B, n_chunks, QB, DIFF_HEADS, 2, HEAD_DIM).swapaxes(0, 1)
    starts = jnp.arange(n_chunks, dtype=jnp.int32) * QB

    def chunk(args):
        qc, start = args
        t = start + jnp.arange(QB)
        s = jnp.einsum('bqhcd,bkhcd->bhcqk', qc, k).astype(jnp.float32) * scale
        d_i = t[:, None] - kpos[None, :]
        p = masked_softmax(s - sl * d_i.astype(jnp.float32), d_i >= 0)
        a = p[:, :, 0] - lam * p[:, :, 1]
        return jnp.einsum('bhqk,bkhe->bqhe', a.astype(dt), v)

    o = lax.map(chunk, (qs, starts)).swapaxes(0, 1).reshape(B, T, DIFF_HEADS, 2 * HEAD_DIM)
    o = rms_norm(o, subln_g, SUBLN_EPS) * (1.0 - lam_init)
    return o.reshape(B, T, DIFF_HEADS * 2 * HEAD_DIM)


def hier_moe(x, wg, bg, we, be, w_gate, w_up, w_down):
    B, T, D = x.shape
    dt = x.dtype
    xf = x.reshape(B * T, D)
    g_logits = (xf @ wg).astype(jnp.float32) + bg.astype(jnp.float32)
    g_prob = jax.nn.softmax(g_logits, axis=-1)
    g_idx = jnp.argmax(g_logits, axis=-1)
    g_w = jnp.take_along_axis(g_prob, g_idx[:, None], axis=1)
    e_logits = jnp.einsum('nd,dge->nge', xf, we).astype(jnp.float32) + be.astype(jnp.float32)
    e_logits = jnp.take_along_axis(e_logits, g_idx[:, None, None], axis=1)[:, 0]
    top_v, top_i = lax.top_k(e_logits, MOE_TOPK)
    top_w = jax.nn.softmax(top_v, axis=-1) * g_w
    combine = jnp.einsum('ng,nke->nge', jax.nn.one_hot(g_idx, MOE_GROUPS),
                         jax.nn.one_hot(top_i, EXPERTS_PER_GROUP) * top_w[..., None])
    combine = combine.reshape(B * T, N_EXPERTS).astype(dt)
    h = jax.nn.silu(jnp.einsum('nd,edf->nef', xf, w_gate)) * jnp.einsum('nd,edf->nef', xf, w_up)
    y = jnp.einsum('nef,efd->nd', h * combine[:, :, None], w_down)
    return y.reshape(B, T, D)


def setup_inputs(seed: int = 0) -> dict:
    key = jax.random.key(seed)
    ks = jax.random.split(key, 24)
    D, L, HD = D_MODEL, DEPTH, HEAD_DIM

    def nrm(k, shape, scale):
        return jax.random.normal(k, shape, jnp.float32) * scale

    def gain(k, shape):
        return 1.0 + 0.02 * jax.random.normal(k, shape, jnp.float32)

    return {
        'x': nrm(ks[0], (BATCH, SEQ, D), 1.0),
        'norm1_g': gain(ks[1], (L, D)),
        'w_in': nrm(ks[2], (L, D, IN_COLS), D ** -0.5),
        'cmp_pe': nrm(ks[3], (L, 2, CMP_BLOCK, HD), 0.02),
        'cmp_w1': nrm(ks[4], (L, 2, CMP_BLOCK * HD, CMP_HIDDEN), (CMP_BLOCK * HD) ** -0.5),
        'cmp_b1': nrm(ks[5], (L, 2, CMP_HIDDEN), 0.02),
        'cmp_w2': nrm(ks[6], (L, 2, CMP_HIDDEN, HD), CMP_HIDDEN ** -0.5),
        'cmp_b2': nrm(ks[7], (L, 2, HD), 0.02),
        'diff_lambda': nrm(ks[8], (L, 4, HD), 0.1),
        'diff_subln_g': gain(ks[9], (L, 2 * HD)),
        'w_branch_a': nrm(ks[10], (L, NSA_HEADS * HD, D), (NSA_HEADS * HD) ** -0.5),
        'w_branch_b': nrm(ks[11], (L, DIFF_HEADS * 2 * HD, D), (DIFF_HEADS * 2 * HD) ** -0.5),
        'w_out': nrm(ks[12], (L, D, D), D ** -0.5),
        'norm2_g': gain(ks[13], (L, D)),
        'router_grp_w': nrm(ks[14], (L, D, MOE_GROUPS), D ** -0.5),
        'router_grp_b': nrm(ks[15], (L, MOE_GROUPS), 0.01),
        'router_exp_w': nrm(ks[16], (L, D, MOE_GROUPS, EXPERTS_PER_GROUP), D ** -0.5),
        'router_exp_b': nrm(ks[17], (L, MOE_GROUPS, EXPERTS_PER_GROUP), 0.01),
        'exp_w_gate': nrm(ks[18], (L, N_EXPERTS, D, EXPERT_FF), D ** -0.5),
        'exp_w_up': nrm(ks[19], (L, N_EXPERTS, D, EXPERT_FF), D ** -0.5),
        'exp_w_down': nrm(ks[20], (L, N_EXPERTS, EXPERT_FF, D), EXPERT_FF ** -0.5),
        'final_norm_g': gain(ks[21], (D,)),
    }


def reference(x, norm1_g, w_in, cmp_pe, cmp_w1, cmp_b1, cmp_w2, cmp_b2, diff_lambda, diff_subln_g,
              w_branch_a, w_branch_b, w_out, norm2_g, router_grp_w, router_grp_b, router_exp_w,
              router_exp_b, exp_w_gate, exp_w_up, exp_w_down, final_norm_g):
    B, T, _ = x.shape
    slopes = alibi_slopes()
    sl_nsa, sl_diff = slopes[:NSA_HEADS], slopes[NSA_HEADS:]
    cuts = split_points()
    G, HD = NSA_KV_GROUPS, HEAD_DIM
    h = x
    for l in range(DEPTH):
        xn = rms_norm(h, norm1_g[l])
        proj = xn @ w_in[l]
        (nq, kc_r, vc_r, ks_r, vs_r, kw_r, vw_r, ngate, dq, dk, dv, gate_a, gate_b) = jnp.split(proj, cuts, axis=-1)
        kc = compress_tokens(kc_r.reshape(B, T, G, HD), cmp_pe[l, 0], cmp_w1[l, 0], cmp_b1[l, 0], cmp_w2[l, 0], cmp_b2[l, 0])
        vc = compress_tokens(vc_r.reshape(B, T, G, HD), cmp_pe[l, 1], cmp_w1[l, 1], cmp_b1[l, 1], cmp_w2[l, 1], cmp_b2[l, 1])
        y_a = nsa_attention(nq.reshape(B, T, NSA_HEADS, HD), kc, vc,
                            ks_r.reshape(B, T, G, HD), vs_r.reshape(B, T, G, HD),
                            kw_r.reshape(B, T, G, HD), vw_r.reshape(B, T, G, HD),
                            ngate.reshape(B, T, NSA_HEADS, 3), sl_nsa)
        lam_init = 0.8 - 0.6 * float(np.exp(-0.3 * l))
        lp = diff_lambda[l].astype(jnp.float32)
        lam = jnp.exp(jnp.sum(lp[0] * lp[1])) - jnp.exp(jnp.sum(lp[2] * lp[3])) + lam_init
        y_b = diff_attention(dq.reshape(B, T, DIFF_HEADS, 2, HD), dk.reshape(B, T, DIFF_HEADS, 2, HD),
                             dv.reshape(B, T, DIFF_HEADS, 2 * HD), lam, lam_init, diff_subln_g[l], sl_diff)
        merged = jax.nn.sigmoid(gate_a) * (y_a @ w_branch_a[l]) + jax.nn.sigmoid(gate_b) * (y_b @ w_branch_b[l])
        h = h + merged @ w_out[l]
        h = h + hier_moe(rms_norm(h, norm2_g[l]), router_grp_w[l], router_grp_b[l], router_exp_w[l],
                         router_exp_b[l], exp_w_gate[l], exp_w_up[l], exp_w_down[l])
    return rms_norm(h, final_norm_g)
```

```python
import functools

import numpy as np
import jax
import jax.numpy as jnp
from jax import lax
from jax.experimental import pallas as pl
from jax.experimental.pallas import tpu as pltpu

F32 = jnp.float32
BF16 = jnp.bfloat16

D_MODEL = 1024
HEAD_DIM = 64
NSA_HEADS = 8
NSA_KV_GROUPS = 2
NSA_GROUP_SIZE = NSA_HEADS // NSA_KV_GROUPS
CMP_BLOCK = 32
CMP_STRIDE = 16
CMP_HIDDEN = 128
SEL_BLOCK = 64
SEL_TOPK = 8
WINDOW = 512
FORCED_SCORE = 1e9
DIFF_HEADS = 4
MOE_GROUPS = 4
EXPERTS_PER_GROUP = 4
N_EXPERTS = MOE_GROUPS * EXPERTS_PER_GROUP
EXPERT_FF = D_MODEL // 8
RMS_EPS = 1e-6
SUBLN_EPS = 1e-5
NEG_INF = -1e30
N_ALIBI_HEADS = NSA_HEADS + DIFF_HEADS

NSA_Q_COLS = NSA_HEADS * HEAD_DIM
NSA_KV_COLS = NSA_KV_GROUPS * HEAD_DIM
NSA_GATE_COLS = 3 * NSA_HEADS
DIFF_QK_COLS = DIFF_HEADS * 2 * HEAD_DIM
DIFF_V_COLS = DIFF_HEADS * 2 * HEAD_DIM
GATE_ROWS_PER_GROUP = 16

CH = 256
ROW_TILE = 512
VMEM_LIMIT = 56 * 1024 * 1024

_NT = (((1,), (1,)), ((), ()))
_TN = (((0,), (0,)), ((), ()))


def _dot(a, b):
    return jnp.dot(a, b, preferred_element_type=F32)


def _const_spec(shape):
    nd = len(shape)
    return pl.BlockSpec(shape, lambda *_: (0,) * nd, pipeline_mode=pl.Buffered(1))


def _params(sem):
    return pltpu.CompilerParams(dimension_semantics=sem, vmem_limit_bytes=VMEM_LIMIT)


def _alibi_slopes():
    return 2.0 ** (-8.0 * np.arange(1, N_ALIBI_HEADS + 1) / N_ALIBI_HEADS)


_NAT_WIDTHS = (NSA_KV_COLS, NSA_KV_COLS, DIFF_QK_COLS, 2 * NSA_KV_COLS, 2 * D_MODEL)
_TR_ROWS = (NSA_Q_COLS, DIFF_QK_COLS, NSA_KV_COLS, NSA_KV_COLS, DIFF_V_COLS, 2 * GATE_ROWS_PER_GROUP)
_TR_SCALE = (HEAD_DIM ** -0.5, HEAD_DIM ** -0.5, 1.0, 1.0, 1.0, 1.0)


def _inproj_kernel(x_ref, g_ref, wn_ref, wt_ref, *out_refs):
    nat_refs = out_refs[:len(_NAT_WIDTHS)]
    tr_refs = out_refs[len(_NAT_WIDTHS):]
    x = x_ref[...]
    xn = (x * lax.rsqrt(jnp.mean(x * x, axis=-1, keepdims=True) + RMS_EPS) * g_ref[...]).astype(BF16)
    off = 0
    for ref, width in zip(nat_refs, _NAT_WIDTHS):
        for c in range(0, width, 512):
            cw = min(512, width - c)
            ref[:, c:c + cw] = _dot(xn, wn_ref[:, off + c:off + c + cw]).astype(ref.dtype)
        off += width
    n_sub = x.shape[0] // CH
    off = 0
    for ref, rows, scale in zip(tr_refs, _TR_ROWS, _TR_SCALE):
        for c in range(0, rows, 256):
            rw = min(256, rows - c)
            res = lax.dot_general(wt_ref[off + c:off + c + rw, :], xn, _NT, preferred_element_type=F32)
            if scale != 1.0:
                res = res * scale
            for j in range(n_sub):
                ref[j, c:c + rw, :] = res[:, j * CH:(j + 1) * CH].astype(ref.dtype)
        off += rows


def _inproj(h2d, g, wn, wt):
    n = h2d.shape[0]
    tm = ROW_TILE
    nat_dtypes = (BF16, BF16, BF16, F32, BF16)
    tr_dtypes = (BF16, BF16, BF16, BF16, BF16, F32)
    out_shape = [jax.ShapeDtypeStruct((n, w), dt) for w, dt in zip(_NAT_WIDTHS, nat_dtypes)]
    out_shape += [jax.ShapeDtypeStruct((n // CH, r, CH), dt) for r, dt in zip(_TR_ROWS, tr_dtypes)]
    out_specs = [pl.BlockSpec((tm, w), lambda i: (i, 0)) for w in _NAT_WIDTHS]
    out_specs += [pl.BlockSpec((tm // CH, r, CH), lambda i: (i, 0, 0)) for r in _TR_ROWS]
    return pl.pallas_call(
        _inproj_kernel,
        grid=(n // tm,),
        in_specs=[pl.BlockSpec((tm, D_MODEL), lambda i: (i, 0)),
                  _const_spec((1, D_MODEL)),
                  _const_spec(wn.shape),
                  _const_spec(wt.shape)],
        out_specs=out_specs,
        out_shape=out_shape,
        compiler_params=_params(("parallel",)),
        name="inproj",
    )(h2d, g, wn, wt)


def _compress_kernel(hb_ref, pe_ref, w1_ref, b1_ref, w2k_ref, b2k_ref, w2v_ref, b2v_ref, kc_ref, vct_ref):
    hb = hb_ref[0]
    rows = hb.shape[0]
    top = (hb + pe_ref[0]).astype(BF16)
    bot = (hb + pe_ref[1]).astype(BF16)
    p = _dot(top, w1_ref[0])
    q = _dot(bot, w1_ref[1])
    q_next = pltpu.roll(q, rows - 1, 0)
    hid = jax.nn.gelu(p + q_next + b1_ref[...])
    width = hid.shape[1] // 2
    kc_ref[0] = (_dot(hid[:, :width].astype(BF16), w2k_ref[...]) + b2k_ref[...]).astype(kc_ref.dtype)
    vct = lax.dot_general(w2v_ref[...], hid[:, width:].astype(BF16), _NT, preferred_element_type=F32)
    vct_ref[0] = (vct + b2v_ref[...]).astype(vct_ref.dtype)


def _compress(hb, pe_hb, w1_big, b1p, w2k, b2k, w2vt, b2v):
    bsz, rows, width = hb.shape
    gk = NSA_KV_COLS
    return pl.pallas_call(
        _compress_kernel,
        grid=(bsz,),
        in_specs=[pl.BlockSpec((1, rows, width), lambda b: (b, 0, 0)),
                  _const_spec(pe_hb.shape), _const_spec(w1_big.shape), _const_spec(b1p.shape),
                  _const_spec(w2k.shape), _const_spec(b2k.shape), _const_spec(w2vt.shape),
                  _const_spec(b2v.shape)],
        out_specs=[pl.BlockSpec((1, rows, gk), lambda b: (b, 0, 0)),
                   pl.BlockSpec((1, gk, rows), lambda b: (b, 0, 0))],
        out_shape=[jax.ShapeDtypeStruct((bsz, rows, gk), BF16),
                   jax.ShapeDtypeStruct((bsz, gk, rows), BF16)],
        compiler_params=_params(("parallel",)),
        name="compress",
    )(hb, pe_hb, w1_big, b1p, w2k, b2k, w2vt, b2v)


def _flash_init(m_ref, l_ref, acc_ref):
    m_ref[...] = jnp.full(m_ref.shape, NEG_INF, F32)
    l_ref[...] = jnp.zeros(l_ref.shape, F32)
    acc_ref[...] = jnp.zeros(acc_ref.shape, F32)


def _flash_step(logits, v_t, m_ref, l_ref, acc_ref):
    m_prev = m_ref[...]
    m_new = jnp.maximum(m_prev, jnp.max(logits, axis=0, keepdims=True))
    alpha = jnp.exp(m_prev - m_new)
    p = jnp.exp(logits - m_new)
    l_ref[...] = alpha * l_ref[...] + jnp.sum(p, axis=0, keepdims=True)
    acc_ref[...] = alpha * acc_ref[...] + _dot(v_t, p.astype(BF16))
    m_ref[...] = m_new


def _nsa_kernel(slopes_ref, q_ref, ks_ref, kw_ref, vs_ref, vw_ref, kc_ref, vc_ref, ng_ref, asel_ref,
                ebig_ref, o_ref, psum_ref, ocmp_ref, mask_ref, m_s, l_s, acc_s, m_w, l_w, acc_w):
    i = pl.program_id(1)
    g = pl.program_id(2)
    t0 = i * CH
    n_cmp = kc_ref.shape[1]
    n_blk = asel_ref.shape[0]
    k_sel = min(SEL_TOPK, n_blk)

    lane = lax.broadcasted_iota(jnp.int32, (1, CH), 1)
    t_pos = t0 + lane
    sub2 = lax.broadcasted_iota(jnp.int32, (2 * HEAD_DIM, CH), 0)
    half0 = g * HEAD_DIM
    place = (sub2 >= half0) & (sub2 < half0 + HEAD_DIM)
    d0_i = lax.broadcasted_iota(jnp.int32, (CH, CH), 1) - lax.broadcasted_iota(jnp.int32, (CH, CH), 0)
    d0 = d0_i.astype(F32)

    def q_aug(hh):
        qh = q_ref[0, hh * HEAD_DIM:(hh + 1) * HEAD_DIM, :]
        return jnp.where(place, jnp.concatenate([qh, qh], axis=0), jnp.zeros((), BF16))

    n_idx = lax.broadcasted_iota(jnp.int32, (n_cmp, CH), 0)
    dist_c = t_pos - (n_idx * CMP_STRIDE + CMP_BLOCK - 1)
    valid_c = dist_c >= 0
    dist_cf = dist_c.astype(F32)
    kc = kc_ref[0]
    vc = vc_ref[0]
    psum_ref[...] = jnp.zeros(psum_ref.shape, F32)
    for hh in range(NSA_GROUP_SIZE):
        slope = slopes_ref[g * NSA_GROUP_SIZE + hh]
        s = _dot(kc, q_aug(hh))
        lg = jnp.where(valid_c, s - slope * dist_cf, NEG_INF)
        m = jnp.max(lg, axis=0, keepdims=True)
        p = jnp.where(valid_c, jnp.exp(lg - m), 0.0)
        l = jnp.sum(p, axis=0, keepdims=True)
        pc = p * jnp.where(l > 0.0, 1.0 / l, 0.0)
        psum_ref[...] += pc
        ocmp_ref[hh] = _dot(vc, pc.astype(BF16))

    imp = jnp.dot(asel_ref[...], psum_ref[...], precision=lax.Precision.HIGHEST,
                  preferred_element_type=F32)
    j_idx = lax.broadcasted_iota(jnp.int32, (n_blk, CH), 0)
    cur = jnp.right_shift(t_pos, 6)
    forced = (j_idx == 0) | (j_idx == cur) | (j_idx == cur - 1)
    imp = jnp.where(forced, FORCED_SCORE, imp)
    imp = jnp.where(j_idx * SEL_BLOCK <= t_pos, imp, -1.0)
    cnt = jnp.zeros((n_blk, CH), jnp.int32)
    for jp in range(n_blk):
        row = imp[jp:jp + 1, :]
        beats = (row > imp) | ((row == imp) & (j_idx > jp))
        cnt = cnt + beats.astype(jnp.int32)
    selm = (cnt < k_sel).astype(BF16)

    def mask_chunk(c, carry):
        mask_ref[c] = _dot(ebig_ref[c], selm)
        return carry
    lax.fori_loop(0, i + 1, mask_chunk, 0)

    for hh in range(NSA_GROUP_SIZE):
        slope = slopes_ref[g * NSA_GROUP_SIZE + hh]
        qa = q_aug(hh)
        sd0 = slope * d0

        _flash_init(m_s, l_s, acc_s)

        def sel_chunk(c, carry):
            s = _dot(ks_ref[c], qa)
            off = (t0 - c * CH).astype(F32)
            lg = jnp.where(mask_ref[c] > 0.5, s - sd0 - slope * off, NEG_INF)
            _flash_step(lg, vs_ref[c], m_s, l_s, acc_s)
            return carry
        lax.fori_loop(0, i, sel_chunk, 0)
        s = _dot(ks_ref[i], qa)
        lg = jnp.where((mask_ref[i] > 0.5) & (d0_i >= 0), s - sd0, NEG_INF)
        _flash_step(lg, vs_ref[i], m_s, l_s, acc_s)

        _flash_init(m_w, l_w, acc_w)
        n_back = WINDOW // CH
        for back in range(n_back, 0, -1):
            @pl.when(i >= back)
            def _():
                c = i - back
                s = _dot(kw_ref[c], qa)
                lg = s - sd0 - slope * float(back * CH)
                if back == n_back:
                    lg = jnp.where(d0_i < 0, lg, NEG_INF)
                _flash_step(lg, vw_ref[c], m_w, l_w, acc_w)
        s = _dot(kw_ref[i], qa)
        lg = jnp.where(d0_i >= 0, s - sd0, NEG_INF)
        _flash_step(lg, vw_ref[i], m_w, l_w, acc_w)

        g_cmp = jax.nn.sigmoid(ng_ref[0, 0 * NSA_GROUP_SIZE + hh:0 * NSA_GROUP_SIZE + hh + 1, :])
        g_sel = jax.nn.sigmoid(ng_ref[0, 1 * NSA_GROUP_SIZE + hh:1 * NSA_GROUP_SIZE + hh + 1, :])
        g_win = jax.nn.sigmoid(ng_ref[0, 2 * NSA_GROUP_SIZE + hh:2 * NSA_GROUP_SIZE + hh + 1, :])
        out = (g_cmp * ocmp_ref[hh] + g_sel * (acc_s[...] / l_s[...]) + g_win * (acc_w[...] / l_w[...]))
        o_ref[0, hh * HEAD_DIM:(hh + 1) * HEAD_DIM, :] = out.astype(o_ref.dtype)


def _nsa(slopes, nq_t, ks3, kw3, vs_t, vw_t, kc, vc_t, ng_t, asel_t, ebig, bsz, nq):
    n_cmp = kc.shape[1]
    n_blk = asel_t.shape[0]
    gh = NSA_GROUP_SIZE * HEAD_DIM
    grid_spec = pltpu.PrefetchScalarGridSpec(
        num_scalar_prefetch=1,
        grid=(bsz, nq, NSA_KV_GROUPS),
        in_specs=[
            pl.BlockSpec((1, gh, CH), lambda b, i, g, s: (b * nq + i, g, 0)),
            pl.BlockSpec((nq, CH, NSA_KV_COLS), lambda b, i, g, s: (b, 0, 0)),
            pl.BlockSpec((nq, CH, NSA_KV_COLS), lambda b, i, g, s: (b, 0, 0)),
            pl.BlockSpec((nq, HEAD_DIM, CH), lambda b, i, g, s: (b, g, 0)),
            pl.BlockSpec((nq, HEAD_DIM, CH), lambda b, i, g, s: (b, g, 0)),
            pl.BlockSpec((1, n_cmp, NSA_KV_COLS), lambda b, i, g, s: (b, 0, 0)),
            pl.BlockSpec((1, HEAD_DIM, n_cmp), lambda b, i, g, s: (b, g, 0)),
            pl.BlockSpec((1, GATE_ROWS_PER_GROUP, CH), lambda b, i, g, s: (b * nq + i, g, 0)),
            pl.BlockSpec(asel_t.shape, lambda b, i, g, s: (0, 0)),
            pl.BlockSpec(ebig.shape, lambda b, i, g, s: (0, 0, 0)),
        ],
        out_specs=pl.BlockSpec((1, gh, CH), lambda b, i, g, s: (b * nq + i, g, 0)),
        scratch_shapes=[
            pltpu.VMEM((n_cmp, CH), F32),
            pltpu.VMEM((NSA_GROUP_SIZE, HEAD_DIM, CH), F32),
            pltpu.VMEM((nq, CH, CH), F32),
            pltpu.VMEM((1, CH), F32), pltpu.VMEM((1, CH), F32), pltpu.VMEM((HEAD_DIM, CH), F32),
            pltpu.VMEM((1, CH), F32), pltpu.VMEM((1, CH), F32), pltpu.VMEM((HEAD_DIM, CH), F32),
        ],
    )
    return pl.pallas_call(
        _nsa_kernel,
        grid_spec=grid_spec,
        out_shape=jax.ShapeDtypeStruct((bsz * nq, NSA_Q_COLS, CH), BF16),
        compiler_params=_params(("parallel", "parallel", "arbitrary")),
        name="nsa_attention",
    )(slopes, nq_t, ks3, kw3, vs_t, vw_t, kc, vc_t, ng_t, asel_t, ebig)


def _diff_kernel(scal_ref, q_ref, k_ref, v_ref, lam_ref, gain_ref, o_ref,
                 m1, l1, acc1, m2, l2, acc2):
    h = pl.program_id(1)
    i = pl.program_id(2)
    slope = scal_ref[2 + h]
    lam_init = scal_ref[0]
    out_scale = scal_ref[1]
    t0 = i * CH
    sub2 = lax.broadcasted_iota(jnp.int32, (2 * HEAD_DIM, CH), 0)
    q = q_ref[0]
    zero = jnp.zeros((), BF16)
    qa = jnp.where(sub2 < HEAD_DIM, q, zero)
    qb = jnp.where(sub2 >= HEAD_DIM, q, zero)
    d0_i = lax.broadcasted_iota(jnp.int32, (CH, CH), 1) - lax.broadcasted_iota(jnp.int32, (CH, CH), 0)
    sd0 = slope * d0_i.astype(F32)

    _flash_init(m1, l1, acc1)
    _flash_init(m2, l2, acc2)

    def chunk(c, carry):
        kblk = k_ref[c]
        vblk = v_ref[c]
        bias = sd0 + slope * (t0 - c * CH).astype(F32)
        _flash_step(_dot(kblk, qa) - bias, vblk, m1, l1, acc1)
        _flash_step(_dot(kblk, qb) - bias, vblk, m2, l2, acc2)
        return carry
    lax.fori_loop(0, i, chunk, 0)
    kblk = k_ref[i]
    vblk = v_ref[i]
    causal = d0_i >= 0
    _flash_step(jnp.where(causal, _dot(kblk, qa) - sd0, NEG_INF), vblk, m1, l1, acc1)
    _flash_step(jnp.where(causal, _dot(kblk, qb) - sd0, NEG_INF), vblk, m2, l2, acc2)

    lp = lam_ref[...]
    lam = (jnp.exp(jnp.sum(lp[0:1] * lp[1:2], axis=1, keepdims=True))
           - jnp.exp(jnp.sum(lp[2:3] * lp[3:4], axis=1, keepdims=True)) + lam_init)
    o = acc1[...] / l1[...] - lam * (acc2[...] / l2[...])
    o = o * lax.rsqrt(jnp.mean(o * o, axis=0, keepdims=True) + SUBLN_EPS) * gain_ref[...]
    o_ref[0] = (o * out_scale).astype(o_ref.dtype)


def _diff(scal, dq_t, dk3, dv_t, lam_p, gain, bsz, nq):
    hd2 = 2 * HEAD_DIM
    grid_spec = pltpu.PrefetchScalarGridSpec(
        num_scalar_prefetch=1,
        grid=(bsz, DIFF_HEADS, nq),
        in_specs=[
            pl.BlockSpec((1, hd2, CH), lambda b, h, i, s: (b * nq + i, h, 0)),
            pl.BlockSpec((nq, CH, hd2), lambda b, h, i, s: (b, 0, h)),
            pl.BlockSpec((nq, hd2, CH), lambda b, h, i, s: (b, h, 0)),
            pl.BlockSpec(lam_p.shape, lambda b, h, i, s: (0, 0)),
            pl.BlockSpec(gain.shape, lambda b, h, i, s: (0, 0)),
        ],
        out_specs=pl.BlockSpec((1, hd2, CH), lambda b, h, i, s: (b * nq + i, h, 0)),
        scratch_shapes=[
            pltpu.VMEM((1, CH), F32), pltpu.VMEM((1, CH), F32), pltpu.VMEM((hd2, CH), F32),
            pltpu.VMEM((1, CH), F32), pltpu.VMEM((1, CH), F32), pltpu.VMEM((hd2, CH), F32),
        ],
    )
    return pl.pallas_call(
        _diff_kernel,
        grid_spec=grid_spec,
        out_shape=jax.ShapeDtypeStruct((bsz * nq, DIFF_V_COLS, CH), BF16),
        compiler_params=_params(("parallel", "parallel", "arbitrary")),
        name="diff_attention",
    )(scal, dq_t, dk3, dv_t, lam_p, gain)


def _merge_kernel(ya_ref, yb_ref, gates_ref, h_ref, wa_ref, wb_ref, wo_ref, o_ref):
    for j in range(ya_ref.shape[0]):
        rows = slice(j * CH, (j + 1) * CH)
        a = lax.dot_general(ya_ref[j], wa_ref[...], _TN, preferred_element_type=F32)
        b = lax.dot_general(yb_ref[j], wb_ref[...], _TN, preferred_element_type=F32)
        ga = jax.nn.sigmoid(gates_ref[rows, :D_MODEL].astype(F32))
        gb = jax.nn.sigmoid(gates_ref[rows, D_MODEL:].astype(F32))
        merged = (ga * a + gb * b).astype(BF16)
        o_ref[rows, :] = h_ref[rows, :] + _dot(merged, wo_ref[...])


def _merge(ya_t, yb_t, gates, h2d, wa, wb, wo):
    n = h2d.shape[0]
    tm = ROW_TILE
    return pl.pallas_call(
        _merge_kernel,
        grid=(n // tm,),
        in_specs=[pl.BlockSpec((tm // CH, NSA_Q_COLS, CH), lambda i: (i, 0, 0)),
                  pl.BlockSpec((tm // CH, DIFF_V_COLS, CH), lambda i: (i, 0, 0)),
                  pl.BlockSpec((tm, 2 * D_MODEL), lambda i: (i, 0)),
                  pl.BlockSpec((tm, D_MODEL), lambda i: (i, 0)),
                  _const_spec(wa.shape), _const_spec(wb.shape), _const_spec(wo.shape)],
        out_specs=pl.BlockSpec((tm, D_MODEL), lambda i: (i, 0)),
        out_shape=jax.ShapeDtypeStruct((n, D_MODEL), F32),
        compiler_params=_params(("parallel",)),
        name="merge_outproj",
    )(ya_t, yb_t, gates, h2d, wa, wb, wo)


def _first_argmax(x, rows, n):
    mx = jnp.max(x, axis=0, keepdims=True)
    idx = jnp.min(jnp.where(x == mx, rows, n), axis=0, keepdims=True)
    return mx, idx


def _moe_kernel(h_ref, g_ref, wr_ref, br_ref, eexp_ref, wg_ref, wu_ref, wd_ref, fg_ref, o_ref, *, final):
    hres = h_ref[...]
    xf = hres * lax.rsqrt(jnp.mean(hres * hres, axis=-1, keepdims=True) + RMS_EPS) * g_ref[...]
    xb = xf.astype(BF16)
    tm = hres.shape[0]

    logits = lax.dot_general(wr_ref[...], xf, _NT, precision=lax.Precision.HIGHEST,
                             preferred_element_type=F32) + br_ref[...]
    gl = logits[0:MOE_GROUPS]
    rows_g = lax.broadcasted_iota(jnp.int32, (MOE_GROUPS, tm), 0)
    gmax, gidx = _first_argmax(gl, rows_g, MOE_GROUPS)
    g_w = 1.0 / jnp.sum(jnp.exp(gl - gmax), axis=0, keepdims=True)
    esel = jnp.zeros((EXPERTS_PER_GROUP, tm), F32)
    for gg in range(MOE_GROUPS):
        lo = MOE_GROUPS + gg * EXPERTS_PER_GROUP
        esel = jnp.where(gidx == gg, logits[lo:lo + EXPERTS_PER_GROUP], esel)
    rows_e = lax.broadcasted_iota(jnp.int32, (EXPERTS_PER_GROUP, tm), 0)
    v1, i1 = _first_argmax(esel, rows_e, EXPERTS_PER_GROUP)
    rest = jnp.where(rows_e == i1, -jnp.inf, esel)
    v2, i2 = _first_argmax(rest, rows_e, EXPERTS_PER_GROUP)
    e21 = jnp.exp(v2 - v1)
    w1 = g_w / (1.0 + e21)
    w2 = g_w * e21 / (1.0 + e21)
    rows_c = lax.broadcasted_iota(jnp.int32, (N_EXPERTS, tm), 0)
    grp_c = jnp.right_shift(rows_c, EXPERTS_PER_GROUP.bit_length() - 1)
    exp_c = jnp.bitwise_and(rows_c, EXPERTS_PER_GROUP - 1)
    comb = jnp.where(grp_c == gidx,
                     jnp.where(exp_c == i1, w1, 0.0) + jnp.where(exp_c == i2, w2, 0.0), 0.0)
    comb_hi = comb.astype(BF16)
    comb_lo = (comb - comb_hi.astype(F32)).astype(BF16)

    acc = hres
    n_ff = wg_ref.shape[1]
    step = 512
    for c in range(0, n_ff, step):
        cols = slice(c, c + step)
        cexp = (lax.dot_general(comb_hi, eexp_ref[:, cols], _TN, preferred_element_type=F32)
                + lax.dot_general(comb_lo, eexp_ref[:, cols], _TN, preferred_element_type=F32))
        hg = _dot(xb, wg_ref[:, cols])
        hu = _dot(xb, wu_ref[:, cols])
        act = (jax.nn.silu(hg) * hu * cexp).astype(BF16)
        acc = acc + _dot(act, wd_ref[cols, :])
    if final:
        acc = acc * lax.rsqrt(jnp.mean(acc * acc, axis=-1, keepdims=True) + RMS_EPS) * fg_ref[...]
    o_ref[...] = acc


def _moe(h2d, g, wr_t, br, eexp, wg, wu, wd, fg, final):
    n = h2d.shape[0]
    tm = ROW_TILE
    return pl.pallas_call(
        functools.partial(_moe_kernel, final=final),
        grid=(n // tm,),
        in_specs=[pl.BlockSpec((tm, D_MODEL), lambda i: (i, 0)),
                  _const_spec(g.shape), _const_spec(wr_t.shape), _const_spec(br.shape),
                  _const_spec(eexp.shape), _const_spec(wg.shape), _const_spec(wu.shape),
                  _const_spec(wd.shape), _const_spec(fg.shape)],
        out_specs=pl.BlockSpec((tm, D_MODEL), lambda i: (i, 0)),
        out_shape=jax.ShapeDtypeStruct((n, D_MODEL), F32),
        compiler_params=_params(("parallel",)),
        name="moe_final" if final else "moe",
    )(h2d, g, wr_t, br, eexp, wg, wu, wd, fg)


def _split_points():
    sizes = ([NSA_Q_COLS] + [NSA_KV_COLS] * 6
             + [NSA_GATE_COLS, DIFF_QK_COLS, DIFF_QK_COLS, DIFF_V_COLS, D_MODEL, D_MODEL])
    return [int(v) for v in np.cumsum(sizes)[:-1]]


def _cmp_to_sel_t(n_rows, nc, nb):
    c0 = np.arange(nc)[:, None] * CMP_STRIDE
    s0 = np.arange(nb)[None, :] * SEL_BLOCK
    ov = np.maximum(0, np.minimum(c0 + CMP_BLOCK, s0 + SEL_BLOCK) - np.maximum(c0, s0)) / CMP_BLOCK
    out = np.zeros((nb, n_rows), np.float32)
    out[:, :nc] = ov.T
    return out


def kernel(x, norm1_g, w_in, cmp_pe, cmp_w1, cmp_b1, cmp_w2, cmp_b2, diff_lambda, diff_subln_g, w_branch_a, w_branch_b, w_out, norm2_g, router_grp_w, router_grp_b, router_exp_w, router_exp_b, exp_w_gate, exp_w_up, exp_w_down, final_norm_g):
    bsz, seq, d = x.shape
    depth = w_in.shape[0]
    assert d == D_MODEL and seq % ROW_TILE == 0 and seq >= WINDOW
    n = bsz * seq
    nq = seq // CH
    n_half = seq // CMP_STRIDE
    nc = (seq - CMP_BLOCK) // CMP_STRIDE + 1
    nb = seq // SEL_BLOCK
    g_kv = NSA_KV_GROUPS
    eye_g = jnp.eye(g_kv, dtype=F32)
    eye_c = jnp.eye(2, dtype=F32)

    (nq_w, kc_w, vc_w, ks_w, vs_w, kw_w, vw_w, ng_w, dq_w, dk_w, dv_w, ga_w, gb_w) = jnp.split(
        w_in, _split_points(), axis=-1)
    ng_w = ng_w.reshape(depth, d, g_kv, NSA_GROUP_SIZE, 3).transpose(0, 1, 2, 4, 3)
    ng_w = ng_w.reshape(depth, d, g_kv, 3 * NSA_GROUP_SIZE)
    ng_w = jnp.pad(ng_w, ((0, 0), (0, 0), (0, 0), (0, GATE_ROWS_PER_GROUP - 3 * NSA_GROUP_SIZE)))
    ng_w = ng_w.reshape(depth, d, g_kv * GATE_ROWS_PER_GROUP)
    wn_all = jnp.concatenate([ks_w, kw_w, dk_w, kc_w, vc_w, ga_w, gb_w], axis=-1).astype(BF16)
    wt_all = jnp.concatenate([nq_w, dq_w, vs_w, vw_w, dv_w, ng_w], axis=-1)
    wt_all = jnp.swapaxes(wt_all, 1, 2).astype(BF16)

    w1r = cmp_w1.reshape(depth, 2, 2, CMP_STRIDE, HEAD_DIM, CMP_HIDDEN)
    w1_big = jnp.einsum('Lchldf,cC,gG->LhlcgdCGf', w1r, eye_c, eye_g)
    w1_big = w1_big.reshape(depth, 2, CMP_STRIDE * 2 * NSA_KV_COLS, 2 * g_kv * CMP_HIDDEN).astype(BF16)
    per = cmp_pe.reshape(depth, 2, 2, CMP_STRIDE, HEAD_DIM)
    pe_hb = jnp.einsum('Lchld,g->Lhlcgd', per, jnp.ones((g_kv,), F32))
    pe_hb = pe_hb.reshape(depth, 2, 1, CMP_STRIDE * 2 * NSA_KV_COLS)
    b1p = jnp.broadcast_to(cmp_b1[:, :, None, :], (depth, 2, g_kv, CMP_HIDDEN)).reshape(depth, 1, -1)
    w2k = jnp.einsum('Lfd,gG->LgfGd', cmp_w2[:, 0], eye_g).reshape(depth, g_kv * CMP_HIDDEN, NSA_KV_COLS)
    w2vt = jnp.einsum('Lfd,gG->LGdgf', cmp_w2[:, 1], eye_g).reshape(depth, NSA_KV_COLS, g_kv * CMP_HIDDEN)
    w2k = w2k.astype(BF16)
    w2vt = w2vt.astype(BF16)
    b2k = jnp.tile(cmp_b2[:, 0], (1, g_kv))[:, None, :]
    b2v = jnp.tile(cmp_b2[:, 1], (1, g_kv))[:, :, None]

    slopes = _alibi_slopes()
    nsa_slopes = jnp.asarray(slopes[:NSA_HEADS], F32)
    asel_t = jnp.asarray(_cmp_to_sel_t(n_half, nc, nb))
    ebig = np.zeros((seq, nb), np.float32)
    ebig[np.arange(seq), np.arange(seq) // SEL_BLOCK] = 1.0
    ebig = jnp.asarray(ebig.reshape(nq, CH, nb), BF16)

    wa_all = w_branch_a.astype(BF16)
    wb_all = w_branch_b.astype(BF16)
    wo_all = w_out.astype(BF16)

    wr = jnp.concatenate([router_grp_w, router_exp_w.reshape(depth, d, N_EXPERTS)], axis=-1)
    n_r = MOE_GROUPS + N_EXPERTS
    wr_t = jnp.pad(jnp.swapaxes(wr, 1, 2), ((0, 0), (0, 32 - n_r), (0, 0)))
    br = jnp.concatenate([router_grp_b, router_exp_b.reshape(depth, N_EXPERTS)], axis=-1)
    br = jnp.pad(br, ((0, 0), (0, 32 - n_r)))[:, :, None]
    eexp = jnp.asarray(np.kron(np.eye(N_EXPERTS, dtype=np.float32), np.ones((1, EXPERT_FF), np.float32)), BF16)
    wg_all = jnp.swapaxes(exp_w_gate, 1, 2).reshape(depth, d, N_EXPERTS * EXPERT_FF).astype(BF16)
    wu_all = jnp.swapaxes(exp_w_up, 1, 2).reshape(depth, d, N_EXPERTS * EXPERT_FF).astype(BF16)
    wd_all = exp_w_down.reshape(depth, N_EXPERTS * EXPERT_FF, d).astype(BF16)

    h = x.reshape(n, d)
    for l in range(depth):
        (ks, kw, dk, kcvc, gates, nq_t, dq_t, vs_t, vw_t, dv_t, ng_t) = _inproj(
            h, norm1_g[l][None, :], wn_all[l], wt_all[l])
        hb = kcvc.reshape(bsz, n_half, CMP_STRIDE * 2 * NSA_KV_COLS)
        kc, vc_t = _compress(hb, pe_hb[l], w1_big[l], b1p[l], w2k[l], b2k[l], w2vt[l], b2v[l])
        ya_t = _nsa(nsa_slopes, nq_t, ks.reshape(n // CH, CH, NSA_KV_COLS), kw.reshape(n // CH, CH, NSA_KV_COLS),
                    vs_t, vw_t, kc, vc_t, ng_t, asel_t, ebig, bsz, nq)
        lam_init = 0.8 - 0.6 * float(np.exp(-0.3 * l))
        scal = jnp.asarray(np.concatenate([[lam_init, 1.0 - lam_init], slopes[NSA_HEADS:]]), F32)
        yb_t = _diff(scal, dq_t, dk.reshape(n // CH, CH, DIFF_QK_COLS), dv_t, diff_lambda[l],
                     diff_subln_g[l][:, None], bsz, nq)
        h = _merge(ya_t, yb_t, gates, h, wa_all[l], wb_all[l], wo_all[l])
        h = _moe(h, norm2_g[l][None, :], wr_t[l], br[l], eexp, wg_all[l], wu_all[l], wd_all[l],
                 final_norm_g[None, :], final=(l == depth - 1))
    return h.reshape(bsz, seq, d)
```

```python
import functools

import numpy as np
import jax
import jax.numpy as jnp
from jax import lax
from jax.experimental import pallas as pl
from jax.experimental.pallas import tpu as pltpu

F32 = jnp.float32
BF16 = jnp.bfloat16

D_MODEL = 1024
HEAD_DIM = 64
NSA_HEADS = 8
NSA_KV_GROUPS = 2
NSA_GROUP_SIZE = NSA_HEADS // NSA_KV_GROUPS
CMP_BLOCK = 32
CMP_STRIDE = 16
CMP_HIDDEN = 128
SEL_BLOCK = 64
SEL_TOPK = 8
WINDOW = 512
FORCED_SCORE = 1e9
DIFF_HEADS = 4
MOE_GROUPS = 4
EXPERTS_PER_GROUP = 4
N_EXPERTS = MOE_GROUPS * EXPERTS_PER_GROUP
EXPERT_FF = D_MODEL // 8
RMS_EPS = 1e-6
SUBLN_EPS = 1e-5
NEG_INF = -1e30
N_ALIBI_HEADS = NSA_HEADS + DIFF_HEADS

NSA_Q_COLS = NSA_HEADS * HEAD_DIM
NSA_KV_COLS = NSA_KV_GROUPS * HEAD_DIM
NSA_GATE_COLS = 3 * NSA_HEADS
DIFF_QK_COLS = DIFF_HEADS * 2 * HEAD_DIM
DIFF_V_COLS = DIFF_HEADS * 2 * HEAD_DIM
GATE_ROWS_PER_GROUP = 16

CH = 256
ROW_TILE = 512
VMEM_LIMIT = 56 * 1024 * 1024

LOG2E = float(np.log2(np.e))
Q_SCALE = HEAD_DIM ** -0.5 * LOG2E

QK_LANES = 2 * HEAD_DIM
AUG_LANES = 128
SLOPE_PIECES = 3
ALIBI_ROWS = 16
MASK_ROW0 = ALIBI_ROWS
MASK_BIG = 1e30

_NT = (((1,), (1,)), ((), ()))
_TN = (((0,), (0,)), ((), ()))


def _dot(a, b):
    return jnp.dot(a, b, preferred_element_type=F32)


def _const_spec(shape):
    nd = len(shape)
    return pl.BlockSpec(shape, lambda *_: (0,) * nd, pipeline_mode=pl.Buffered(1))


def _params(sem):
    return pltpu.CompilerParams(dimension_semantics=sem, vmem_limit_bytes=VMEM_LIMIT)


def _alibi_slopes():
    return 2.0 ** (-8.0 * np.arange(1, N_ALIBI_HEADS + 1) / N_ALIBI_HEADS)


_NAT_WIDTHS = (NSA_KV_COLS, NSA_KV_COLS, DIFF_QK_COLS, 2 * NSA_KV_COLS, 2 * D_MODEL)
_TR_ROWS = (NSA_Q_COLS, DIFF_QK_COLS, NSA_KV_COLS, NSA_KV_COLS, DIFF_V_COLS, 2 * GATE_ROWS_PER_GROUP)
_TR_SCALE = (Q_SCALE, Q_SCALE, 1.0, 1.0, 1.0, 1.0)


def _inproj_kernel(x_ref, g_ref, wn_ref, wt_ref, *out_refs):
    nat_refs = out_refs[:len(_NAT_WIDTHS)]
    tr_refs = out_refs[len(_NAT_WIDTHS):]
    x = x_ref[...]
    xn = (x * lax.rsqrt(jnp.mean(x * x, axis=-1, keepdims=True) + RMS_EPS) * g_ref[...]).astype(BF16)
    off = 0
    for ref, width in zip(nat_refs, _NAT_WIDTHS):
        for c in range(0, width, 512):
            cw = min(512, width - c)
            ref[:, c:c + cw] = _dot(xn, wn_ref[:, off + c:off + c + cw]).astype(ref.dtype)
        off += width
    n_sub = x.shape[0] // CH
    off = 0
    for ref, rows, scale in zip(tr_refs, _TR_ROWS, _TR_SCALE):
        for c in range(0, rows, 256):
            rw = min(256, rows - c)
            res = lax.dot_general(wt_ref[off + c:off + c + rw, :], xn, _NT, preferred_element_type=F32)
            if scale != 1.0:
                res = res * scale
            for j in range(n_sub):
                ref[j, c:c + rw, :] = res[:, j * CH:(j + 1) * CH].astype(ref.dtype)
        off += rows


def _inproj(h2d, g, wn, wt):
    n = h2d.shape[0]
    tm = ROW_TILE
    nat_dtypes = (BF16, BF16, BF16, F32, BF16)
    tr_dtypes = (BF16, BF16, BF16, BF16, BF16, F32)
    out_shape = [jax.ShapeDtypeStruct((n, w), dt) for w, dt in zip(_NAT_WIDTHS, nat_dtypes)]
    out_shape += [jax.ShapeDtypeStruct((n // CH, r, CH), dt) for r, dt in zip(_TR_ROWS, tr_dtypes)]
    out_specs = [pl.BlockSpec((tm, w), lambda i: (i, 0)) for w in _NAT_WIDTHS]
    out_specs += [pl.BlockSpec((tm // CH, r, CH), lambda i: (i, 0, 0)) for r in _TR_ROWS]
    return pl.pallas_call(
        _inproj_kernel,
        grid=(n // tm,),
        in_specs=[pl.BlockSpec((tm, D_MODEL), lambda i: (i, 0)),
                  _const_spec((1, D_MODEL)),
                  _const_spec(wn.shape),
                  _const_spec(wt.shape)],
        out_specs=out_specs,
        out_shape=out_shape,
        compiler_params=_params(("parallel",)),
        name="inproj",
    )(h2d, g, wn, wt)


def _compress_kernel(hb_ref, pe_ref, w1_ref, b1_ref, w2k_ref, b2k_ref, w2v_ref, b2v_ref, kc_ref, vct_ref):
    hb = hb_ref[0]
    rows = hb.shape[0]
    top = (hb + pe_ref[0]).astype(BF16)
    bot = (hb + pe_ref[1]).astype(BF16)
    p = _dot(top, w1_ref[0])
    q = _dot(bot, w1_ref[1])
    q_next = pltpu.roll(q, rows - 1, 0)
    hid = jax.nn.gelu(p + q_next + b1_ref[...])
    width = hid.shape[1] // 2
    kc_ref[0] = (_dot(hid[:, :width].astype(BF16), w2k_ref[...]) + b2k_ref[...]).astype(kc_ref.dtype)
    vct = lax.dot_general(w2v_ref[...], hid[:, width:].astype(BF16), _NT, preferred_element_type=F32)
    vct_ref[0] = (vct + b2v_ref[...]).astype(vct_ref.dtype)


def _compress(hb, pe_hb, w1_big, b1p, w2k, b2k, w2vt, b2v):
    bsz, rows, width = hb.shape
    gk = NSA_KV_COLS
    return pl.pallas_call(
        _compress_kernel,
        grid=(bsz,),
        in_specs=[pl.BlockSpec((1, rows, width), lambda b: (b, 0, 0)),
                  _const_spec(pe_hb.shape), _const_spec(w1_big.shape), _const_spec(b1p.shape),
                  _const_spec(w2k.shape), _const_spec(b2k.shape), _const_spec(w2vt.shape),
                  _const_spec(b2v.shape)],
        out_specs=[pl.BlockSpec((1, rows, gk), lambda b: (b, 0, 0)),
                   pl.BlockSpec((1, gk, rows), lambda b: (b, 0, 0))],
        out_shape=[jax.ShapeDtypeStruct((bsz, rows, gk), BF16),
                   jax.ShapeDtypeStruct((bsz, gk, rows), BF16)],
        compiler_params=_params(("parallel",)),
        name="compress",
    )(hb, pe_hb, w1_big, b1p, w2k, b2k, w2vt, b2v)


def _bf16_pieces(x):
    out = []
    rest = np.asarray(x, np.float32)
    for _ in range(SLOPE_PIECES):
        piece = rest.astype(BF16).astype(np.float32)
        out.append(piece)
        rest = rest - piece
    return out


def _slope_rows(slopes, heads_per_block, cols_per_head):
    sl2 = (np.asarray(slopes, np.float32).astype(np.float64) * LOG2E).astype(np.float32)
    pieces = np.stack(_bf16_pieces(sl2) * 2, axis=0)
    rows = np.zeros((ALIBI_ROWS, sl2.shape[0]), np.float32)
    rows[:pieces.shape[0]] = pieces
    rows = np.repeat(rows, cols_per_head, axis=1)
    rows = rows.reshape(ALIBI_ROWS, -1, heads_per_block * cols_per_head).transpose(1, 0, 2)
    return jnp.asarray(rows, BF16)


def _key_aug_tables(seq, nb):
    pos = np.arange(seq)
    aug = np.zeros((2, seq, AUG_LANES), np.float32)
    aug[:, :, 0:SLOPE_PIECES] = (pos % CH)[None, :, None]
    aug[:, :, SLOPE_PIECES:2 * SLOPE_PIECES] = (pos // CH * CH)[None, :, None]
    aug[0, pos, MASK_ROW0 + pos // SEL_BLOCK] = 1.0
    return jnp.asarray(aug.reshape(2, seq // CH, CH, AUG_LANES), BF16)


def _cmp_aug_table(n_rows):
    aug = np.zeros((n_rows, AUG_LANES), np.float32)
    aug[:, 0:SLOPE_PIECES] = (np.arange(n_rows) * CMP_STRIDE)[:, None]
    aug[:, SLOPE_PIECES:2 * SLOPE_PIECES] = CMP_BLOCK - 1
    return jnp.asarray(aug, BF16)


def _tile_lanes(x, reps):
    return jnp.concatenate([x] * reps, axis=1)


def _query_minus_key(reps):
    shape = (CH, reps * CH)
    q_off = jnp.bitwise_and(lax.broadcasted_iota(jnp.int32, shape, 1), CH - 1)
    return q_off - lax.broadcasted_iota(jnp.int32, shape, 0)


def _flash_init(m_ref, l_ref, acc_ref):
    m_ref[...] = jnp.full(m_ref.shape, NEG_INF, F32)
    l_ref[...] = jnp.zeros(l_ref.shape, F32)
    acc_ref[...] = jnp.zeros(acc_ref.shape, F32)


def _flash_step(logits, v_t, m_ref, l_ref, acc_ref):
    m_prev = m_ref[...]
    m_new = jnp.maximum(m_prev, jnp.max(logits, axis=0, keepdims=True))
    alpha = jnp.exp2(m_prev - m_new)
    p = jnp.exp2(logits - m_new)
    l_ref[...] = alpha * l_ref[...] + jnp.sum(p, axis=0, keepdims=True)
    acc_ref[...] = alpha * acc_ref[...] + _dot(v_t, p.astype(BF16))
    m_ref[...] = m_new


def _chunk_scores(k_blk, aug_blk, qa_ref):
    return _dot(jnp.concatenate([k_blk, aug_blk], axis=1), qa_ref[...])


def _nsa_kernel(q_ref, srow_ref, ks_ref, kw_ref, vs_ref, vw_ref, kaug_ref, kc_ref, kcaug_ref, vc_ref,
                ng_ref, asel_ref, o_ref, qa_ref, ocmp_ref, m_s, l_s, acc_s, m_w, l_w, acc_w):
    i = pl.program_id(1)
    g = pl.program_id(2)
    t0 = i * CH
    nh = NSA_GROUP_SIZE
    n_cmp = kc_ref.shape[1]
    n_blk = asel_ref.shape[0]
    k_sel = min(SEL_TOPK, n_blk)

    t_pos = t0 + lax.broadcasted_iota(jnp.int32, (1, CH), 1)
    d0_i = _query_minus_key(nh)
    causal = d0_i >= 0
    window_edge = d0_i < 0

    sub2 = lax.broadcasted_iota(jnp.int32, (QK_LANES, CH), 0)
    half0 = g * HEAD_DIM
    place = (sub2 >= half0) & (sub2 < half0 + HEAD_DIM)
    for hh in range(nh):
        qh = q_ref[0, hh * HEAD_DIM:(hh + 1) * HEAD_DIM, :]
        qa_ref[0:QK_LANES, hh * CH:(hh + 1) * CH] = jnp.where(
            place, jnp.concatenate([qh, qh], axis=0), jnp.zeros((), BF16))
    qa_ref[QK_LANES:QK_LANES + ALIBI_ROWS, :] = srow_ref[0]
    qa_ref[QK_LANES + MASK_ROW0:, :] = jnp.zeros((AUG_LANES - MASK_ROW0, nh * CH), BF16)

    n_idx = lax.broadcasted_iota(jnp.int32, (n_cmp, nh * CH), 0)
    t_pos_all = t0 + jnp.bitwise_and(lax.broadcasted_iota(jnp.int32, (1, nh * CH), 1), CH - 1)
    valid_c = n_idx * CMP_STRIDE + (CMP_BLOCK - 1) <= t_pos_all
    lg = jnp.where(valid_c, _chunk_scores(kc_ref[0], kcaug_ref[...], qa_ref), NEG_INF)
    m = jnp.max(lg, axis=0, keepdims=True)
    p = jnp.where(valid_c, jnp.exp2(lg - m), 0.0)
    l = jnp.sum(p, axis=0, keepdims=True)
    pc = p * jnp.where(l > 0.0, 1.0 / l, 0.0)
    ocmp_ref[...] = _dot(vc_ref[0], pc.astype(BF16))
    psum = pc[:, 0:CH]
    for hh in range(1, nh):
        psum = psum + pc[:, hh * CH:(hh + 1) * CH]

    imp = jnp.dot(asel_ref[...], psum, precision=lax.Precision.HIGHEST, preferred_element_type=F32)
    j_idx = lax.broadcasted_iota(jnp.int32, (n_blk, CH), 0)
    cur = jnp.right_shift(t_pos, SEL_BLOCK.bit_length() - 1)
    forced = (j_idx == 0) | (j_idx == cur) | (j_idx == cur - 1)
    imp = jnp.where(forced, FORCED_SCORE, imp)
    imp = jnp.where(j_idx * SEL_BLOCK <= t_pos, imp, -1.0)
    cnt = jnp.zeros((n_blk, CH), jnp.int32)
    for jp in range(n_blk):
        row = imp[jp:jp + 1, :]
        beats = (row > imp) | ((row == imp) & (j_idx > jp))
        cnt = cnt + beats.astype(jnp.int32)
    mask_rows = jnp.where(cnt < k_sel, 0.0, -MASK_BIG).astype(BF16)
    qa_ref[QK_LANES + MASK_ROW0:QK_LANES + MASK_ROW0 + n_blk, :] = _tile_lanes(mask_rows, nh)

    _flash_init(m_s, l_s, acc_s)

    def sel_chunk(c, carry):
        _flash_step(_chunk_scores(ks_ref[c], kaug_ref[0, c], qa_ref), vs_ref[c], m_s, l_s, acc_s)
        return carry
    lax.fori_loop(0, i, sel_chunk, 0)
    s = _chunk_scores(ks_ref[i], kaug_ref[0, i], qa_ref)
    _flash_step(jnp.where(causal, s, NEG_INF), vs_ref[i], m_s, l_s, acc_s)

    _flash_init(m_w, l_w, acc_w)
    n_back = WINDOW // CH
    for back in range(n_back, 0, -1):
        @pl.when(i >= back)
        def _():
            c = i - back
            s = _chunk_scores(kw_ref[c], kaug_ref[1, c], qa_ref)
            if back == n_back:
                s = jnp.where(window_edge, s, NEG_INF)
            _flash_step(s, vw_ref[c], m_w, l_w, acc_w)
    s = _chunk_scores(kw_ref[i], kaug_ref[1, i], qa_ref)
    _flash_step(jnp.where(causal, s, NEG_INF), vw_ref[i], m_w, l_w, acc_w)

    def gate(branch):
        rows = [ng_ref[0, branch * nh + hh:branch * nh + hh + 1, :] for hh in range(nh)]
        return jax.nn.sigmoid(jnp.concatenate(rows, axis=1))
    out = (gate(0) * ocmp_ref[...] + gate(1) * (acc_s[...] / l_s[...]) + gate(2) * (acc_w[...] / l_w[...]))
    for hh in range(nh):
        o_ref[0, hh * HEAD_DIM:(hh + 1) * HEAD_DIM, :] = out[:, hh * CH:(hh + 1) * CH].astype(o_ref.dtype)


def _nsa(nq_t, srows, ks3, kw3, vs_t, vw_t, kaug, kc, kcaug, vc_t, ng_t, asel_t, bsz, nq):
    n_cmp = kc.shape[1]
    nh = NSA_GROUP_SIZE
    gh = nh * HEAD_DIM
    cols = nh * CH
    return pl.pallas_call(
        _nsa_kernel,
        grid=(bsz, nq, NSA_KV_GROUPS),
        in_specs=[
            pl.BlockSpec((1, gh, CH), lambda b, i, g: (b * nq + i, g, 0)),
            pl.BlockSpec((1, ALIBI_ROWS, cols), lambda b, i, g: (g, 0, 0)),
            pl.BlockSpec((nq, CH, NSA_KV_COLS), lambda b, i, g: (b, 0, 0)),
            pl.BlockSpec((nq, CH, NSA_KV_COLS), lambda b, i, g: (b, 0, 0)),
            pl.BlockSpec((nq, HEAD_DIM, CH), lambda b, i, g: (b, g, 0)),
            pl.BlockSpec((nq, HEAD_DIM, CH), lambda b, i, g: (b, g, 0)),
            pl.BlockSpec(kaug.shape, lambda b, i, g: (0, 0, 0, 0)),
            pl.BlockSpec((1, n_cmp, NSA_KV_COLS), lambda b, i, g: (b, 0, 0)),
            pl.BlockSpec(kcaug.shape, lambda b, i, g: (0, 0)),
            pl.BlockSpec((1, HEAD_DIM, n_cmp), lambda b, i, g: (b, g, 0)),
            pl.BlockSpec((1, GATE_ROWS_PER_GROUP, CH), lambda b, i, g: (b * nq + i, g, 0)),
            pl.BlockSpec(asel_t.shape, lambda b, i, g: (0, 0)),
        ],
        out_specs=pl.BlockSpec((1, gh, CH), lambda b, i, g: (b * nq + i, g, 0)),
        out_shape=jax.ShapeDtypeStruct((bsz * nq, NSA_Q_COLS, CH), BF16),
        scratch_shapes=[
            pltpu.VMEM((QK_LANES + AUG_LANES, cols), BF16),
            pltpu.VMEM((HEAD_DIM, cols), F32),
            pltpu.VMEM((1, cols), F32), pltpu.VMEM((1, cols), F32), pltpu.VMEM((HEAD_DIM, cols), F32),
            pltpu.VMEM((1, cols), F32), pltpu.VMEM((1, cols), F32), pltpu.VMEM((HEAD_DIM, cols), F32),
        ],
        compiler_params=_params(("parallel", "parallel", "arbitrary")),
        name="nsa_attention",
    )(nq_t, srows, ks3, kw3, vs_t, vw_t, kaug, kc, kcaug, vc_t, ng_t, asel_t)


def _diff_kernel(scal_ref, q_ref, srow_ref, k_ref, v_ref, kaug_ref, lam_ref, gain_ref, o_ref,
                 qa_ref, m_r, l_r, acc_r):
    i = pl.program_id(2)
    lam_init = scal_ref[0]
    out_scale = scal_ref[1]
    sub2 = lax.broadcasted_iota(jnp.int32, (QK_LANES, CH), 0)
    q = q_ref[0]
    zero = jnp.zeros((), BF16)
    qa_ref[0:QK_LANES, 0:CH] = jnp.where(sub2 < HEAD_DIM, q, zero)
    qa_ref[0:QK_LANES, CH:2 * CH] = jnp.where(sub2 >= HEAD_DIM, q, zero)
    qa_ref[QK_LANES:QK_LANES + ALIBI_ROWS, :] = srow_ref[0]
    qa_ref[QK_LANES + MASK_ROW0:, :] = jnp.zeros((AUG_LANES - MASK_ROW0, 2 * CH), BF16)
    causal = _query_minus_key(2) >= 0

    _flash_init(m_r, l_r, acc_r)

    def chunk(c, carry):
        _flash_step(_chunk_scores(k_ref[c], kaug_ref[c], qa_ref), v_ref[c], m_r, l_r, acc_r)
        return carry
    lax.fori_loop(0, i, chunk, 0)
    s = _chunk_scores(k_ref[i], kaug_ref[i], qa_ref)
    _flash_step(jnp.where(causal, s, NEG_INF), v_ref[i], m_r, l_r, acc_r)

    lp = lam_ref[...]
    lam = (jnp.exp(jnp.sum(lp[0:1] * lp[1:2], axis=1, keepdims=True))
           - jnp.exp(jnp.sum(lp[2:3] * lp[3:4], axis=1, keepdims=True)) + lam_init)
    att = acc_r[...] / l_r[...]
    o = att[:, 0:CH] - lam * att[:, CH:2 * CH]
    o = o * lax.rsqrt(jnp.mean(o * o, axis=0, keepdims=True) + SUBLN_EPS) * gain_ref[...]
    o_ref[0] = (o * out_scale).astype(o_ref.dtype)


def _diff(scal, dq_t, srows, dk3, dv_t, kaug_plain, lam_p, gain, bsz, nq):
    hd2 = 2 * HEAD_DIM
    cols = 2 * CH
    grid_spec = pltpu.PrefetchScalarGridSpec(
        num_scalar_prefetch=1,
        grid=(bsz, DIFF_HEADS, nq),
        in_specs=[
            pl.BlockSpec((1, hd2, CH), lambda b, h, i, s: (b * nq + i, h, 0)),
            pl.BlockSpec((1, ALIBI_ROWS, cols), lambda b, h, i, s: (h, 0, 0)),
            pl.BlockSpec((nq, CH, hd2), lambda b, h, i, s: (b, 0, h)),
            pl.BlockSpec((nq, hd2, CH), lambda b, h, i, s: (b, h, 0)),
            pl.BlockSpec(kaug_plain.shape, lambda b, h, i, s: (0, 0, 0)),
            pl.BlockSpec(lam_p.shape, lambda b, h, i, s: (0, 0)),
            pl.BlockSpec(gain.shape, lambda b, h, i, s: (0, 0)),
        ],
        out_specs=pl.BlockSpec((1, hd2, CH), lambda b, h, i, s: (b * nq + i, h, 0)),
        scratch_shapes=[
            pltpu.VMEM((QK_LANES + AUG_LANES, cols), BF16),
            pltpu.VMEM((1, cols), F32), pltpu.VMEM((1, cols), F32), pltpu.VMEM((hd2, cols), F32),
        ],
    )
    return pl.pallas_call(
        _diff_kernel,
        grid_spec=grid_spec,
        out_shape=jax.ShapeDtypeStruct((bsz * nq, DIFF_V_COLS, CH), BF16),
        compiler_params=_params(("parallel", "parallel", "arbitrary")),
        name="diff_attention",
    )(scal, dq_t, srows, dk3, dv_t, kaug_plain, lam_p, gain)


def _merge_kernel(ya_ref, yb_ref, gates_ref, h_ref, wa_ref, wb_ref, wo_ref, o_ref):
    for j in range(ya_ref.shape[0]):
        rows = slice(j * CH, (j + 1) * CH)
        a = lax.dot_general(ya_ref[j], wa_ref[...], _TN, preferred_element_type=F32)
        b = lax.dot_general(yb_ref[j], wb_ref[...], _TN, preferred_element_type=F32)
        ga = jax.nn.sigmoid(gates_ref[rows, :D_MODEL].astype(F32))
        gb = jax.nn.sigmoid(gates_ref[rows, D_MODEL:].astype(F32))
        merged = (ga * a + gb * b).astype(BF16)
        o_ref[rows, :] = h_ref[rows, :] + _dot(merged, wo_ref[...])


def _merge(ya_t, yb_t, gates, h2d, wa, wb, wo):
    n = h2d.shape[0]
    tm = ROW_TILE
    return pl.pallas_call(
        _merge_kernel,
        grid=(n // tm,),
        in_specs=[pl.BlockSpec((tm // CH, NSA_Q_COLS, CH), lambda i: (i, 0, 0)),
                  pl.BlockSpec((tm // CH, DIFF_V_COLS, CH), lambda i: (i, 0, 0)),
                  pl.BlockSpec((tm, 2 * D_MODEL), lambda i: (i, 0)),
                  pl.BlockSpec((tm, D_MODEL), lambda i: (i, 0)),
                  _const_spec(wa.shape), _const_spec(wb.shape), _const_spec(wo.shape)],
        out_specs=pl.BlockSpec((tm, D_MODEL), lambda i: (i, 0)),
        out_shape=jax.ShapeDtypeStruct((n, D_MODEL), F32),
        compiler_params=_params(("parallel",)),
        name="merge_outproj",
    )(ya_t, yb_t, gates, h2d, wa, wb, wo)


def _first_argmax(x, rows, n):
    mx = jnp.max(x, axis=0, keepdims=True)
    idx = jnp.min(jnp.where(x == mx, rows, n), axis=0, keepdims=True)
    return mx, idx


def _moe_kernel(h_ref, g_ref, wr_ref, br_ref, eexp_ref, wg_ref, wu_ref, wd_ref, fg_ref, o_ref, *, final):
    hres = h_ref[...]
    xf = hres * lax.rsqrt(jnp.mean(hres * hres, axis=-1, keepdims=True) + RMS_EPS) * g_ref[...]
    xb = xf.astype(BF16)
    tm = hres.shape[0]

    logits = lax.dot_general(wr_ref[...], xf, _NT, precision=lax.Precision.HIGHEST,
                             preferred_element_type=F32) + br_ref[...]
    gl = logits[0:MOE_GROUPS]
    rows_g = lax.broadcasted_iota(jnp.int32, (MOE_GROUPS, tm), 0)
    gmax, gidx = _first_argmax(gl, rows_g, MOE_GROUPS)
    g_w = 1.0 / jnp.sum(jnp.exp(gl - gmax), axis=0, keepdims=True)
    esel = jnp.zeros((EXPERTS_PER_GROUP, tm), F32)
    for gg in range(MOE_GROUPS):
        lo = MOE_GROUPS + gg * EXPERTS_PER_GROUP
        esel = jnp.where(gidx == gg, logits[lo:lo + EXPERTS_PER_GROUP], esel)
    rows_e = lax.broadcasted_iota(jnp.int32, (EXPERTS_PER_GROUP, tm), 0)
    v1, i1 = _first_argmax(esel, rows_e, EXPERTS_PER_GROUP)
    rest = jnp.where(rows_e == i1, -jnp.inf, esel)
    v2, i2 = _first_argmax(rest, rows_e, EXPERTS_PER_GROUP)
    e21 = jnp.exp(v2 - v1)
    w1 = g_w / (1.0 + e21)
    w2 = g_w * e21 / (1.0 + e21)
    rows_c = lax.broadcasted_iota(jnp.int32, (N_EXPERTS, tm), 0)
    grp_c = jnp.right_shift(rows_c, EXPERTS_PER_GROUP.bit_length() - 1)
    exp_c = jnp.bitwise_and(rows_c, EXPERTS_PER_GROUP - 1)
    comb = jnp.where(grp_c == gidx,
                     jnp.where(exp_c == i1, w1, 0.0) + jnp.where(exp_c == i2, w2, 0.0), 0.0)
    comb_hi = comb.astype(BF16)
    comb_lo = (comb - comb_hi.astype(F32)).astype(BF16)

    acc = hres
    n_ff = wg_ref.shape[1]
    step = 512
    for c in range(0, n_ff, step):
        cols = slice(c, c + step)
        cexp = (lax.dot_general(comb_hi, eexp_ref[:, cols], _TN, preferred_element_type=F32)
                + lax.dot_general(comb_lo, eexp_ref[:, cols], _TN, preferred_element_type=F32))
        hg = _dot(xb, wg_ref[:, cols])
        hu = _dot(xb, wu_ref[:, cols])
        act = (jax.nn.silu(hg) * hu * cexp).astype(BF16)
        acc = acc + _dot(act, wd_ref[cols, :])
    if final:
        acc = acc * lax.rsqrt(jnp.mean(acc * acc, axis=-1, keepdims=True) + RMS_EPS) * fg_ref[...]
    o_ref[...] = acc


def _moe(h2d, g, wr_t, br, eexp, wg, wu, wd, fg, final):
    n = h2d.shape[0]
    tm = ROW_TILE
    return pl.pallas_call(
        functools.partial(_moe_kernel, final=final),
        grid=(n // tm,),
        in_specs=[pl.BlockSpec((tm, D_MODEL), lambda i: (i, 0)),
                  _const_spec(g.shape), _const_spec(wr_t.shape), _const_spec(br.shape),
                  _const_spec(eexp.shape), _const_spec(wg.shape), _const_spec(wu.shape),
                  _const_spec(wd.shape), _const_spec(fg.shape)],
        out_specs=pl.BlockSpec((tm, D_MODEL), lambda i: (i, 0)),
        out_shape=jax.ShapeDtypeStruct((n, D_MODEL), F32),
        compiler_params=_params(("parallel",)),
        name="moe_final" if final else "moe",
    )(h2d, g, wr_t, br, eexp, wg, wu, wd, fg)


def _split_points():
    sizes = ([NSA_Q_COLS] + [NSA_KV_COLS] * 6
             + [NSA_GATE_COLS, DIFF_QK_COLS, DIFF_QK_COLS, DIFF_V_COLS, D_MODEL, D_MODEL])
    return [int(v) for v in np.cumsum(sizes)[:-1]]


def _cmp_to_sel_t(n_rows, nc, nb):
    c0 = np.arange(nc)[:, None] * CMP_STRIDE
    s0 = np.arange(nb)[None, :] * SEL_BLOCK
    ov = np.maximum(0, np.minimum(c0 + CMP_BLOCK, s0 + SEL_BLOCK) - np.maximum(c0, s0)) / CMP_BLOCK
    out = np.zeros((nb, n_rows), np.float32)
    out[:, :nc] = ov.T
    return out


def kernel(x, norm1_g, w_in, cmp_pe, cmp_w1, cmp_b1, cmp_w2, cmp_b2, diff_lambda, diff_subln_g, w_branch_a, w_branch_b, w_out, norm2_g, router_grp_w, router_grp_b, router_exp_w, router_exp_b, exp_w_gate, exp_w_up, exp_w_down, final_norm_g):
    bsz, seq, d = x.shape
    depth = w_in.shape[0]
    n = bsz * seq
    nq = seq // CH
    n_half = seq // CMP_STRIDE
    nc = (seq - CMP_BLOCK) // CMP_STRIDE + 1
    nb = seq // SEL_BLOCK
    assert d == D_MODEL and seq % ROW_TILE == 0 and seq >= WINDOW and WINDOW % CH == 0
    assert nb % 16 == 0 and MASK_ROW0 + nb <= AUG_LANES and n_half <= 256
    g_kv = NSA_KV_GROUPS
    eye_g = jnp.eye(g_kv, dtype=F32)
    eye_c = jnp.eye(2, dtype=F32)

    (nq_w, kc_w, vc_w, ks_w, vs_w, kw_w, vw_w, ng_w, dq_w, dk_w, dv_w, ga_w, gb_w) = jnp.split(
        w_in, _split_points(), axis=-1)
    ng_w = ng_w.reshape(depth, d, g_kv, NSA_GROUP_SIZE, 3).transpose(0, 1, 2, 4, 3)
    ng_w = ng_w.reshape(depth, d, g_kv, 3 * NSA_GROUP_SIZE)
    ng_w = jnp.pad(ng_w, ((0, 0), (0, 0), (0, 0), (0, GATE_ROWS_PER_GROUP - 3 * NSA_GROUP_SIZE)))
    ng_w = ng_w.reshape(depth, d, g_kv * GATE_ROWS_PER_GROUP)
    wn_all = jnp.concatenate([ks_w, kw_w, dk_w, kc_w, vc_w, ga_w, gb_w], axis=-1).astype(BF16)
    wt_all = jnp.concatenate([nq_w, dq_w, vs_w, vw_w, dv_w, ng_w], axis=-1)
    wt_all = jnp.swapaxes(wt_all, 1, 2).astype(BF16)

    w1r = cmp_w1.reshape(depth, 2, 2, CMP_STRIDE, HEAD_DIM, CMP_HIDDEN)
    w1_big = jnp.einsum('Lchldf,cC,gG->LhlcgdCGf', w1r, eye_c, eye_g)
    w1_big = w1_big.reshape(depth, 2, CMP_STRIDE * 2 * NSA_KV_COLS, 2 * g_kv * CMP_HIDDEN).astype(BF16)
    per = cmp_pe.reshape(depth, 2, 2, CMP_STRIDE, HEAD_DIM)
    pe_hb = jnp.einsum('Lchld,g->Lhlcgd', per, jnp.ones((g_kv,), F32))
    pe_hb = pe_hb.reshape(depth, 2, 1, CMP_STRIDE * 2 * NSA_KV_COLS)
    b1p = jnp.broadcast_to(cmp_b1[:, :, None, :], (depth, 2, g_kv, CMP_HIDDEN)).reshape(depth, 1, -1)
    w2k = jnp.einsum('Lfd,gG->LgfGd', cmp_w2[:, 0], eye_g).reshape(depth, g_kv * CMP_HIDDEN, NSA_KV_COLS)
    w2vt = jnp.einsum('Lfd,gG->LGdgf', cmp_w2[:, 1], eye_g).reshape(depth, NSA_KV_COLS, g_kv * CMP_HIDDEN)
    w2k = w2k.astype(BF16)
    w2vt = w2vt.astype(BF16)
    b2k = jnp.tile(cmp_b2[:, 0], (1, g_kv))[:, None, :]
    b2v = jnp.tile(cmp_b2[:, 1], (1, g_kv))[:, :, None]

    slopes = _alibi_slopes()
    nsa_srows = _slope_rows(slopes[:NSA_HEADS], NSA_GROUP_SIZE, CH)
    diff_srows = _slope_rows(np.repeat(slopes[NSA_HEADS:], 2), 2, CH)
    kaug = _key_aug_tables(seq, nb)
    kcaug = _cmp_aug_table(n_half)
    asel_t = jnp.asarray(_cmp_to_sel_t(n_half, nc, nb))

    wa_all = w_branch_a.astype(BF16)
    wb_all = w_branch_b.astype(BF16)
    wo_all = w_out.astype(BF16)

    wr = jnp.concatenate([router_grp_w, router_exp_w.reshape(depth, d, N_EXPERTS)], axis=-1)
    n_r = MOE_GROUPS + N_EXPERTS
    wr_t = jnp.pad(jnp.swapaxes(wr, 1, 2), ((0, 0), (0, 32 - n_r), (0, 0)))
    br = jnp.concatenate([router_grp_b, router_exp_b.reshape(depth, N_EXPERTS)], axis=-1)
    br = jnp.pad(br, ((0, 0), (0, 32 - n_r)))[:, :, None]
    eexp = jnp.asarray(np.kron(np.eye(N_EXPERTS, dtype=np.float32), np.ones((1, EXPERT_FF), np.float32)), BF16)
    wg_all = jnp.swapaxes(exp_w_gate, 1, 2).reshape(depth, d, N_EXPERTS * EXPERT_FF).astype(BF16)
    wu_all = jnp.swapaxes(exp_w_up, 1, 2).reshape(depth, d, N_EXPERTS * EXPERT_FF).astype(BF16)
    wd_all = exp_w_down.reshape(depth, N_EXPERTS * EXPERT_FF, d).astype(BF16)

    h = x.reshape(n, d)
    for l in range(depth):
        (ks, kw, dk, kcvc, gates, nq_t, dq_t, vs_t, vw_t, dv_t, ng_t) = _inproj(
            h, norm1_g[l][None, :], wn_all[l], wt_all[l])
        hb = kcvc.reshape(bsz, n_half, CMP_STRIDE * 2 * NSA_KV_COLS)
        kc, vc_t = _compress(hb, pe_hb[l], w1_big[l], b1p[l], w2k[l], b2k[l], w2vt[l], b2v[l])
        ya_t = _nsa(nq_t, nsa_srows, ks.reshape(n // CH, CH, NSA_KV_COLS), kw.reshape(n // CH, CH, NSA_KV_COLS),
                    vs_t, vw_t, kaug, kc, kcaug, vc_t, ng_t, asel_t, bsz, nq)
        lam_init = 0.8 - 0.6 * float(np.exp(-0.3 * l))
        scal = jnp.asarray([lam_init, 1.0 - lam_init], F32)
        yb_t = _diff(scal, dq_t, diff_srows, dk.reshape(n // CH, CH, DIFF_QK_COLS), dv_t, kaug[1],
                     diff_lambda[l], diff_subln_g[l][:, None], bsz, nq)
        h = _merge(ya_t, yb_t, gates, h, wa_all[l], wb_all[l], wo_all[l])
        h = _moe(h, norm2_g[l][None, :], wr_t[l], br[l], eexp, wg_all[l], wu_all[l], wd_all[l],
                 final_norm_g[None, :], final=(l == depth - 1))
    return h.reshape(bsz, seq, d)
```

```python
import functools

import numpy as np
import jax
import jax.numpy as jnp
from jax import lax
from jax.experimental import pallas as pl
from jax.experimental.pallas import tpu as pltpu

F32 = jnp.float32
BF16 = jnp.bfloat16

D_MODEL = 1024
HEAD_DIM = 64
NSA_HEADS = 8
NSA_KV_GROUPS = 2
NSA_GROUP_SIZE = NSA_HEADS // NSA_KV_GROUPS
CMP_BLOCK = 32
CMP_STRIDE = 16
CMP_HIDDEN = 128
SEL_BLOCK = 64
SEL_TOPK = 8
WINDOW = 512
FORCED_SCORE = 1e9
DIFF_HEADS = 4
MOE_GROUPS = 4
EXPERTS_PER_GROUP = 4
N_EXPERTS = MOE_GROUPS * EXPERTS_PER_GROUP
EXPERT_FF = D_MODEL // 8
RMS_EPS = 1e-6
SUBLN_EPS = 1e-5
NEG_INF = -1e30
N_ALIBI_HEADS = NSA_HEADS + DIFF_HEADS

NSA_Q_COLS = NSA_HEADS * HEAD_DIM
NSA_KV_COLS = NSA_KV_GROUPS * HEAD_DIM
NSA_GATE_COLS = 3 * NSA_HEADS
DIFF_QK_COLS = DIFF_HEADS * 2 * HEAD_DIM
DIFF_V_COLS = DIFF_HEADS * 2 * HEAD_DIM
GATE_ROWS_PER_GROUP = 16

CH = 256
ROW_TILE = 512
VMEM_LIMIT = 56 * 1024 * 1024

LOG2E = float(np.log2(np.e))
Q_SCALE = HEAD_DIM ** -0.5 * LOG2E

QK_LANES = 2 * HEAD_DIM
AUG_LANES = 128
SLOPE_PIECES = 3
PAD_ROW = 2 * SLOPE_PIECES
ALIBI_ROWS = 16
MASK_ROW0 = ALIBI_ROWS
MASK_BIG = 1e30
SUM_ROWS = 16

_NT = (((1,), (1,)), ((), ()))
_TN = (((0,), (0,)), ((), ()))


def _dot(a, b):
    return jnp.dot(a, b, preferred_element_type=F32)


def _const_spec(shape):
    nd = len(shape)
    return pl.BlockSpec(shape, lambda *_: (0,) * nd, pipeline_mode=pl.Buffered(1))


def _params(sem):
    return pltpu.CompilerParams(dimension_semantics=sem, vmem_limit_bytes=VMEM_LIMIT)


def _alibi_slopes():
    return 2.0 ** (-8.0 * np.arange(1, N_ALIBI_HEADS + 1) / N_ALIBI_HEADS)


_NAT_WIDTHS = (NSA_KV_COLS, NSA_KV_COLS, DIFF_QK_COLS, 2 * NSA_KV_COLS, 2 * D_MODEL)
_TR_ROWS = (NSA_Q_COLS, DIFF_QK_COLS, NSA_KV_COLS, NSA_KV_COLS, DIFF_V_COLS, 2 * GATE_ROWS_PER_GROUP)
_TR_SCALE = (Q_SCALE, Q_SCALE, 1.0, 1.0, 1.0, 1.0)


def _inproj_kernel(x_ref, g_ref, wn_ref, wt_ref, *out_refs):
    nat_refs = out_refs[:len(_NAT_WIDTHS)]
    tr_refs = out_refs[len(_NAT_WIDTHS):]
    x = x_ref[...]
    xn = (x * lax.rsqrt(jnp.mean(x * x, axis=-1, keepdims=True) + RMS_EPS) * g_ref[...]).astype(BF16)
    off = 0
    for ref, width in zip(nat_refs, _NAT_WIDTHS):
        for c in range(0, width, 512):
            cw = min(512, width - c)
            ref[:, c:c + cw] = _dot(xn, wn_ref[:, off + c:off + c + cw]).astype(ref.dtype)
        off += width
    n_sub = x.shape[0] // CH
    off = 0
    for ref, rows, scale in zip(tr_refs, _TR_ROWS, _TR_SCALE):
        for c in range(0, rows, 256):
            rw = min(256, rows - c)
            res = lax.dot_general(wt_ref[off + c:off + c + rw, :], xn, _NT, preferred_element_type=F32)
            if scale != 1.0:
                res = res * scale
            for j in range(n_sub):
                ref[j, c:c + rw, :] = res[:, j * CH:(j + 1) * CH].astype(ref.dtype)
        off += rows


def _inproj(h2d, g, wn, wt):
    n = h2d.shape[0]
    tm = ROW_TILE
    nat_dtypes = (BF16, BF16, BF16, F32, BF16)
    tr_dtypes = (BF16, BF16, BF16, BF16, BF16, F32)
    out_shape = [jax.ShapeDtypeStruct((n, w), dt) for w, dt in zip(_NAT_WIDTHS, nat_dtypes)]
    out_shape += [jax.ShapeDtypeStruct((n // CH, r, CH), dt) for r, dt in zip(_TR_ROWS, tr_dtypes)]
    out_specs = [pl.BlockSpec((tm, w), lambda i: (i, 0)) for w in _NAT_WIDTHS]
    out_specs += [pl.BlockSpec((tm // CH, r, CH), lambda i: (i, 0, 0)) for r in _TR_ROWS]
    return pl.pallas_call(
        _inproj_kernel,
        grid=(n // tm,),
        in_specs=[pl.BlockSpec((tm, D_MODEL), lambda i: (i, 0)),
                  _const_spec((1, D_MODEL)),
                  _const_spec(wn.shape),
                  _const_spec(wt.shape)],
        out_specs=out_specs,
        out_shape=out_shape,
        compiler_params=_params(("parallel",)),
        name="inproj",
    )(h2d, g, wn, wt)


def _compress_kernel(hb_ref, pe_ref, w1_ref, b1_ref, w2k_ref, b2k_ref, w2v_ref, b2v_ref, kc_ref, vct_ref):
    hb = hb_ref[0]
    rows = hb.shape[0]
    top = (hb + pe_ref[0]).astype(BF16)
    bot = (hb + pe_ref[1]).astype(BF16)
    p = _dot(top, w1_ref[0])
    q = _dot(bot, w1_ref[1])
    q_next = pltpu.roll(q, rows - 1, 0)
    hid = jax.nn.gelu(p + q_next + b1_ref[...])
    width = hid.shape[1] // 2
    kc_ref[0] = (_dot(hid[:, :width].astype(BF16), w2k_ref[...]) + b2k_ref[...]).astype(kc_ref.dtype)
    vct = lax.dot_general(w2v_ref[...], hid[:, width:].astype(BF16), _NT, preferred_element_type=F32)
    vct_ref[0] = (vct + b2v_ref[...]).astype(vct_ref.dtype)


def _compress(hb, pe_hb, w1_big, b1p, w2k, b2k, w2vt, b2v):
    bsz, rows, width = hb.shape
    gk = NSA_KV_COLS
    return pl.pallas_call(
        _compress_kernel,
        grid=(bsz,),
        in_specs=[pl.BlockSpec((1, rows, width), lambda b: (b, 0, 0)),
                  _const_spec(pe_hb.shape), _const_spec(w1_big.shape), _const_spec(b1p.shape),
                  _const_spec(w2k.shape), _const_spec(b2k.shape), _const_spec(w2vt.shape),
                  _const_spec(b2v.shape)],
        out_specs=[pl.BlockSpec((1, rows, gk), lambda b: (b, 0, 0)),
                   pl.BlockSpec((1, gk, rows), lambda b: (b, 0, 0))],
        out_shape=[jax.ShapeDtypeStruct((bsz, rows, gk), BF16),
                   jax.ShapeDtypeStruct((bsz, gk, rows), BF16)],
        compiler_params=_params(("parallel",)),
        name="compress",
    )(hb, pe_hb, w1_big, b1p, w2k, b2k, w2vt, b2v)


def _bf16_pieces(x):
    out = []
    rest = np.asarray(x, np.float32)
    for _ in range(SLOPE_PIECES):
        piece = rest.astype(BF16).astype(np.float32)
        out.append(piece)
        rest = rest - piece
    return out


def _slope_rows(slopes, heads_per_block, cols_per_head):
    sl2 = (np.asarray(slopes, np.float32).astype(np.float64) * LOG2E).astype(np.float32)
    pieces = np.stack(_bf16_pieces(sl2) * 2, axis=0)
    rows = np.zeros((ALIBI_ROWS, sl2.shape[0]), np.float32)
    rows[:pieces.shape[0]] = pieces
    rows[PAD_ROW] = -MASK_BIG
    rows = np.repeat(rows, cols_per_head, axis=1)
    rows = rows.reshape(ALIBI_ROWS, -1, heads_per_block * cols_per_head).transpose(1, 0, 2)
    return jnp.asarray(rows, BF16)


def _key_aug_tables(seq, nb):
    pos = np.arange(seq)
    aug = np.zeros((2, seq + CH, AUG_LANES), np.float32)
    aug[:, :seq, 0:SLOPE_PIECES] = (pos % CH)[None, :, None]
    aug[:, :seq, SLOPE_PIECES:2 * SLOPE_PIECES] = (pos // CH * CH)[None, :, None]
    aug[0, pos, MASK_ROW0 + pos // SEL_BLOCK] = 1.0
    aug[:, seq:, PAD_ROW] = 1.0
    return jnp.asarray(aug.reshape(2, seq // CH + 1, CH, AUG_LANES), BF16)


def _cmp_aug_table(n_rows):
    aug = np.zeros((n_rows, AUG_LANES), np.float32)
    aug[:, 0:SLOPE_PIECES] = (np.arange(n_rows) * CMP_STRIDE)[:, None]
    aug[:, SLOPE_PIECES:2 * SLOPE_PIECES] = CMP_BLOCK - 1
    return jnp.asarray(aug, BF16)


def _tile_lanes(x, reps):
    return jnp.concatenate([x] * reps, axis=1)


def _query_minus_key(reps):
    shape = (CH, reps * CH)
    q_off = jnp.bitwise_and(lax.broadcasted_iota(jnp.int32, shape, 1), CH - 1)
    return q_off - lax.broadcasted_iota(jnp.int32, shape, 0)


def _flash_init(m_ref, acc_ref):
    m_ref[...] = jnp.full(m_ref.shape, NEG_INF, F32)
    acc_ref[...] = jnp.zeros(acc_ref.shape, F32)


def _normalized(acc_ref):
    dv = acc_ref.shape[0] - SUM_ROWS
    return acc_ref[0:dv, :] / acc_ref[dv:dv + 1, :]


def _chunk_scores(k_blk, aug_blk, qa_ref):
    return _dot(jnp.concatenate([k_blk, aug_blk], axis=1), qa_ref[...])


def _produce(k_blk, aug_blk, qa_ref, buf, mask=None):
    s_ref, mx_ref = buf
    s = _chunk_scores(k_blk, aug_blk, qa_ref)
    if mask is not None:
        s = jnp.where(mask, s, NEG_INF)
    s_ref[...] = s
    mx_ref[...] = jnp.max(s, axis=0, keepdims=True)


def _consume(buf, v_t, state, mask=None):
    s_ref, mx_ref = buf
    m_ref, acc_ref = state
    s = s_ref[...]
    if mask is None:
        mx = mx_ref[...]
    else:
        s = jnp.where(mask, s, NEG_INF)
        mx = jnp.max(s, axis=0, keepdims=True)
    m_prev = m_ref[...]
    m_new = jnp.maximum(m_prev, mx)
    alpha = jnp.exp2(m_prev - m_new)
    p = jnp.exp2(s - m_new)
    v_ones = jnp.concatenate([v_t, jnp.ones((SUM_ROWS, v_t.shape[1]), BF16)], axis=0)
    acc_ref[...] = alpha * acc_ref[...] + _dot(v_ones, p.astype(BF16))
    m_ref[...] = m_new


def _pad_or(aug_ref, c, is_pad):
    return aug_ref[jnp.where(is_pad, aug_ref.shape[0] - 1, c)]


def _causal_first(i, k_ref, aug_ref, qa_ref, buf0):
    _produce(k_ref[0], _pad_or(aug_ref, 0, jnp.bitwise_and(i, 1) == 1), qa_ref, buf0)


def _causal_pairs(i, k_ref, v_ref, aug_ref, qa_ref, bufs, state):
    buf0, buf1 = bufs
    pad = jnp.bitwise_and(i, 1)

    def pair(k, carry):
        c = 2 * k - pad
        _produce(k_ref[c + 1], aug_ref[c + 1], qa_ref, buf1)
        _consume(buf0, v_ref[jnp.maximum(c, 0)], state)
        _produce(k_ref[c + 2], aug_ref[c + 2], qa_ref, buf0)
        _consume(buf1, v_ref[c + 1], state)
        return carry
    lax.fori_loop(0, jnp.right_shift(i + pad, 1), pair, 0)


def _nsa_kernel(q_ref, srow_ref, ks_ref, kw_ref, vs_ref, vw_ref, kaug_ref, kc_ref, kcaug_ref, vc_ref,
                ng_ref, asel_ref, o_ref, qa_ref, ocmp_ref, s0, x0, s1, x1, w0, y0, w1, y1, w2, y2,
                m_s, acc_s, m_w, acc_w):
    i = pl.program_id(1)
    g = pl.program_id(2)
    t0 = i * CH
    nh = NSA_GROUP_SIZE
    n_cmp = kc_ref.shape[1]
    n_blk = asel_ref.shape[0]
    k_sel = min(SEL_TOPK, n_blk)
    bufs_s = ((s0, x0), (s1, x1))
    bufs_w = ((w0, y0), (w1, y1), (w2, y2))
    state_s = (m_s, acc_s)
    state_w = (m_w, acc_w)
    kaug_s = kaug_ref.at[0]
    kaug_w = kaug_ref.at[1]

    t_pos = t0 + lax.broadcasted_iota(jnp.int32, (1, CH), 1)
    d0_i = _query_minus_key(nh)
    causal = d0_i >= 0
    window_edge = d0_i < 0

    sub2 = lax.broadcasted_iota(jnp.int32, (QK_LANES, CH), 0)
    half0 = g * HEAD_DIM
    place = (sub2 >= half0) & (sub2 < half0 + HEAD_DIM)
    for hh in range(nh):
        qh = q_ref[0, hh * HEAD_DIM:(hh + 1) * HEAD_DIM, :]
        qa_ref[0:QK_LANES, hh * CH:(hh + 1) * CH] = jnp.where(
            place, jnp.concatenate([qh, qh], axis=0), jnp.zeros((), BF16))
    qa_ref[QK_LANES:QK_LANES + ALIBI_ROWS, :] = srow_ref[0]
    qa_ref[QK_LANES + MASK_ROW0:, :] = jnp.zeros((AUG_LANES - MASK_ROW0, nh * CH), BF16)

    n_back = WINDOW // CH
    chunks_w = []
    for back in range(n_back, 0, -1):
        c = jnp.maximum(i - back, 0)
        _produce(kw_ref[c], _pad_or(kaug_w, c, i < back), qa_ref, bufs_w[n_back - back],
                 mask=window_edge if back == n_back else None)
        chunks_w.append(c)
    _produce(kw_ref[i], kaug_w[i], qa_ref, bufs_w[n_back], mask=causal)
    chunks_w.append(i)

    n_idx = lax.broadcasted_iota(jnp.int32, (n_cmp, nh * CH), 0)
    t_pos_all = t0 + jnp.bitwise_and(lax.broadcasted_iota(jnp.int32, (1, nh * CH), 1), CH - 1)
    valid_c = n_idx * CMP_STRIDE + (CMP_BLOCK - 1) <= t_pos_all
    lg = jnp.where(valid_c, _chunk_scores(kc_ref[0], kcaug_ref[...], qa_ref), NEG_INF)
    m = jnp.max(lg, axis=0, keepdims=True)
    p = jnp.where(valid_c, jnp.exp2(lg - m), 0.0)
    l = jnp.sum(p, axis=0, keepdims=True)
    pc = p * jnp.where(l > 0.0, 1.0 / l, 0.0)
    ocmp_ref[...] = _dot(vc_ref[0], pc.astype(BF16))
    psum = pc[:, 0:CH]
    for hh in range(1, nh):
        psum = psum + pc[:, hh * CH:(hh + 1) * CH]

    imp = jnp.dot(asel_ref[...], psum, precision=lax.Precision.HIGHEST, preferred_element_type=F32)
    j_idx = lax.broadcasted_iota(jnp.int32, (n_blk, CH), 0)
    cur = jnp.right_shift(t_pos, SEL_BLOCK.bit_length() - 1)
    forced = (j_idx == 0) | (j_idx == cur) | (j_idx == cur - 1)
    imp = jnp.where(forced, FORCED_SCORE, imp)
    imp = jnp.where(j_idx * SEL_BLOCK <= t_pos, imp, -1.0)
    rows_per = 8
    j_loc = lax.broadcasted_iota(jnp.int32, (rows_per, CH), 0)
    mask_blocks = []
    for r0 in range(0, n_blk, rows_per):
        blk = imp[r0:r0 + rows_per, :]
        cnt = jnp.zeros((rows_per, CH), jnp.int32)
        for jp in range(n_blk):
            row = imp[jp:jp + 1, :]
            gt = jnp.where(row > blk, 1, 0)
            ge = jnp.where(row >= blk, 1, 0)
            if jp >= r0 + rows_per - 1:
                cnt = cnt + gt
            elif jp < r0:
                cnt = cnt + ge
            else:
                cnt = cnt + jnp.where(j_loc + r0 > jp, ge, gt)
        mask_blocks.append(jnp.where(cnt < k_sel, 0.0, -MASK_BIG))
    mask_rows = jnp.concatenate(mask_blocks, axis=0).astype(BF16)
    qa_ref[QK_LANES + MASK_ROW0:QK_LANES + MASK_ROW0 + n_blk, :] = _tile_lanes(mask_rows, nh)

    _causal_first(i, ks_ref, kaug_s, qa_ref, bufs_s[0])

    _flash_init(*state_w)
    for buf, c in zip(bufs_w, chunks_w):
        _consume(buf, vw_ref[c], state_w)

    _flash_init(*state_s)
    _causal_pairs(i, ks_ref, vs_ref, kaug_s, qa_ref, bufs_s, state_s)
    _consume(bufs_s[0], vs_ref[i], state_s, mask=causal)

    def gate(branch):
        rows = [ng_ref[0, branch * nh + hh:branch * nh + hh + 1, :] for hh in range(nh)]
        return jax.nn.sigmoid(jnp.concatenate(rows, axis=1))
    out = gate(0) * ocmp_ref[...] + gate(1) * _normalized(acc_s) + gate(2) * _normalized(acc_w)
    for hh in range(nh):
        o_ref[0, hh * HEAD_DIM:(hh + 1) * HEAD_DIM, :] = out[:, hh * CH:(hh + 1) * CH].astype(o_ref.dtype)


def _nsa(nq_t, srows, ks3, kw3, vs_t, vw_t, kaug, kc, kcaug, vc_t, ng_t, asel_t, bsz, nq):
    n_cmp = kc.shape[1]
    nh = NSA_GROUP_SIZE
    gh = nh * HEAD_DIM
    cols = nh * CH
    return pl.pallas_call(
        _nsa_kernel,
        grid=(bsz, nq, NSA_KV_GROUPS),
        in_specs=[
            pl.BlockSpec((1, gh, CH), lambda b, i, g: (b * nq + i, g, 0)),
            pl.BlockSpec((1, ALIBI_ROWS, cols), lambda b, i, g: (g, 0, 0)),
            pl.BlockSpec((nq, CH, NSA_KV_COLS), lambda b, i, g: (b, 0, 0)),
            pl.BlockSpec((nq, CH, NSA_KV_COLS), lambda b, i, g: (b, 0, 0)),
            pl.BlockSpec((nq, HEAD_DIM, CH), lambda b, i, g: (b, g, 0)),
            pl.BlockSpec((nq, HEAD_DIM, CH), lambda b, i, g: (b, g, 0)),
            pl.BlockSpec(kaug.shape, lambda b, i, g: (0, 0, 0, 0)),
            pl.BlockSpec((1, n_cmp, NSA_KV_COLS), lambda b, i, g: (b, 0, 0)),
            pl.BlockSpec(kcaug.shape, lambda b, i, g: (0, 0)),
            pl.BlockSpec((1, HEAD_DIM, n_cmp), lambda b, i, g: (b, g, 0)),
            pl.BlockSpec((1, GATE_ROWS_PER_GROUP, CH), lambda b, i, g: (b * nq + i, g, 0)),
            pl.BlockSpec(asel_t.shape, lambda b, i, g: (0, 0)),
        ],
        out_specs=pl.BlockSpec((1, gh, CH), lambda b, i, g: (b * nq + i, g, 0)),
        out_shape=jax.ShapeDtypeStruct((bsz * nq, NSA_Q_COLS, CH), BF16),
        scratch_shapes=[
            pltpu.VMEM((QK_LANES + AUG_LANES, cols), BF16),
            pltpu.VMEM((HEAD_DIM, cols), F32),
            pltpu.VMEM((CH, cols), F32), pltpu.VMEM((1, cols), F32),
            pltpu.VMEM((CH, cols), F32), pltpu.VMEM((1, cols), F32),
            pltpu.VMEM((CH, cols), F32), pltpu.VMEM((1, cols), F32),
            pltpu.VMEM((CH, cols), F32), pltpu.VMEM((1, cols), F32),
            pltpu.VMEM((CH, cols), F32), pltpu.VMEM((1, cols), F32),
            pltpu.VMEM((1, cols), F32), pltpu.VMEM((HEAD_DIM + SUM_ROWS, cols), F32),
            pltpu.VMEM((1, cols), F32), pltpu.VMEM((HEAD_DIM + SUM_ROWS, cols), F32),
        ],
        compiler_params=_params(("parallel", "parallel", "arbitrary")),
        name="nsa_attention",
    )(nq_t, srows, ks3, kw3, vs_t, vw_t, kaug, kc, kcaug, vc_t, ng_t, asel_t)


def _diff_kernel(scal_ref, q_ref, srow_ref, k_ref, v_ref, kaug_ref, lam_ref, gain_ref, o_ref,
                 qa_ref, s0, x0, s1, x1, m_r, acc_r):
    i = pl.program_id(2)
    lam_init = scal_ref[0]
    out_scale = scal_ref[1]
    sub2 = lax.broadcasted_iota(jnp.int32, (QK_LANES, CH), 0)
    q = q_ref[0]
    zero = jnp.zeros((), BF16)
    qa_ref[0:QK_LANES, 0:CH] = jnp.where(sub2 < HEAD_DIM, q, zero)
    qa_ref[0:QK_LANES, CH:2 * CH] = jnp.where(sub2 >= HEAD_DIM, q, zero)
    qa_ref[QK_LANES:QK_LANES + ALIBI_ROWS, :] = srow_ref[0]
    qa_ref[QK_LANES + MASK_ROW0:, :] = jnp.zeros((AUG_LANES - MASK_ROW0, 2 * CH), BF16)
    causal = _query_minus_key(2) >= 0

    state = (m_r, acc_r)
    bufs = ((s0, x0), (s1, x1))
    _causal_first(i, k_ref, kaug_ref, qa_ref, bufs[0])
    _flash_init(*state)
    _causal_pairs(i, k_ref, v_ref, kaug_ref, qa_ref, bufs, state)
    _consume(bufs[0], v_ref[i], state, mask=causal)

    lp = lam_ref[...]
    lam = (jnp.exp(jnp.sum(lp[0:1] * lp[1:2], axis=1, keepdims=True))
           - jnp.exp(jnp.sum(lp[2:3] * lp[3:4], axis=1, keepdims=True)) + lam_init)
    att = _normalized(acc_r)
    o = att[:, 0:CH] - lam * att[:, CH:2 * CH]
    o = o * lax.rsqrt(jnp.mean(o * o, axis=0, keepdims=True) + SUBLN_EPS) * gain_ref[...]
    o_ref[0] = (o * out_scale).astype(o_ref.dtype)


def _diff(scal, dq_t, srows, dk3, dv_t, kaug_plain, lam_p, gain, bsz, nq):
    hd2 = 2 * HEAD_DIM
    cols = 2 * CH
    grid_spec = pltpu.PrefetchScalarGridSpec(
        num_scalar_prefetch=1,
        grid=(bsz, DIFF_HEADS, nq),
        in_specs=[
            pl.BlockSpec((1, hd2, CH), lambda b, h, i, s: (b * nq + i, h, 0)),
            pl.BlockSpec((1, ALIBI_ROWS, cols), lambda b, h, i, s: (h, 0, 0)),
            pl.BlockSpec((nq, CH, hd2), lambda b, h, i, s: (b, 0, h)),
            pl.BlockSpec((nq, hd2, CH), lambda b, h, i, s: (b, h, 0)),
            pl.BlockSpec(kaug_plain.shape, lambda b, h, i, s: (0, 0, 0)),
            pl.BlockSpec(lam_p.shape, lambda b, h, i, s: (0, 0)),
            pl.BlockSpec(gain.shape, lambda b, h, i, s: (0, 0)),
        ],
        out_specs=pl.BlockSpec((1, hd2, CH), lambda b, h, i, s: (b * nq + i, h, 0)),
        scratch_shapes=[
            pltpu.VMEM((QK_LANES + AUG_LANES, cols), BF16),
            pltpu.VMEM((CH, cols), F32), pltpu.VMEM((1, cols), F32),
            pltpu.VMEM((CH, cols), F32), pltpu.VMEM((1, cols), F32),
            pltpu.VMEM((1, cols), F32), pltpu.VMEM((hd2 + SUM_ROWS, cols), F32),
        ],
    )
    return pl.pallas_call(
        _diff_kernel,
        grid_spec=grid_spec,
        out_shape=jax.ShapeDtypeStruct((bsz * nq, DIFF_V_COLS, CH), BF16),
        compiler_params=_params(("parallel", "parallel", "arbitrary")),
        name="diff_attention",
    )(scal, dq_t, srows, dk3, dv_t, kaug_plain, lam_p, gain)


def _merge_kernel(ya_ref, yb_ref, gates_ref, h_ref, wa_ref, wb_ref, wo_ref, o_ref):
    for j in range(ya_ref.shape[0]):
        rows = slice(j * CH, (j + 1) * CH)
        a = lax.dot_general(ya_ref[j], wa_ref[...], _TN, preferred_element_type=F32)
        b = lax.dot_general(yb_ref[j], wb_ref[...], _TN, preferred_element_type=F32)
        ga = jax.nn.sigmoid(gates_ref[rows, :D_MODEL].astype(F32))
        gb = jax.nn.sigmoid(gates_ref[rows, D_MODEL:].astype(F32))
        merged = (ga * a + gb * b).astype(BF16)
        o_ref[rows, :] = h_ref[rows, :] + _dot(merged, wo_ref[...])


def _merge(ya_t, yb_t, gates, h2d, wa, wb, wo):
    n = h2d.shape[0]
    tm = ROW_TILE
    return pl.pallas_call(
        _merge_kernel,
        grid=(n // tm,),
        in_specs=[pl.BlockSpec((tm // CH, NSA_Q_COLS, CH), lambda i: (i, 0, 0)),
                  pl.BlockSpec((tm // CH, DIFF_V_COLS, CH), lambda i: (i, 0, 0)),
                  pl.BlockSpec((tm, 2 * D_MODEL), lambda i: (i, 0)),
                  pl.BlockSpec((tm, D_MODEL), lambda i: (i, 0)),
                  _const_spec(wa.shape), _const_spec(wb.shape), _const_spec(wo.shape)],
        out_specs=pl.BlockSpec((tm, D_MODEL), lambda i: (i, 0)),
        out_shape=jax.ShapeDtypeStruct((n, D_MODEL), F32),
        compiler_params=_params(("parallel",)),
        name="merge_outproj",
    )(ya_t, yb_t, gates, h2d, wa, wb, wo)


def _first_argmax(x, rows, n):
    mx = jnp.max(x, axis=0, keepdims=True)
    idx = jnp.min(jnp.where(x == mx, rows, n), axis=0, keepdims=True)
    return mx, idx


def _moe_kernel(h_ref, g_ref, wr_ref, br_ref, eexp_ref, wg_ref, wu_ref, wd_ref, fg_ref, o_ref, *, final):
    hres = h_ref[...]
    xf = hres * lax.rsqrt(jnp.mean(hres * hres, axis=-1, keepdims=True) + RMS_EPS) * g_ref[...]
    xb = xf.astype(BF16)
    tm = hres.shape[0]

    logits = lax.dot_general(wr_ref[...], xf, _NT, precision=lax.Precision.HIGHEST,
                             preferred_element_type=F32) + br_ref[...]
    gl = logits[0:MOE_GROUPS]
    rows_g = lax.broadcasted_iota(jnp.int32, (MOE_GROUPS, tm), 0)
    gmax, gidx = _first_argmax(gl, rows_g, MOE_GROUPS)
    g_w = 1.0 / jnp.sum(jnp.exp(gl - gmax), axis=0, keepdims=True)
    esel = jnp.zeros((EXPERTS_PER_GROUP, tm), F32)
    for gg in range(MOE_GROUPS):
        lo = MOE_GROUPS + gg * EXPERTS_PER_GROUP
        esel = jnp.where(gidx == gg, logits[lo:lo + EXPERTS_PER_GROUP], esel)
    rows_e = lax.broadcasted_iota(jnp.int32, (EXPERTS_PER_GROUP, tm), 0)
    v1, i1 = _first_argmax(esel, rows_e, EXPERTS_PER_GROUP)
    rest = jnp.where(rows_e == i1, -jnp.inf, esel)
    v2, i2 = _first_argmax(rest, rows_e, EXPERTS_PER_GROUP)
    e21 = jnp.exp(v2 - v1)
    w1 = g_w / (1.0 + e21)
    w2 = g_w * e21 / (1.0 + e21)
    rows_c = lax.broadcasted_iota(jnp.int32, (N_EXPERTS, tm), 0)
    grp_c = jnp.right_shift(rows_c, EXPERTS_PER_GROUP.bit_length() - 1)
    exp_c = jnp.bitwise_and(rows_c, EXPERTS_PER_GROUP - 1)
    comb = jnp.where(grp_c == gidx,
                     jnp.where(exp_c == i1, w1, 0.0) + jnp.where(exp_c == i2, w2, 0.0), 0.0)
    comb_hi = comb.astype(BF16)
    comb_lo = (comb - comb_hi.astype(F32)).astype(BF16)

    acc = hres
    n_ff = wg_ref.shape[1]
    step = 512
    for c in range(0, n_ff, step):
        cols = slice(c, c + step)
        cexp = (lax.dot_general(comb_hi, eexp_ref[:, cols], _TN, preferred_element_type=F32)
                + lax.dot_general(comb_lo, eexp_ref[:, cols], _TN, preferred_element_type=F32))
        hg = _dot(xb, wg_ref[:, cols])
        hu = _dot(xb, wu_ref[:, cols])
        act = (jax.nn.silu(hg) * hu * cexp).astype(BF16)
        acc = acc + _dot(act, wd_ref[cols, :])
    if final:
        acc = acc * lax.rsqrt(jnp.mean(acc * acc, axis=-1, keepdims=True) + RMS_EPS) * fg_ref[...]
    o_ref[...] = acc


def _moe(h2d, g, wr_t, br, eexp, wg, wu, wd, fg, final):
    n = h2d.shape[0]
    tm = ROW_TILE
    return pl.pallas_call(
        functools.partial(_moe_kernel, final=final),
        grid=(n // tm,),
        in_specs=[pl.BlockSpec((tm, D_MODEL), lambda i: (i, 0)),
                  _const_spec(g.shape), _const_spec(wr_t.shape), _const_spec(br.shape),
                  _const_spec(eexp.shape), _const_spec(wg.shape), _const_spec(wu.shape),
                  _const_spec(wd.shape), _const_spec(fg.shape)],
        out_specs=pl.BlockSpec((tm, D_MODEL), lambda i: (i, 0)),
        out_shape=jax.ShapeDtypeStruct((n, D_MODEL), F32),
        compiler_params=_params(("parallel",)),
        name="moe_final" if final else "moe",
    )(h2d, g, wr_t, br, eexp, wg, wu, wd, fg)


def _split_points():
    sizes = ([NSA_Q_COLS] + [NSA_KV_COLS] * 6
             + [NSA_GATE_COLS, DIFF_QK_COLS, DIFF_QK_COLS, DIFF_V_COLS, D_MODEL, D_MODEL])
    return [int(v) for v in np.cumsum(sizes)[:-1]]


def _cmp_to_sel_t(n_rows, nc, nb):
    c0 = np.arange(nc)[:, None] * CMP_STRIDE
    s0 = np.arange(nb)[None, :] * SEL_BLOCK
    ov = np.maximum(0, np.minimum(c0 + CMP_BLOCK, s0 + SEL_BLOCK) - np.maximum(c0, s0)) / CMP_BLOCK
    out = np.zeros((nb, n_rows), np.float32)
    out[:, :nc] = ov.T
    return out


def kernel(x, norm1_g, w_in, cmp_pe, cmp_w1, cmp_b1, cmp_w2, cmp_b2, diff_lambda, diff_subln_g, w_branch_a, w_branch_b, w_out, norm2_g, router_grp_w, router_grp_b, router_exp_w, router_exp_b, exp_w_gate, exp_w_up, exp_w_down, final_norm_g):
    bsz, seq, d = x.shape
    depth = w_in.shape[0]
    n = bsz * seq
    nq = seq // CH
    n_half = seq // CMP_STRIDE
    nc = (seq - CMP_BLOCK) // CMP_STRIDE + 1
    nb = seq // SEL_BLOCK
    assert d == D_MODEL and seq % ROW_TILE == 0 and seq >= WINDOW and WINDOW % CH == 0
    assert nb % 16 == 0 and MASK_ROW0 + nb <= AUG_LANES and n_half <= 256
    g_kv = NSA_KV_GROUPS
    eye_g = jnp.eye(g_kv, dtype=F32)
    eye_c = jnp.eye(2, dtype=F32)

    (nq_w, kc_w, vc_w, ks_w, vs_w, kw_w, vw_w, ng_w, dq_w, dk_w, dv_w, ga_w, gb_w) = jnp.split(
        w_in, _split_points(), axis=-1)
    ng_w = ng_w.reshape(depth, d, g_kv, NSA_GROUP_SIZE, 3).transpose(0, 1, 2, 4, 3)
    ng_w = ng_w.reshape(depth, d, g_kv, 3 * NSA_GROUP_SIZE)
    ng_w = jnp.pad(ng_w, ((0, 0), (0, 0), (0, 0), (0, GATE_ROWS_PER_GROUP - 3 * NSA_GROUP_SIZE)))
    ng_w = ng_w.reshape(depth, d, g_kv * GATE_ROWS_PER_GROUP)
    wn_all = jnp.concatenate([ks_w, kw_w, dk_w, kc_w, vc_w, ga_w, gb_w], axis=-1).astype(BF16)
    wt_all = jnp.concatenate([nq_w, dq_w, vs_w, vw_w, dv_w, ng_w], axis=-1)
    wt_all = jnp.swapaxes(wt_all, 1, 2).astype(BF16)

    w1r = cmp_w1.reshape(depth, 2, 2, CMP_STRIDE, HEAD_DIM, CMP_HIDDEN)
    w1_big = jnp.einsum('Lchldf,cC,gG->LhlcgdCGf', w1r, eye_c, eye_g)
    w1_big = w1_big.reshape(depth, 2, CMP_STRIDE * 2 * NSA_KV_COLS, 2 * g_kv * CMP_HIDDEN).astype(BF16)
    per = cmp_pe.reshape(depth, 2, 2, CMP_STRIDE, HEAD_DIM)
    pe_hb = jnp.einsum('Lchld,g->Lhlcgd', per, jnp.ones((g_kv,), F32))
    pe_hb = pe_hb.reshape(depth, 2, 1, CMP_STRIDE * 2 * NSA_KV_COLS)
    b1p = jnp.broadcast_to(cmp_b1[:, :, None, :], (depth, 2, g_kv, CMP_HIDDEN)).reshape(depth, 1, -1)
    w2k = jnp.einsum('Lfd,gG->LgfGd', cmp_w2[:, 0], eye_g).reshape(depth, g_kv * CMP_HIDDEN, NSA_KV_COLS)
    w2vt = jnp.einsum('Lfd,gG->LGdgf', cmp_w2[:, 1], eye_g).reshape(depth, NSA_KV_COLS, g_kv * CMP_HIDDEN)
    w2k = w2k.astype(BF16)
    w2vt = w2vt.astype(BF16)
    b2k = jnp.tile(cmp_b2[:, 0], (1, g_kv))[:, None, :]
    b2v = jnp.tile(cmp_b2[:, 1], (1, g_kv))[:, :, None]

    slopes = _alibi_slopes()
    nsa_srows = _slope_rows(slopes[:NSA_HEADS], NSA_GROUP_SIZE, CH)
    diff_srows = _slope_rows(np.repeat(slopes[NSA_HEADS:], 2), 2, CH)
    kaug = _key_aug_tables(seq, nb)
    kcaug = _cmp_aug_table(n_half)
    asel_t = jnp.asarray(_cmp_to_sel_t(n_half, nc, nb))

    wa_all = w_branch_a.astype(BF16)
    wb_all = w_branch_b.astype(BF16)
    wo_all = w_out.astype(BF16)

    wr = jnp.concatenate([router_grp_w, router_exp_w.reshape(depth, d, N_EXPERTS)], axis=-1)
    n_r = MOE_GROUPS + N_EXPERTS
    wr_t = jnp.pad(jnp.swapaxes(wr, 1, 2), ((0, 0), (0, 32 - n_r), (0, 0)))
    br = jnp.concatenate([router_grp_b, router_exp_b.reshape(depth, N_EXPERTS)], axis=-1)
    br = jnp.pad(br, ((0, 0), (0, 32 - n_r)))[:, :, None]
    eexp = jnp.asarray(np.kron(np.eye(N_EXPERTS, dtype=np.float32), np.ones((1, EXPERT_FF), np.float32)), BF16)
    wg_all = jnp.swapaxes(exp_w_gate, 1, 2).reshape(depth, d, N_EXPERTS * EXPERT_FF).astype(BF16)
    wu_all = jnp.swapaxes(exp_w_up, 1, 2).reshape(depth, d, N_EXPERTS * EXPERT_FF).astype(BF16)
    wd_all = exp_w_down.reshape(depth, N_EXPERTS * EXPERT_FF, d).astype(BF16)

    h = x.reshape(n, d)
    for l in range(depth):
        (ks, kw, dk, kcvc, gates, nq_t, dq_t, vs_t, vw_t, dv_t, ng_t) = _inproj(
            h, norm1_g[l][None, :], wn_all[l], wt_all[l])
        hb = kcvc.reshape(bsz, n_half, CMP_STRIDE * 2 * NSA_KV_COLS)
        kc, vc_t = _compress(hb, pe_hb[l], w1_big[l], b1p[l], w2k[l], b2k[l], w2vt[l], b2v[l])
        ya_t = _nsa(nq_t, nsa_srows, ks.reshape(n // CH, CH, NSA_KV_COLS), kw.reshape(n // CH, CH, NSA_KV_COLS),
                    vs_t, vw_t, kaug, kc, kcaug, vc_t, ng_t, asel_t, bsz, nq)
        lam_init = 0.8 - 0.6 * float(np.exp(-0.3 * l))
        scal = jnp.asarray([lam_init, 1.0 - lam_init], F32)
        yb_t = _diff(scal, dq_t, diff_srows, dk.reshape(n // CH, CH, DIFF_QK_COLS), dv_t, kaug[1],
                     diff_lambda[l], diff_subln_g[l][:, None], bsz, nq)
        h = _merge(ya_t, yb_t, gates, h, wa_all[l], wb_all[l], wo_all[l])
        h = _moe(h, norm2_g[l][None, :], wr_t[l], br[l], eexp, wg_all[l], wu_all[l], wd_all[l],
                 final_norm_g[None, :], final=(l == depth - 1))
    return h.reshape(bsz, seq, d)
```

```python
import functools

import numpy as np
import jax
import jax.numpy as jnp
from jax import lax
from jax.experimental import pallas as pl
from jax.experimental.pallas import tpu as pltpu

F32 = jnp.float32
BF16 = jnp.bfloat16

D_MODEL = 1024
HEAD_DIM = 64
NSA_HEADS = 8
NSA_KV_GROUPS = 2
NSA_GROUP_SIZE = NSA_HEADS // NSA_KV_GROUPS
CMP_BLOCK = 32
CMP_STRIDE = 16
CMP_HIDDEN = 128
SEL_BLOCK = 64
SEL_TOPK = 8
WINDOW = 512
FORCED_SCORE = 1e9
DIFF_HEADS = 4
MOE_GROUPS = 4
EXPERTS_PER_GROUP = 4
N_EXPERTS = MOE_GROUPS * EXPERTS_PER_GROUP
EXPERT_FF = D_MODEL // 8
RMS_EPS = 1e-6
SUBLN_EPS = 1e-5
NEG_INF = -1e30
N_ALIBI_HEADS = NSA_HEADS + DIFF_HEADS

NSA_Q_COLS = NSA_HEADS * HEAD_DIM
NSA_KV_COLS = NSA_KV_GROUPS * HEAD_DIM
NSA_GATE_COLS = 3 * NSA_HEADS
DIFF_QK_COLS = DIFF_HEADS * 2 * HEAD_DIM
DIFF_V_COLS = DIFF_HEADS * 2 * HEAD_DIM
GATE_ROWS_PER_GROUP = 16

CH = 256
DIFF_TILE_CHUNKS = 2
ROW_TILE = 512
VMEM_LIMIT = 56 * 1024 * 1024

LOG2E = float(np.log2(np.e))
Q_SCALE = HEAD_DIM ** -0.5 * LOG2E

QK_LANES = 2 * HEAD_DIM
AUG_LANES = 128
SLOPE_PIECES = 3
PAD_ROW = 2 * SLOPE_PIECES
ALIBI_ROWS = 16
MASK_ROW0 = ALIBI_ROWS
MASK_BIG = 1e30
SUM_ROWS = 16

_NT = (((1,), (1,)), ((), ()))
_TN = (((0,), (0,)), ((), ()))


def _dot(a, b):
    return jnp.dot(a, b, preferred_element_type=F32)


def _const_spec(shape):
    nd = len(shape)
    return pl.BlockSpec(shape, lambda *_: (0,) * nd, pipeline_mode=pl.Buffered(1))


def _params(sem):
    return pltpu.CompilerParams(dimension_semantics=sem, vmem_limit_bytes=VMEM_LIMIT)


def _alibi_slopes():
    return 2.0 ** (-8.0 * np.arange(1, N_ALIBI_HEADS + 1) / N_ALIBI_HEADS)


_NAT_WIDTHS = (NSA_KV_COLS, NSA_KV_COLS, DIFF_QK_COLS, 2 * NSA_KV_COLS, 2 * D_MODEL)
_TR_ROWS = (NSA_Q_COLS, DIFF_QK_COLS, NSA_KV_COLS, NSA_KV_COLS, DIFF_V_COLS, 2 * GATE_ROWS_PER_GROUP)
_TR_SCALE = (Q_SCALE, Q_SCALE, 1.0, 1.0, 1.0, 1.0)


def _inproj_kernel(x_ref, g_ref, wn_ref, wt_ref, *out_refs):
    nat_refs = out_refs[:len(_NAT_WIDTHS)]
    tr_refs = out_refs[len(_NAT_WIDTHS):]
    x = x_ref[...]
    xn = (x * lax.rsqrt(jnp.mean(x * x, axis=-1, keepdims=True) + RMS_EPS) * g_ref[...]).astype(BF16)
    off = 0
    for ref, width in zip(nat_refs, _NAT_WIDTHS):
        for c in range(0, width, 512):
            cw = min(512, width - c)
            ref[:, c:c + cw] = _dot(xn, wn_ref[:, off + c:off + c + cw]).astype(ref.dtype)
        off += width
    n_sub = x.shape[0] // CH
    off = 0
    for ref, rows, scale in zip(tr_refs, _TR_ROWS, _TR_SCALE):
        for c in range(0, rows, 256):
            rw = min(256, rows - c)
            res = lax.dot_general(wt_ref[off + c:off + c + rw, :], xn, _NT, preferred_element_type=F32)
            if scale != 1.0:
                res = res * scale
            for j in range(n_sub):
                ref[j, c:c + rw, :] = res[:, j * CH:(j + 1) * CH].astype(ref.dtype)
        off += rows


def _inproj(h2d, g, wn, wt):
    n = h2d.shape[0]
    tm = ROW_TILE
    nat_dtypes = (BF16, BF16, BF16, F32, BF16)
    tr_dtypes = (BF16, BF16, BF16, BF16, BF16, F32)
    out_shape = [jax.ShapeDtypeStruct((n, w), dt) for w, dt in zip(_NAT_WIDTHS, nat_dtypes)]
    out_shape += [jax.ShapeDtypeStruct((n // CH, r, CH), dt) for r, dt in zip(_TR_ROWS, tr_dtypes)]
    out_specs = [pl.BlockSpec((tm, w), lambda i: (i, 0)) for w in _NAT_WIDTHS]
    out_specs += [pl.BlockSpec((tm // CH, r, CH), lambda i: (i, 0, 0)) for r in _TR_ROWS]
    return pl.pallas_call(
        _inproj_kernel,
        grid=(n // tm,),
        in_specs=[pl.BlockSpec((tm, D_MODEL), lambda i: (i, 0)),
                  _const_spec((1, D_MODEL)),
                  _const_spec(wn.shape),
                  _const_spec(wt.shape)],
        out_specs=out_specs,
        out_shape=out_shape,
        compiler_params=_params(("parallel",)),
        name="inproj",
    )(h2d, g, wn, wt)


def _compress_kernel(hb_ref, pe_ref, w1_ref, b1_ref, w2k_ref, b2k_ref, w2v_ref, b2v_ref, kc_ref, vct_ref):
    hb = hb_ref[0]
    rows = hb.shape[0]
    top = (hb + pe_ref[0]).astype(BF16)
    bot = (hb + pe_ref[1]).astype(BF16)
    p = _dot(top, w1_ref[0])
    q = _dot(bot, w1_ref[1])
    q_next = pltpu.roll(q, rows - 1, 0)
    hid = jax.nn.gelu(p + q_next + b1_ref[...])
    width = hid.shape[1] // 2
    kc_ref[0] = (_dot(hid[:, :width].astype(BF16), w2k_ref[...]) + b2k_ref[...]).astype(kc_ref.dtype)
    vct = lax.dot_general(w2v_ref[...], hid[:, width:].astype(BF16), _NT, preferred_element_type=F32)
    vct_ref[0] = (vct + b2v_ref[...]).astype(vct_ref.dtype)


def _compress(hb, pe_hb, w1_big, b1p, w2k, b2k, w2vt, b2v):
    bsz, rows, width = hb.shape
    gk = NSA_KV_COLS
    return pl.pallas_call(
        _compress_kernel,
        grid=(bsz,),
        in_specs=[pl.BlockSpec((1, rows, width), lambda b: (b, 0, 0)),
                  _const_spec(pe_hb.shape), _const_spec(w1_big.shape), _const_spec(b1p.shape),
                  _const_spec(w2k.shape), _const_spec(b2k.shape), _const_spec(w2vt.shape),
                  _const_spec(b2v.shape)],
        out_specs=[pl.BlockSpec((1, rows, gk), lambda b: (b, 0, 0)),
                   pl.BlockSpec((1, gk, rows), lambda b: (b, 0, 0))],
        out_shape=[jax.ShapeDtypeStruct((bsz, rows, gk), BF16),
                   jax.ShapeDtypeStruct((bsz, gk, rows), BF16)],
        compiler_params=_params(("parallel",)),
        name="compress",
    )(hb, pe_hb, w1_big, b1p, w2k, b2k, w2vt, b2v)


def _bf16_pieces(x):
    out = []
    rest = np.asarray(x, np.float32)
    for _ in range(SLOPE_PIECES):
        piece = rest.astype(BF16).astype(np.float32)
        out.append(piece)
        rest = rest - piece
    return out


def _slope_rows(slopes, heads_per_block, cols_per_head):
    sl2 = (np.asarray(slopes, np.float32).astype(np.float64) * LOG2E).astype(np.float32)
    pieces = np.stack(_bf16_pieces(sl2) * 2, axis=0)
    rows = np.zeros((ALIBI_ROWS, sl2.shape[0]), np.float32)
    rows[:pieces.shape[0]] = pieces
    rows[PAD_ROW] = -MASK_BIG
    rows = np.repeat(rows, cols_per_head, axis=1)
    rows = rows.reshape(ALIBI_ROWS, -1, heads_per_block * cols_per_head).transpose(1, 0, 2)
    return jnp.asarray(rows, BF16)


def _key_aug_tables(seq, nb):
    pos = np.arange(seq)
    aug = np.zeros((2, seq + CH, AUG_LANES), np.float32)
    aug[:, :seq, 0:SLOPE_PIECES] = (pos % CH)[None, :, None]
    aug[:, :seq, SLOPE_PIECES:2 * SLOPE_PIECES] = (pos // CH * CH)[None, :, None]
    aug[0, pos, MASK_ROW0 + pos // SEL_BLOCK] = 1.0
    aug[:, seq:, PAD_ROW] = 1.0
    return jnp.asarray(aug.reshape(2, seq // CH + 1, CH, AUG_LANES), BF16)


def _cmp_aug_table(n_rows):
    aug = np.zeros((n_rows, AUG_LANES), np.float32)
    aug[:, 0:SLOPE_PIECES] = (np.arange(n_rows) * CMP_STRIDE)[:, None]
    aug[:, SLOPE_PIECES:2 * SLOPE_PIECES] = CMP_BLOCK - 1
    return jnp.asarray(aug, BF16)


def _tile_lanes(x, reps):
    return jnp.concatenate([x] * reps, axis=1)


def _query_minus_key(reps):
    shape = (CH, reps * CH)
    q_off = jnp.bitwise_and(lax.broadcasted_iota(jnp.int32, shape, 1), CH - 1)
    return q_off - lax.broadcasted_iota(jnp.int32, shape, 0)


def _flash_init(m_ref, acc_ref):
    m_ref[...] = jnp.full(m_ref.shape, NEG_INF, F32)
    acc_ref[...] = jnp.zeros(acc_ref.shape, F32)


def _normalized(acc_ref):
    dv = acc_ref.shape[0] - SUM_ROWS
    return acc_ref[0:dv, :] / acc_ref[dv:dv + 1, :]


def _chunk_scores(k_blk, aug_blk, qa_ref):
    return _dot(jnp.concatenate([k_blk, aug_blk], axis=1), qa_ref[...])


def _produce(k_blk, aug_blk, qa_ref, buf, mask=None):
    s_ref, mx_ref = buf
    s = _chunk_scores(k_blk, aug_blk, qa_ref)
    if mask is not None:
        s = jnp.where(mask, s, NEG_INF)
    s_ref[...] = s
    mx_ref[...] = jnp.max(s, axis=0, keepdims=True)


def _consume(buf, v_t, state, mask=None):
    s_ref, mx_ref = buf
    m_ref, acc_ref = state
    s = s_ref[...]
    if mask is None:
        mx = mx_ref[...]
    else:
        s = jnp.where(mask, s, NEG_INF)
        mx = jnp.max(s, axis=0, keepdims=True)
    m_prev = m_ref[...]
    m_new = jnp.maximum(m_prev, mx)
    alpha = jnp.exp2(m_prev - m_new)
    p = jnp.exp2(s - m_new)
    v_ones = jnp.concatenate([v_t, jnp.ones((SUM_ROWS, v_t.shape[1]), BF16)], axis=0)
    acc_ref[...] = alpha * acc_ref[...] + _dot(v_ones, p.astype(BF16))
    m_ref[...] = m_new


def _pad_or(aug_ref, c, is_pad):
    return aug_ref[jnp.where(is_pad, aug_ref.shape[0] - 1, c)]


def _causal_first(i, k_ref, aug_ref, qa_ref, buf0):
    _produce(k_ref[0], _pad_or(aug_ref, 0, jnp.bitwise_and(i, 1) == 1), qa_ref, buf0)


def _causal_pairs(i, k_ref, v_ref, aug_ref, qa_ref, bufs, state):
    buf0, buf1 = bufs
    pad = jnp.bitwise_and(i, 1)

    def pair(k, carry):
        c = 2 * k - pad
        _produce(k_ref[c + 1], aug_ref[c + 1], qa_ref, buf1)
        _consume(buf0, v_ref[jnp.maximum(c, 0)], state)
        _produce(k_ref[c + 2], aug_ref[c + 2], qa_ref, buf0)
        _consume(buf1, v_ref[c + 1], state)
        return carry
    lax.fori_loop(0, jnp.right_shift(i + pad, 1), pair, 0)


def _nsa_kernel(q_ref, srow_ref, ks_ref, kw_ref, vs_ref, vw_ref, kaug_ref, kc_ref, kcaug_ref, vc_ref,
                ng_ref, asel_ref, o_ref, qa_ref, ocmp_ref, s0, x0, s1, x1, w0, y0, w1, y1, w2, y2,
                m_s, acc_s, m_w, acc_w):
    i = pl.program_id(1)
    g = pl.program_id(2)
    t0 = i * CH
    nh = NSA_GROUP_SIZE
    n_cmp = kc_ref.shape[1]
    n_blk = asel_ref.shape[0]
    k_sel = min(SEL_TOPK, n_blk)
    bufs_s = ((s0, x0), (s1, x1))
    bufs_w = ((w0, y0), (w1, y1), (w2, y2))
    state_s = (m_s, acc_s)
    state_w = (m_w, acc_w)
    kaug_s = kaug_ref.at[0]
    kaug_w = kaug_ref.at[1]

    t_pos = t0 + lax.broadcasted_iota(jnp.int32, (1, CH), 1)
    d0_i = _query_minus_key(nh)
    causal = d0_i >= 0
    window_edge = d0_i < 0

    sub2 = lax.broadcasted_iota(jnp.int32, (QK_LANES, CH), 0)
    half0 = g * HEAD_DIM
    place = (sub2 >= half0) & (sub2 < half0 + HEAD_DIM)
    for hh in range(nh):
        qh = q_ref[0, hh * HEAD_DIM:(hh + 1) * HEAD_DIM, :]
        qa_ref[0:QK_LANES, hh * CH:(hh + 1) * CH] = jnp.where(
            place, jnp.concatenate([qh, qh], axis=0), jnp.zeros((), BF16))
    qa_ref[QK_LANES:QK_LANES + ALIBI_ROWS, :] = srow_ref[0]
    qa_ref[QK_LANES + MASK_ROW0:, :] = jnp.zeros((AUG_LANES - MASK_ROW0, nh * CH), BF16)

    cmp_scores = _chunk_scores(kc_ref[0], kcaug_ref[...], qa_ref)

    n_back = WINDOW // CH
    chunks_w = []
    for back in range(n_back, 0, -1):
        c = jnp.maximum(i - back, 0)
        _produce(kw_ref[c], _pad_or(kaug_w, c, i < back), qa_ref, bufs_w[n_back - back],
                 mask=window_edge if back == n_back else None)
        chunks_w.append(c)
    _produce(kw_ref[i], kaug_w[i], qa_ref, bufs_w[n_back], mask=causal)
    chunks_w.append(i)

    n_idx = lax.broadcasted_iota(jnp.int32, (n_cmp, nh * CH), 0)
    t_pos_all = t0 + jnp.bitwise_and(lax.broadcasted_iota(jnp.int32, (1, nh * CH), 1), CH - 1)
    valid_c = n_idx * CMP_STRIDE + (CMP_BLOCK - 1) <= t_pos_all
    lg = jnp.where(valid_c, cmp_scores, NEG_INF)
    m = jnp.max(lg, axis=0, keepdims=True)
    p = jnp.where(valid_c, jnp.exp2(lg - m), 0.0)
    l = jnp.sum(p, axis=0, keepdims=True)
    pc = p * jnp.where(l > 0.0, 1.0 / l, 0.0)
    ocmp_ref[...] = _dot(vc_ref[0], pc.astype(BF16))
    psum = pc[:, 0:CH]
    for hh in range(1, nh):
        psum = psum + pc[:, hh * CH:(hh + 1) * CH]

    imp = jnp.dot(asel_ref[...], psum, precision=lax.Precision.HIGHEST, preferred_element_type=F32)
    j_idx = lax.broadcasted_iota(jnp.int32, (n_blk, CH), 0)
    cur = jnp.right_shift(t_pos, SEL_BLOCK.bit_length() - 1)
    forced = (j_idx == 0) | (j_idx == cur) | (j_idx == cur - 1)
    imp = jnp.where(forced, FORCED_SCORE, imp)
    imp = jnp.where(j_idx * SEL_BLOCK <= t_pos, imp, -1.0)
    rows_per = 8
    j_loc = lax.broadcasted_iota(jnp.int32, (rows_per, CH), 0)
    mask_blocks = []
    for r0 in range(0, n_blk, rows_per):
        blk = imp[r0:r0 + rows_per, :]
        cnt = jnp.zeros((rows_per, CH), jnp.int32)
        for jp in range(n_blk):
            row = imp[jp:jp + 1, :]
            gt = jnp.where(row > blk, 1, 0)
            ge = jnp.where(row >= blk, 1, 0)
            if jp >= r0 + rows_per - 1:
                cnt = cnt + gt
            elif jp < r0:
                cnt = cnt + ge
            else:
                cnt = cnt + jnp.where(j_loc + r0 > jp, ge, gt)
        mask_blocks.append(jnp.where(cnt < k_sel, 0.0, -MASK_BIG))
    mask_rows = jnp.concatenate(mask_blocks, axis=0).astype(BF16)
    qa_ref[QK_LANES + MASK_ROW0:QK_LANES + MASK_ROW0 + n_blk, :] = _tile_lanes(mask_rows, nh)

    _causal_first(i, ks_ref, kaug_s, qa_ref, bufs_s[0])

    _flash_init(*state_w)
    for buf, c in zip(bufs_w, chunks_w):
        _consume(buf, vw_ref[c], state_w)

    _flash_init(*state_s)
    _causal_pairs(i, ks_ref, vs_ref, kaug_s, qa_ref, bufs_s, state_s)
    _consume(bufs_s[0], vs_ref[i], state_s, mask=causal)

    def gate(branch):
        rows = [ng_ref[0, branch * nh + hh:branch * nh + hh + 1, :] for hh in range(nh)]
        return jax.nn.sigmoid(jnp.concatenate(rows, axis=1))
    out = gate(0) * ocmp_ref[...] + gate(1) * _normalized(acc_s) + gate(2) * _normalized(acc_w)
    for hh in range(nh):
        o_ref[0, hh * HEAD_DIM:(hh + 1) * HEAD_DIM, :] = out[:, hh * CH:(hh + 1) * CH].astype(o_ref.dtype)


def _nsa(nq_t, srows, ks3, kw3, vs_t, vw_t, kaug, kc, kcaug, vc_t, ng_t, asel_t, bsz, nq):
    n_cmp = kc.shape[1]
    nh = NSA_GROUP_SIZE
    gh = nh * HEAD_DIM
    cols = nh * CH
    return pl.pallas_call(
        _nsa_kernel,
        grid=(bsz, nq, NSA_KV_GROUPS),
        in_specs=[
            pl.BlockSpec((1, gh, CH), lambda b, i, g: (b * nq + i, g, 0)),
            pl.BlockSpec((1, ALIBI_ROWS, cols), lambda b, i, g: (g, 0, 0)),
            pl.BlockSpec((nq, CH, NSA_KV_COLS), lambda b, i, g: (b, 0, 0)),
            pl.BlockSpec((nq, CH, NSA_KV_COLS), lambda b, i, g: (b, 0, 0)),
            pl.BlockSpec((nq, HEAD_DIM, CH), lambda b, i, g: (b, g, 0)),
            pl.BlockSpec((nq, HEAD_DIM, CH), lambda b, i, g: (b, g, 0)),
            pl.BlockSpec(kaug.shape, lambda b, i, g: (0, 0, 0, 0)),
            pl.BlockSpec((1, n_cmp, NSA_KV_COLS), lambda b, i, g: (b, 0, 0)),
            pl.BlockSpec(kcaug.shape, lambda b, i, g: (0, 0)),
            pl.BlockSpec((1, HEAD_DIM, n_cmp), lambda b, i, g: (b, g, 0)),
            pl.BlockSpec((1, GATE_ROWS_PER_GROUP, CH), lambda b, i, g: (b * nq + i, g, 0)),
            pl.BlockSpec(asel_t.shape, lambda b, i, g: (0, 0)),
        ],
        out_specs=pl.BlockSpec((1, gh, CH), lambda b, i, g: (b * nq + i, g, 0)),
        out_shape=jax.ShapeDtypeStruct((bsz * nq, NSA_Q_COLS, CH), BF16),
        scratch_shapes=[
            pltpu.VMEM((QK_LANES + AUG_LANES, cols), BF16),
            pltpu.VMEM((HEAD_DIM, cols), F32),
            pltpu.VMEM((CH, cols), F32), pltpu.VMEM((1, cols), F32),
            pltpu.VMEM((CH, cols), F32), pltpu.VMEM((1, cols), F32),
            pltpu.VMEM((CH, cols), F32), pltpu.VMEM((1, cols), F32),
            pltpu.VMEM((CH, cols), F32), pltpu.VMEM((1, cols), F32),
            pltpu.VMEM((CH, cols), F32), pltpu.VMEM((1, cols), F32),
            pltpu.VMEM((1, cols), F32), pltpu.VMEM((HEAD_DIM + SUM_ROWS, cols), F32),
            pltpu.VMEM((1, cols), F32), pltpu.VMEM((HEAD_DIM + SUM_ROWS, cols), F32),
        ],
        compiler_params=_params(("parallel", "parallel", "arbitrary")),
        name="nsa_attention",
    )(nq_t, srows, ks3, kw3, vs_t, vw_t, kaug, kc, kcaug, vc_t, ng_t, asel_t)


def _diff_kernel(scal_ref, q_ref, srow_ref, k_ref, v_ref, kaug_ref, lam_ref, gain_ref, o_ref,
                 qa_ref, s0, x0, s1, x1, m_r, acc_r):
    nt = DIFF_TILE_CHUNKS
    tq = nt * CH
    c0 = pl.program_id(2) * nt
    lam_init = scal_ref[0]
    out_scale = scal_ref[1]
    sub2 = lax.broadcasted_iota(jnp.int32, (QK_LANES, CH), 0)
    zero = jnp.zeros((), BF16)
    for j in range(nt):
        q = q_ref[j]
        qa_ref[0:QK_LANES, j * CH:(j + 1) * CH] = jnp.where(sub2 < HEAD_DIM, q, zero)
        qa_ref[0:QK_LANES, tq + j * CH:tq + (j + 1) * CH] = jnp.where(sub2 >= HEAD_DIM, q, zero)
    qa_ref[QK_LANES:QK_LANES + ALIBI_ROWS, :] = srow_ref[0]
    qa_ref[QK_LANES + MASK_ROW0:, :] = jnp.zeros((AUG_LANES - MASK_ROW0, 2 * tq), BF16)
    q_off = jnp.bitwise_and(lax.broadcasted_iota(jnp.int32, (CH, 2 * tq), 1), tq - 1)
    d0_i = q_off - lax.broadcasted_iota(jnp.int32, (CH, 2 * tq), 0)

    state = (m_r, acc_r)
    bufs = ((s0, x0), (s1, x1))
    _causal_first(c0, k_ref, kaug_ref, qa_ref, bufs[0])
    _flash_init(*state)
    _causal_pairs(c0, k_ref, v_ref, kaug_ref, qa_ref, bufs, state)
    _produce(k_ref[c0 + 1], kaug_ref[c0 + 1], qa_ref, bufs[1], mask=d0_i >= CH)
    _consume(bufs[0], v_ref[c0], state, mask=d0_i >= 0)
    _consume(bufs[1], v_ref[c0 + 1], state)

    lp = lam_ref[...]
    lam = (jnp.exp(jnp.sum(lp[0:1] * lp[1:2], axis=1, keepdims=True))
           - jnp.exp(jnp.sum(lp[2:3] * lp[3:4], axis=1, keepdims=True)) + lam_init)
    att = _normalized(acc_r)
    o = att[:, 0:tq] - lam * att[:, tq:2 * tq]
    o = o * lax.rsqrt(jnp.mean(o * o, axis=0, keepdims=True) + SUBLN_EPS) * gain_ref[...]
    o = (o * out_scale).astype(o_ref.dtype)
    for j in range(nt):
        o_ref[j] = o[:, j * CH:(j + 1) * CH]


def _diff(scal, dq_t, srows, dk3, dv_t, kaug_plain, lam_p, gain, bsz, nq):
    hd2 = 2 * HEAD_DIM
    nt = DIFF_TILE_CHUNKS
    n_tiles = nq // nt
    cols = 2 * nt * CH
    grid_spec = pltpu.PrefetchScalarGridSpec(
        num_scalar_prefetch=1,
        grid=(bsz, DIFF_HEADS, n_tiles),
        in_specs=[
            pl.BlockSpec((nt, hd2, CH), lambda b, h, i, s: (b * n_tiles + i, h, 0)),
            pl.BlockSpec((1, ALIBI_ROWS, cols), lambda b, h, i, s: (h, 0, 0)),
            pl.BlockSpec((nq, CH, hd2), lambda b, h, i, s: (b, 0, h)),
            pl.BlockSpec((nq, hd2, CH), lambda b, h, i, s: (b, h, 0)),
            pl.BlockSpec(kaug_plain.shape, lambda b, h, i, s: (0, 0, 0)),
            pl.BlockSpec(lam_p.shape, lambda b, h, i, s: (0, 0)),
            pl.BlockSpec(gain.shape, lambda b, h, i, s: (0, 0)),
        ],
        out_specs=pl.BlockSpec((nt, hd2, CH), lambda b, h, i, s: (b * n_tiles + i, h, 0)),
        scratch_shapes=[
            pltpu.VMEM((QK_LANES + AUG_LANES, cols), BF16),
            pltpu.VMEM((CH, cols), F32), pltpu.VMEM((1, cols), F32),
            pltpu.VMEM((CH, cols), F32), pltpu.VMEM((1, cols), F32),
            pltpu.VMEM((1, cols), F32), pltpu.VMEM((hd2 + SUM_ROWS, cols), F32),
        ],
    )
    return pl.pallas_call(
        _diff_kernel,
        grid_spec=grid_spec,
        out_shape=jax.ShapeDtypeStruct((bsz * nq, DIFF_V_COLS, CH), BF16),
        compiler_params=_params(("parallel", "parallel", "arbitrary")),
        name="diff_attention",
    )(scal, dq_t, srows, dk3, dv_t, kaug_plain, lam_p, gain)


def _merge_kernel(ya_ref, yb_ref, gates_ref, h_ref, wa_ref, wb_ref, wo_ref, o_ref):
    for j in range(ya_ref.shape[0]):
        rows = slice(j * CH, (j + 1) * CH)
        a = lax.dot_general(ya_ref[j], wa_ref[...], _TN, preferred_element_type=F32)
        b = lax.dot_general(yb_ref[j], wb_ref[...], _TN, preferred_element_type=F32)
        ga = jax.nn.sigmoid(gates_ref[rows, :D_MODEL].astype(F32))
        gb = jax.nn.sigmoid(gates_ref[rows, D_MODEL:].astype(F32))
        merged = (ga * a + gb * b).astype(BF16)
        o_ref[rows, :] = h_ref[rows, :] + _dot(merged, wo_ref[...])


def _merge(ya_t, yb_t, gates, h2d, wa, wb, wo):
    n = h2d.shape[0]
    tm = ROW_TILE
    return pl.pallas_call(
        _merge_kernel,
        grid=(n // tm,),
        in_specs=[pl.BlockSpec((tm // CH, NSA_Q_COLS, CH), lambda i: (i, 0, 0)),
                  pl.BlockSpec((tm // CH, DIFF_V_COLS, CH), lambda i: (i, 0, 0)),
                  pl.BlockSpec((tm, 2 * D_MODEL), lambda i: (i, 0)),
                  pl.BlockSpec((tm, D_MODEL), lambda i: (i, 0)),
                  _const_spec(wa.shape), _const_spec(wb.shape), _const_spec(wo.shape)],
        out_specs=pl.BlockSpec((tm, D_MODEL), lambda i: (i, 0)),
        out_shape=jax.ShapeDtypeStruct((n, D_MODEL), F32),
        compiler_params=_params(("parallel",)),
        name="merge_outproj",
    )(ya_t, yb_t, gates, h2d, wa, wb, wo)


def _first_argmax(x, rows, n):
    mx = jnp.max(x, axis=0, keepdims=True)
    idx = jnp.min(jnp.where(x == mx, rows, n), axis=0, keepdims=True)
    return mx, idx


def _moe_kernel(h_ref, g_ref, wr_ref, br_ref, eexp_ref, wg_ref, wu_ref, wd_ref, fg_ref, o_ref, *, final):
    hres = h_ref[...]
    xf = hres * lax.rsqrt(jnp.mean(hres * hres, axis=-1, keepdims=True) + RMS_EPS) * g_ref[...]
    xb = xf.astype(BF16)
    tm = hres.shape[0]

    logits = lax.dot_general(wr_ref[...], xf, _NT, precision=lax.Precision.HIGHEST,
                             preferred_element_type=F32) + br_ref[...]
    gl = logits[0:MOE_GROUPS]
    rows_g = lax.broadcasted_iota(jnp.int32, (MOE_GROUPS, tm), 0)
    gmax, gidx = _first_argmax(gl, rows_g, MOE_GROUPS)
    g_w = 1.0 / jnp.sum(jnp.exp(gl - gmax), axis=0, keepdims=True)
    esel = jnp.zeros((EXPERTS_PER_GROUP, tm), F32)
    for gg in range(MOE_GROUPS):
        lo = MOE_GROUPS + gg * EXPERTS_PER_GROUP
        esel = jnp.where(gidx == gg, logits[lo:lo + EXPERTS_PER_GROUP], esel)
    rows_e = lax.broadcasted_iota(jnp.int32, (EXPERTS_PER_GROUP, tm), 0)
    v1, i1 = _first_argmax(esel, rows_e, EXPERTS_PER_GROUP)
    rest = jnp.where(rows_e == i1, -jnp.inf, esel)
    v2, i2 = _first_argmax(rest, rows_e, EXPERTS_PER_GROUP)
    e21 = jnp.exp(v2 - v1)
    w1 = g_w / (1.0 + e21)
    w2 = g_w * e21 / (1.0 + e21)
    rows_c = lax.broadcasted_iota(jnp.int32, (N_EXPERTS, tm), 0)
    grp_c = jnp.right_shift(rows_c, EXPERTS_PER_GROUP.bit_length() - 1)
    exp_c = jnp.bitwise_and(rows_c, EXPERTS_PER_GROUP - 1)
    comb = jnp.where(grp_c == gidx,
                     jnp.where(exp_c == i1, w1, 0.0) + jnp.where(exp_c == i2, w2, 0.0), 0.0)
    comb_hi = comb.astype(BF16)
    comb_lo = (comb - comb_hi.astype(F32)).astype(BF16)

    acc = hres
    n_ff = wg_ref.shape[1]
    step = 512
    for c in range(0, n_ff, step):
        cols = slice(c, c + step)
        cexp = (lax.dot_general(comb_hi, eexp_ref[:, cols], _TN, preferred_element_type=F32)
                + lax.dot_general(comb_lo, eexp_ref[:, cols], _TN, preferred_element_type=F32))
        hg = _dot(xb, wg_ref[:, cols])
        hu = _dot(xb, wu_ref[:, cols])
        act = (jax.nn.silu(hg) * hu * cexp).astype(BF16)
        acc = acc + _dot(act, wd_ref[cols, :])
    if final:
        acc = acc * lax.rsqrt(jnp.mean(acc * acc, axis=-1, keepdims=True) + RMS_EPS) * fg_ref[...]
    o_ref[...] = acc


def _moe(h2d, g, wr_t, br, eexp, wg, wu, wd, fg, final):
    n = h2d.shape[0]
    tm = ROW_TILE
    return pl.pallas_call(
        functools.partial(_moe_kernel, final=final),
        grid=(n // tm,),
        in_specs=[pl.BlockSpec((tm, D_MODEL), lambda i: (i, 0)),
                  _const_spec(g.shape), _const_spec(wr_t.shape), _const_spec(br.shape),
                  _const_spec(eexp.shape), _const_spec(wg.shape), _const_spec(wu.shape),
                  _const_spec(wd.shape), _const_spec(fg.shape)],
        out_specs=pl.BlockSpec((tm, D_MODEL), lambda i: (i, 0)),
        out_shape=jax.ShapeDtypeStruct((n, D_MODEL), F32),
        compiler_params=_params(("parallel",)),
        name="moe_final" if final else "moe",
    )(h2d, g, wr_t, br, eexp, wg, wu, wd, fg)


def _split_points():
    sizes = ([NSA_Q_COLS] + [NSA_KV_COLS] * 6
             + [NSA_GATE_COLS, DIFF_QK_COLS, DIFF_QK_COLS, DIFF_V_COLS, D_MODEL, D_MODEL])
    return [int(v) for v in np.cumsum(sizes)[:-1]]


def _cmp_to_sel_t(n_rows, nc, nb):
    c0 = np.arange(nc)[:, None] * CMP_STRIDE
    s0 = np.arange(nb)[None, :] * SEL_BLOCK
    ov = np.maximum(0, np.minimum(c0 + CMP_BLOCK, s0 + SEL_BLOCK) - np.maximum(c0, s0)) / CMP_BLOCK
    out = np.zeros((nb, n_rows), np.float32)
    out[:, :nc] = ov.T
    return out


def kernel(x, norm1_g, w_in, cmp_pe, cmp_w1, cmp_b1, cmp_w2, cmp_b2, diff_lambda, diff_subln_g, w_branch_a, w_branch_b, w_out, norm2_g, router_grp_w, router_grp_b, router_exp_w, router_exp_b, exp_w_gate, exp_w_up, exp_w_down, final_norm_g):
    bsz, seq, d = x.shape
    depth = w_in.shape[0]
    n = bsz * seq
    nq = seq // CH
    n_half = seq // CMP_STRIDE
    nc = (seq - CMP_BLOCK) // CMP_STRIDE + 1
    nb = seq // SEL_BLOCK
    assert d == D_MODEL and seq % ROW_TILE == 0 and seq >= WINDOW and WINDOW % CH == 0
    assert nb % 16 == 0 and MASK_ROW0 + nb <= AUG_LANES and n_half <= 256
    g_kv = NSA_KV_GROUPS
    eye_g = jnp.eye(g_kv, dtype=F32)
    eye_c = jnp.eye(2, dtype=F32)

    (nq_w, kc_w, vc_w, ks_w, vs_w, kw_w, vw_w, ng_w, dq_w, dk_w, dv_w, ga_w, gb_w) = jnp.split(
        w_in, _split_points(), axis=-1)
    ng_w = ng_w.reshape(depth, d, g_kv, NSA_GROUP_SIZE, 3).transpose(0, 1, 2, 4, 3)
    ng_w = ng_w.reshape(depth, d, g_kv, 3 * NSA_GROUP_SIZE)
    ng_w = jnp.pad(ng_w, ((0, 0), (0, 0), (0, 0), (0, GATE_ROWS_PER_GROUP - 3 * NSA_GROUP_SIZE)))
    ng_w = ng_w.reshape(depth, d, g_kv * GATE_ROWS_PER_GROUP)
    wn_all = jnp.concatenate([ks_w, kw_w, dk_w, kc_w, vc_w, ga_w, gb_w], axis=-1).astype(BF16)
    wt_all = jnp.concatenate([nq_w, dq_w, vs_w, vw_w, dv_w, ng_w], axis=-1)
    wt_all = jnp.swapaxes(wt_all, 1, 2).astype(BF16)

    w1r = cmp_w1.reshape(depth, 2, 2, CMP_STRIDE, HEAD_DIM, CMP_HIDDEN)
    w1_big = jnp.einsum('Lchldf,cC,gG->LhlcgdCGf', w1r, eye_c, eye_g)
    w1_big = w1_big.reshape(depth, 2, CMP_STRIDE * 2 * NSA_KV_COLS, 2 * g_kv * CMP_HIDDEN).astype(BF16)
    per = cmp_pe.reshape(depth, 2, 2, CMP_STRIDE, HEAD_DIM)
    pe_hb = jnp.einsum('Lchld,g->Lhlcgd', per, jnp.ones((g_kv,), F32))
    pe_hb = pe_hb.reshape(depth, 2, 1, CMP_STRIDE * 2 * NSA_KV_COLS)
    b1p = jnp.broadcast_to(cmp_b1[:, :, None, :], (depth, 2, g_kv, CMP_HIDDEN)).reshape(depth, 1, -1)
    w2k = jnp.einsum('Lfd,gG->LgfGd', cmp_w2[:, 0], eye_g).reshape(depth, g_kv * CMP_HIDDEN, NSA_KV_COLS)
    w2vt = jnp.einsum('Lfd,gG->LGdgf', cmp_w2[:, 1], eye_g).reshape(depth, NSA_KV_COLS, g_kv * CMP_HIDDEN)
    w2k = w2k.astype(BF16)
    w2vt = w2vt.astype(BF16)
    b2k = jnp.tile(cmp_b2[:, 0], (1, g_kv))[:, None, :]
    b2v = jnp.tile(cmp_b2[:, 1], (1, g_kv))[:, :, None]

    slopes = _alibi_slopes()
    nsa_srows = _slope_rows(slopes[:NSA_HEADS], NSA_GROUP_SIZE, CH)
    diff_srows = _slope_rows(np.repeat(slopes[NSA_HEADS:], 2), 2, DIFF_TILE_CHUNKS * CH)
    kaug = _key_aug_tables(seq, nb)
    kcaug = _cmp_aug_table(n_half)
    asel_t = jnp.asarray(_cmp_to_sel_t(n_half, nc, nb))

    wa_all = w_branch_a.astype(BF16)
    wb_all = w_branch_b.astype(BF16)
    wo_all = w_out.astype(BF16)

    wr = jnp.concatenate([router_grp_w, router_exp_w.reshape(depth, d, N_EXPERTS)], axis=-1)
    n_r = MOE_GROUPS + N_EXPERTS
    wr_t = jnp.pad(jnp.swapaxes(wr, 1, 2), ((0, 0), (0, 32 - n_r), (0, 0)))
    br = jnp.concatenate([router_grp_b, router_exp_b.reshape(depth, N_EXPERTS)], axis=-1)
    br = jnp.pad(br, ((0, 0), (0, 32 - n_r)))[:, :, None]
    eexp = jnp.asarray(np.kron(np.eye(N_EXPERTS, dtype=np.float32), np.ones((1, EXPERT_FF), np.float32)), BF16)
    wg_all = jnp.swapaxes(exp_w_gate, 1, 2).reshape(depth, d, N_EXPERTS * EXPERT_FF).astype(BF16)
    wu_all = jnp.swapaxes(exp_w_up, 1, 2).reshape(depth, d, N_EXPERTS * EXPERT_FF).astype(BF16)
    wd_all = exp_w_down.reshape(depth, N_EXPERTS * EXPERT_FF, d).astype(BF16)

    h = x.reshape(n, d)
    for l in range(depth):
        (ks, kw, dk, kcvc, gates, nq_t, dq_t, vs_t, vw_t, dv_t, ng_t) = _inproj(
            h, norm1_g[l][None, :], wn_all[l], wt_all[l])
        hb = kcvc.reshape(bsz, n_half, CMP_STRIDE * 2 * NSA_KV_COLS)
        kc, vc_t = _compress(hb, pe_hb[l], w1_big[l], b1p[l], w2k[l], b2k[l], w2vt[l], b2v[l])
        ya_t = _nsa(nq_t, nsa_srows, ks.reshape(n // CH, CH, NSA_KV_COLS), kw.reshape(n // CH, CH, NSA_KV_COLS),
                    vs_t, vw_t, kaug, kc, kcaug, vc_t, ng_t, asel_t, bsz, nq)
        lam_init = 0.8 - 0.6 * float(np.exp(-0.3 * l))
        scal = jnp.asarray([lam_init, 1.0 - lam_init], F32)
        yb_t = _diff(scal, dq_t, diff_srows, dk.reshape(n // CH, CH, DIFF_QK_COLS), dv_t, kaug[1],
                     diff_lambda[l], diff_subln_g[l][:, None], bsz, nq)
        h = _merge(ya_t, yb_t, gates, h, wa_all[l], wb_all[l], wo_all[l])
        h = _moe(h, norm2_g[l][None, :], wr_t[l], br[l], eexp, wg_all[l], wu_all[l], wd_all[l],
                 final_norm_g[None, :], final=(l == depth - 1))
    return h.reshape(bsz, seq, d)
```

```python
import functools

import numpy as np
import jax
import jax.numpy as jnp
from jax import lax
from jax.experimental import pallas as pl
from jax.experimental.pallas import tpu as pltpu

F32 = jnp.float32
BF16 = jnp.bfloat16

D_MODEL = 1024
HEAD_DIM = 64
NSA_HEADS = 8
NSA_KV_GROUPS = 2
NSA_GROUP_SIZE = NSA_HEADS // NSA_KV_GROUPS
CMP_BLOCK = 32
CMP_STRIDE = 16
CMP_HIDDEN = 128
SEL_BLOCK = 64
SEL_TOPK = 8
WINDOW = 512
FORCED_SCORE = 1e9
DIFF_HEADS = 4
MOE_GROUPS = 4
EXPERTS_PER_GROUP = 4
N_EXPERTS = MOE_GROUPS * EXPERTS_PER_GROUP
EXPERT_FF = D_MODEL // 8
RMS_EPS = 1e-6
SUBLN_EPS = 1e-5
NEG_INF = -1e30
N_ALIBI_HEADS = NSA_HEADS + DIFF_HEADS

NSA_Q_COLS = NSA_HEADS * HEAD_DIM
NSA_KV_COLS = NSA_KV_GROUPS * HEAD_DIM
NSA_GATE_COLS = 3 * NSA_HEADS
DIFF_QK_COLS = DIFF_HEADS * 2 * HEAD_DIM
DIFF_V_COLS = DIFF_HEADS * 2 * HEAD_DIM
GATE_ROWS_PER_GROUP = 16

CH = 256
DIFF_TILE_CHUNKS = 2
ROW_TILE = 512
VMEM_LIMIT = 56 * 1024 * 1024

LOG2E = float(np.log2(np.e))
Q_SCALE = HEAD_DIM ** -0.5 * LOG2E

QK_LANES = 2 * HEAD_DIM
AUG_LANES = 128
SLOPE_PIECES = 3
PAD_ROW = 2 * SLOPE_PIECES
ALIBI_ROWS = 16
MASK_ROW0 = ALIBI_ROWS
MASK_BIG = 1e30
SUM_ROWS = 16
BLOCK_COLS = 256

_NT = (((1,), (1,)), ((), ()))
_TN = (((0,), (0,)), ((), ()))


def _dot(a, b):
    return jnp.dot(a, b, preferred_element_type=F32)


def _const_spec(shape):
    nd = len(shape)
    return pl.BlockSpec(shape, lambda *_: (0,) * nd, pipeline_mode=pl.Buffered(1))


def _params(sem):
    return pltpu.CompilerParams(dimension_semantics=sem, vmem_limit_bytes=VMEM_LIMIT)


def _alibi_slopes():
    return 2.0 ** (-8.0 * np.arange(1, N_ALIBI_HEADS + 1) / N_ALIBI_HEADS)


_NAT_WIDTHS = (NSA_KV_COLS, NSA_KV_COLS, DIFF_QK_COLS, 2 * NSA_KV_COLS, 2 * D_MODEL)
_TR_ROWS = (NSA_Q_COLS, DIFF_QK_COLS, NSA_KV_COLS, NSA_KV_COLS, DIFF_V_COLS, 2 * GATE_ROWS_PER_GROUP)
_TR_SCALE = (Q_SCALE, Q_SCALE, 1.0, 1.0, 1.0, 1.0)


def _inproj_kernel(x_ref, g_ref, wn_ref, wt_ref, *out_refs):
    nat_refs = out_refs[:len(_NAT_WIDTHS)]
    tr_refs = out_refs[len(_NAT_WIDTHS):]
    x = x_ref[...]
    xn = (x * lax.rsqrt(jnp.mean(x * x, axis=-1, keepdims=True) + RMS_EPS) * g_ref[...]).astype(BF16)
    off = 0
    for ref, width in zip(nat_refs, _NAT_WIDTHS):
        for c in range(0, width, 512):
            cw = min(512, width - c)
            ref[:, c:c + cw] = _dot(xn, wn_ref[:, off + c:off + c + cw]).astype(ref.dtype)
        off += width
    n_sub = x.shape[0] // CH
    off = 0
    for ref, rows, scale in zip(tr_refs, _TR_ROWS, _TR_SCALE):
        for c in range(0, rows, 256):
            rw = min(256, rows - c)
            res = lax.dot_general(wt_ref[off + c:off + c + rw, :], xn, _NT, preferred_element_type=F32)
            if scale != 1.0:
                res = res * scale
            for j in range(n_sub):
                ref[j, c:c + rw, :] = res[:, j * CH:(j + 1) * CH].astype(ref.dtype)
        off += rows


def _inproj(h2d, g, wn, wt):
    n = h2d.shape[0]
    tm = ROW_TILE
    nat_dtypes = (BF16, BF16, BF16, F32, BF16)
    tr_dtypes = (BF16, BF16, BF16, BF16, BF16, F32)
    out_shape = [jax.ShapeDtypeStruct((n, w), dt) for w, dt in zip(_NAT_WIDTHS, nat_dtypes)]
    out_shape += [jax.ShapeDtypeStruct((n // CH, r, CH), dt) for r, dt in zip(_TR_ROWS, tr_dtypes)]
    out_specs = [pl.BlockSpec((tm, w), lambda i: (i, 0)) for w in _NAT_WIDTHS]
    out_specs += [pl.BlockSpec((tm // CH, r, CH), lambda i: (i, 0, 0)) for r in _TR_ROWS]
    return pl.pallas_call(
        _inproj_kernel,
        grid=(n // tm,),
        in_specs=[pl.BlockSpec((tm, D_MODEL), lambda i: (i, 0)),
                  _const_spec((1, D_MODEL)),
                  _const_spec(wn.shape),
                  _const_spec(wt.shape)],
        out_specs=out_specs,
        out_shape=out_shape,
        compiler_params=_params(("parallel",)),
        name="inproj",
    )(h2d, g, wn, wt)


def _compress_kernel(hb_ref, pe_ref, w1_ref, b1_ref, w2k_ref, b2k_ref, w2v_ref, b2v_ref, kc_ref, vct_ref):
    hb = hb_ref[0]
    rows = hb.shape[0]
    top = (hb + pe_ref[0]).astype(BF16)
    bot = (hb + pe_ref[1]).astype(BF16)
    p = _dot(top, w1_ref[0])
    q = _dot(bot, w1_ref[1])
    q_next = pltpu.roll(q, rows - 1, 0)
    hid = jax.nn.gelu(p + q_next + b1_ref[...])
    width = hid.shape[1] // 2
    kc_ref[0] = (_dot(hid[:, :width].astype(BF16), w2k_ref[...]) + b2k_ref[...]).astype(kc_ref.dtype)
    vct = lax.dot_general(w2v_ref[...], hid[:, width:].astype(BF16), _NT, preferred_element_type=F32)
    vct_ref[0] = (vct + b2v_ref[...]).astype(vct_ref.dtype)


def _compress(hb, pe_hb, w1_big, b1p, w2k, b2k, w2vt, b2v):
    bsz, rows, width = hb.shape
    gk = NSA_KV_COLS
    return pl.pallas_call(
        _compress_kernel,
        grid=(bsz,),
        in_specs=[pl.BlockSpec((1, rows, width), lambda b: (b, 0, 0)),
                  _const_spec(pe_hb.shape), _const_spec(w1_big.shape), _const_spec(b1p.shape),
                  _const_spec(w2k.shape), _const_spec(b2k.shape), _const_spec(w2vt.shape),
                  _const_spec(b2v.shape)],
        out_specs=[pl.BlockSpec((1, rows, gk), lambda b: (b, 0, 0)),
                   pl.BlockSpec((1, gk, rows), lambda b: (b, 0, 0))],
        out_shape=[jax.ShapeDtypeStruct((bsz, rows, gk), BF16),
                   jax.ShapeDtypeStruct((bsz, gk, rows), BF16)],
        compiler_params=_params(("parallel",)),
        name="compress",
    )(hb, pe_hb, w1_big, b1p, w2k, b2k, w2vt, b2v)


def _bf16_pieces(x):
    out = []
    rest = np.asarray(x, np.float32)
    for _ in range(SLOPE_PIECES):
        piece = rest.astype(BF16).astype(np.float32)
        out.append(piece)
        rest = rest - piece
    return out


def _slope_rows(slopes, heads_per_block, cols_per_head):
    sl2 = (np.asarray(slopes, np.float32).astype(np.float64) * LOG2E).astype(np.float32)
    pieces = np.stack(_bf16_pieces(sl2) * 2, axis=0)
    rows = np.zeros((ALIBI_ROWS, sl2.shape[0]), np.float32)
    rows[:pieces.shape[0]] = pieces
    rows[PAD_ROW] = -MASK_BIG
    rows = np.repeat(rows, cols_per_head, axis=1)
    rows = rows.reshape(ALIBI_ROWS, -1, heads_per_block * cols_per_head).transpose(1, 0, 2)
    return jnp.asarray(rows, BF16)


def _key_aug_tables(seq, nb):
    pos = np.arange(seq)
    aug = np.zeros((2, seq + CH, AUG_LANES), np.float32)
    aug[:, :seq, 0:SLOPE_PIECES] = (pos % CH)[None, :, None]
    aug[:, :seq, SLOPE_PIECES:2 * SLOPE_PIECES] = (pos // CH * CH)[None, :, None]
    aug[0, pos, MASK_ROW0 + pos // SEL_BLOCK] = 1.0
    aug[:, seq:, PAD_ROW] = 1.0
    return jnp.asarray(aug.reshape(2, seq // CH + 1, CH, AUG_LANES), BF16)


def _cmp_aug_table(n_rows):
    aug = np.zeros((n_rows, AUG_LANES), np.float32)
    aug[:, 0:SLOPE_PIECES] = (np.arange(n_rows) * CMP_STRIDE)[:, None]
    aug[:, SLOPE_PIECES:2 * SLOPE_PIECES] = CMP_BLOCK - 1
    return jnp.asarray(aug, BF16)


def _tile_lanes(x, reps):
    return jnp.concatenate([x] * reps, axis=1)


def _query_minus_key(reps):
    shape = (CH, reps * CH)
    q_off = jnp.bitwise_and(lax.broadcasted_iota(jnp.int32, shape, 1), CH - 1)
    return q_off - lax.broadcasted_iota(jnp.int32, shape, 0)


def _flash_init(m_ref, acc_ref):
    m_ref[...] = jnp.full(m_ref.shape, NEG_INF, F32)
    acc_ref[...] = jnp.zeros(acc_ref.shape, F32)


def _normalized(acc_ref, g):
    dv = acc_ref.shape[1] - SUM_ROWS
    return acc_ref[g, 0:dv, :] / acc_ref[g, dv:dv + 1, :]


def _chunk_scores(k_blk, aug_blk, qa_ref):
    return _dot(jnp.concatenate([k_blk, aug_blk], axis=1), qa_ref[...])


def _stage_and_consume(prod, cons, qa_ref, block_cols=BLOCK_COLS):
    if prod is not None:
        k_blk, aug_blk, (ps_ref, pmx_ref), pmask = prod
        k_full = jnp.concatenate([k_blk, aug_blk], axis=1)
    if cons is not None:
        (cs_ref, cmx_ref), v_t, (m_ref, acc_ref), cmask = cons
        _, rows, gcols = acc_ref.shape
        dv = rows - SUM_ROWS
        ones = jnp.ones((SUM_ROWS, v_t.shape[1]), BF16)
    for c0 in range(0, qa_ref.shape[1], block_cols):
        csl = slice(c0, c0 + block_cols)
        if prod is not None:
            s = _dot(k_full, qa_ref[:, csl])
            if pmask is not None:
                s = jnp.where(pmask(c0), s, NEG_INF)
            ps_ref[:, csl] = s
            pmx_ref[:, csl] = jnp.max(s, axis=0, keepdims=True)
        if cons is not None:
            s = cs_ref[:, csl]
            if cmask is None:
                mx = cmx_ref[:, csl]
            else:
                s = jnp.where(cmask(c0), s, NEG_INF)
                mx = jnp.max(s, axis=0, keepdims=True)
            m_prev = m_ref[:, csl]
            m_new = jnp.maximum(m_prev, mx)
            alpha = jnp.exp2(m_prev - m_new)
            p = jnp.exp2(s - m_new).astype(BF16)
            g = c0 // gcols
            gsl = slice(c0 - g * gcols, c0 - g * gcols + block_cols)
            v_ones = jnp.concatenate([v_t[g * dv:(g + 1) * dv], ones], axis=0)
            acc_ref[g, :, gsl] = alpha * acc_ref[g, :, gsl] + _dot(v_ones, p)
            m_ref[:, csl] = m_new


def _consume(buf, v_t, state, qa_ref, mask=None, block_cols=BLOCK_COLS):
    _stage_and_consume(None, (buf, v_t, state, mask), qa_ref, block_cols)


def _pad_or(aug_ref, c, is_pad):
    return aug_ref[jnp.where(is_pad, aug_ref.shape[0] - 1, c)]


def _causal_first(i, k_ref, aug_ref, buf0):
    return (k_ref[0], _pad_or(aug_ref, 0, jnp.bitwise_and(i, 1) == 1), buf0, None)


def _causal_pairs(i, k_ref, v_ref, aug_ref, qa_ref, bufs, state, block_cols=BLOCK_COLS):
    buf0, buf1 = bufs
    pad = jnp.bitwise_and(i, 1)

    def pair(k, carry):
        c = 2 * k - pad
        _stage_and_consume((k_ref[c + 1], aug_ref[c + 1], buf1, None),
                           (buf0, v_ref[jnp.maximum(c, 0)], state, None), qa_ref, block_cols)
        _stage_and_consume((k_ref[c + 2], aug_ref[c + 2], buf0, None),
                           (buf1, v_ref[c + 1], state, None), qa_ref, block_cols)
        return carry
    lax.fori_loop(0, jnp.right_shift(i + pad, 1), pair, 0)


def _unselected_mask_rows(imp, k_sel):
    n_blk, width = imp.shape
    rows_per = 8
    j_loc = lax.broadcasted_iota(jnp.int32, (rows_per, width), 0)
    mask_blocks = []
    for r0 in range(0, n_blk, rows_per):
        blk = imp[r0:r0 + rows_per, :]
        cnt = jnp.zeros((rows_per, width), jnp.int32)
        for jp in range(n_blk):
            row = imp[jp:jp + 1, :]
            gt = jnp.where(row > blk, 1, 0)
            ge = jnp.where(row >= blk, 1, 0)
            if jp >= r0 + rows_per - 1:
                cnt = cnt + gt
            elif jp < r0:
                cnt = cnt + ge
            else:
                cnt = cnt + jnp.where(j_loc + r0 > jp, ge, gt)
        mask_blocks.append(jnp.where(cnt < k_sel, 0.0, -MASK_BIG))
    return jnp.concatenate(mask_blocks, axis=0)


def _nsa_kernel(q_ref, srow_ref, ks_ref, kw_ref, vs_ref, vw_ref, kaug_ref, kc_ref, kcaug_ref, vc_ref,
                ng_ref, asel_ref, o_ref, qa_ref, ocmp_ref, s0, x0, s1, x1, w0, y0, w1, y1, w2, y2,
                m_s, acc_s, m_w, acc_w):
    i = pl.program_id(1)
    t0 = i * CH
    nh = NSA_GROUP_SIZE
    ng = NSA_KV_GROUPS
    gcols = nh * CH
    cols = ng * gcols
    n_cmp = kc_ref.shape[1]
    n_blk = asel_ref.shape[0]
    k_sel = min(SEL_TOPK, n_blk)
    bufs_s = ((s0, x0), (s1, x1))
    bufs_w = ((w0, y0), (w1, y1), (w2, y2))
    state_s = (m_s, acc_s)
    state_w = (m_w, acc_w)
    kaug_s = kaug_ref.at[0]
    kaug_w = kaug_ref.at[1]

    t_pos = t0 + lax.broadcasted_iota(jnp.int32, (1, CH), 1)
    d0_i = _query_minus_key(BLOCK_COLS // CH)
    causal = lambda c0: d0_i >= 0
    window_edge = lambda c0: d0_i < 0

    zeros_q = jnp.zeros((HEAD_DIM, CH), BF16)
    for g in range(ng):
        for hh in range(nh):
            h = g * nh + hh
            qh = q_ref[0, h * HEAD_DIM:(h + 1) * HEAD_DIM, :]
            for gg in range(ng):
                qa_ref[gg * HEAD_DIM:(gg + 1) * HEAD_DIM, h * CH:(h + 1) * CH] = qh if gg == g else zeros_q
    qa_ref[QK_LANES:QK_LANES + ALIBI_ROWS, :] = srow_ref[...]
    qa_ref[QK_LANES + MASK_ROW0:, :] = jnp.zeros((AUG_LANES - MASK_ROW0, cols), BF16)

    cmp_scores = _chunk_scores(kc_ref[0], kcaug_ref[...], qa_ref)

    n_back = WINDOW // CH
    stage_w = []
    chunks_w = []
    for back in range(n_back, 0, -1):
        c = jnp.maximum(i - back, 0)
        stage_w.append((kw_ref[c], _pad_or(kaug_w, c, i < back), bufs_w[n_back - back],
                        window_edge if back == n_back else None))
        chunks_w.append(c)
    stage_w.append((kw_ref[i], kaug_w[i], bufs_w[n_back], causal))
    chunks_w.append(i)
    _stage_and_consume(stage_w[0], None, qa_ref)

    n_idx = lax.broadcasted_iota(jnp.int32, (n_cmp, cols), 0)
    t_pos_all = t0 + jnp.bitwise_and(lax.broadcasted_iota(jnp.int32, (1, cols), 1), CH - 1)
    valid_c = n_idx * CMP_STRIDE + (CMP_BLOCK - 1) <= t_pos_all
    lg = jnp.where(valid_c, cmp_scores, NEG_INF)
    m = jnp.max(lg, axis=0, keepdims=True)
    p = jnp.where(valid_c, jnp.exp2(lg - m), 0.0)
    l = jnp.sum(p, axis=0, keepdims=True)
    pc = p * jnp.where(l > 0.0, 1.0 / l, 0.0)
    pc_b = pc.astype(BF16)
    vc = vc_ref[0]

    j_idx = lax.broadcasted_iota(jnp.int32, (n_blk, CH), 0)
    cur = jnp.right_shift(t_pos, SEL_BLOCK.bit_length() - 1)
    forced = (j_idx == 0) | (j_idx == cur) | (j_idx == cur - 1)
    in_past = j_idx * SEL_BLOCK <= t_pos
    for g in range(ng):
        gsl = slice(g * gcols, (g + 1) * gcols)
        ocmp_ref[g] = _dot(vc[g * HEAD_DIM:(g + 1) * HEAD_DIM], pc_b[:, gsl])
        psum = pc[:, g * gcols:g * gcols + CH]
        for hh in range(1, nh):
            psum = psum + pc[:, g * gcols + hh * CH:g * gcols + (hh + 1) * CH]
        imp = jnp.dot(asel_ref[...], psum, precision=lax.Precision.HIGHEST, preferred_element_type=F32)
        imp = jnp.where(in_past, jnp.where(forced, FORCED_SCORE, imp), -1.0)
        mask_rows = _unselected_mask_rows(imp, k_sel).astype(BF16)
        qa_ref[QK_LANES + MASK_ROW0:QK_LANES + MASK_ROW0 + n_blk, gsl] = _tile_lanes(mask_rows, nh)

    _flash_init(*state_w)
    for j in range(1, n_back + 1):
        _stage_and_consume(stage_w[j], (bufs_w[j - 1], vw_ref[chunks_w[j - 1]], state_w, None), qa_ref)
    _stage_and_consume(_causal_first(i, ks_ref, kaug_s, bufs_s[0]),
                       (bufs_w[n_back], vw_ref[i], state_w, None), qa_ref)
    _flash_init(*state_s)
    _causal_pairs(i, ks_ref, vs_ref, kaug_s, qa_ref, bufs_s, state_s)
    _consume(bufs_s[0], vs_ref[i], state_s, qa_ref, mask=causal)

    for g in range(ng):
        def gate(branch):
            r0 = g * GATE_ROWS_PER_GROUP + branch * nh
            return jax.nn.sigmoid(jnp.concatenate([ng_ref[0, r0 + hh:r0 + hh + 1, :] for hh in range(nh)], axis=1))
        out = gate(0) * ocmp_ref[g] + gate(1) * _normalized(acc_s, g) + gate(2) * _normalized(acc_w, g)
        for hh in range(nh):
            h = g * nh + hh
            o_ref[0, h * HEAD_DIM:(h + 1) * HEAD_DIM, :] = out[:, hh * CH:(hh + 1) * CH].astype(o_ref.dtype)


def _nsa(nq_t, srows, ks3, kw3, vs_t, vw_t, kaug, kc, kcaug, vc_t, ng_t, asel_t, bsz, nq):
    n_cmp = kc.shape[1]
    ng = NSA_KV_GROUPS
    gcols = NSA_GROUP_SIZE * CH
    cols = ng * gcols
    acc_shape = (ng, HEAD_DIM + SUM_ROWS, gcols)
    stage = [pltpu.VMEM((CH, cols), F32), pltpu.VMEM((1, cols), F32)]
    return pl.pallas_call(
        _nsa_kernel,
        grid=(bsz, nq),
        in_specs=[
            pl.BlockSpec((1, NSA_Q_COLS, CH), lambda b, i: (b * nq + i, 0, 0)),
            pl.BlockSpec(srows.shape, lambda b, i: (0, 0)),
            pl.BlockSpec((nq, CH, NSA_KV_COLS), lambda b, i: (b, 0, 0)),
            pl.BlockSpec((nq, CH, NSA_KV_COLS), lambda b, i: (b, 0, 0)),
            pl.BlockSpec((nq, NSA_KV_COLS, CH), lambda b, i: (b, 0, 0)),
            pl.BlockSpec((nq, NSA_KV_COLS, CH), lambda b, i: (b, 0, 0)),
            pl.BlockSpec(kaug.shape, lambda b, i: (0, 0, 0, 0)),
            pl.BlockSpec((1, n_cmp, NSA_KV_COLS), lambda b, i: (b, 0, 0)),
            pl.BlockSpec(kcaug.shape, lambda b, i: (0, 0)),
            pl.BlockSpec((1, NSA_KV_COLS, n_cmp), lambda b, i: (b, 0, 0)),
            pl.BlockSpec((1, ng * GATE_ROWS_PER_GROUP, CH), lambda b, i: (b * nq + i, 0, 0)),
            pl.BlockSpec(asel_t.shape, lambda b, i: (0, 0)),
        ],
        out_specs=pl.BlockSpec((1, NSA_Q_COLS, CH), lambda b, i: (b * nq + i, 0, 0)),
        out_shape=jax.ShapeDtypeStruct((bsz * nq, NSA_Q_COLS, CH), BF16),
        scratch_shapes=(
            [pltpu.VMEM((QK_LANES + AUG_LANES, cols), BF16), pltpu.VMEM((ng, HEAD_DIM, gcols), F32)]
            + stage * 5
            + [pltpu.VMEM((1, cols), F32), pltpu.VMEM(acc_shape, F32),
               pltpu.VMEM((1, cols), F32), pltpu.VMEM(acc_shape, F32)]),
        compiler_params=_params(("parallel", "arbitrary")),
        name="nsa_attention",
    )(nq_t, srows, ks3, kw3, vs_t, vw_t, kaug, kc, kcaug, vc_t, ng_t, asel_t)


def _diff_kernel(scal_ref, q_ref, srow_ref, k_ref, v_ref, kaug_ref, lam_ref, gain_ref, o_ref,
                 qa_ref, s0, x0, s1, x1, m_r, acc_r):
    nt = DIFF_TILE_CHUNKS
    tq = nt * CH
    c0 = pl.program_id(2) * nt
    lam_init = scal_ref[0]
    out_scale = scal_ref[1]
    sub2 = lax.broadcasted_iota(jnp.int32, (QK_LANES, CH), 0)
    zero = jnp.zeros((), BF16)
    for j in range(nt):
        q = q_ref[j]
        qa_ref[0:QK_LANES, j * CH:(j + 1) * CH] = jnp.where(sub2 < HEAD_DIM, q, zero)
        qa_ref[0:QK_LANES, tq + j * CH:tq + (j + 1) * CH] = jnp.where(sub2 >= HEAD_DIM, q, zero)
    qa_ref[QK_LANES:QK_LANES + ALIBI_ROWS, :] = srow_ref[0]
    qa_ref[QK_LANES + MASK_ROW0:, :] = jnp.zeros((AUG_LANES - MASK_ROW0, 2 * tq), BF16)
    bc = 2 * tq
    lane = lax.broadcasted_iota(jnp.int32, (CH, bc), 1)
    sub = lax.broadcasted_iota(jnp.int32, (CH, bc), 0)

    def on_or_after(first_key):
        return lambda col0: jnp.bitwise_and(lane + col0, tq - 1) - sub >= first_key

    state = (m_r, acc_r)
    bufs = ((s0, x0), (s1, x1))
    _stage_and_consume(_causal_first(c0, k_ref, kaug_ref, bufs[0]), None, qa_ref, bc)
    _flash_init(*state)
    _causal_pairs(c0, k_ref, v_ref, kaug_ref, qa_ref, bufs, state, bc)
    _stage_and_consume((k_ref[c0 + 1], kaug_ref[c0 + 1], bufs[1], on_or_after(CH)),
                       (bufs[0], v_ref[c0], state, on_or_after(0)), qa_ref, bc)
    _consume(bufs[1], v_ref[c0 + 1], state, qa_ref, block_cols=bc)

    lp = lam_ref[...]
    lam = (jnp.exp(jnp.sum(lp[0:1] * lp[1:2], axis=1, keepdims=True))
           - jnp.exp(jnp.sum(lp[2:3] * lp[3:4], axis=1, keepdims=True)) + lam_init)
    att = _normalized(acc_r, 0)
    o = att[:, 0:tq] - lam * att[:, tq:2 * tq]
    o = o * lax.rsqrt(jnp.mean(o * o, axis=0, keepdims=True) + SUBLN_EPS) * gain_ref[...]
    o = (o * out_scale).astype(o_ref.dtype)
    for j in range(nt):
        o_ref[j] = o[:, j * CH:(j + 1) * CH]


def _diff(scal, dq_t, srows, dk3, dv_t, kaug_plain, lam_p, gain, bsz, nq):
    hd2 = 2 * HEAD_DIM
    nt = DIFF_TILE_CHUNKS
    n_tiles = nq // nt
    cols = 2 * nt * CH
    grid_spec = pltpu.PrefetchScalarGridSpec(
        num_scalar_prefetch=1,
        grid=(bsz, DIFF_HEADS, n_tiles),
        in_specs=[
            pl.BlockSpec((nt, hd2, CH), lambda b, h, i, s: (b * n_tiles + i, h, 0)),
            pl.BlockSpec((1, ALIBI_ROWS, cols), lambda b, h, i, s: (h, 0, 0)),
            pl.BlockSpec((nq, CH, hd2), lambda b, h, i, s: (b, 0, h)),
            pl.BlockSpec((nq, hd2, CH), lambda b, h, i, s: (b, h, 0)),
            pl.BlockSpec(kaug_plain.shape, lambda b, h, i, s: (0, 0, 0)),
            pl.BlockSpec(lam_p.shape, lambda b, h, i, s: (0, 0)),
            pl.BlockSpec(gain.shape, lambda b, h, i, s: (0, 0)),
        ],
        out_specs=pl.BlockSpec((nt, hd2, CH), lambda b, h, i, s: (b * n_tiles + i, h, 0)),
        scratch_shapes=[
            pltpu.VMEM((QK_LANES + AUG_LANES, cols), BF16),
            pltpu.VMEM((CH, cols), F32), pltpu.VMEM((1, cols), F32),
            pltpu.VMEM((CH, cols), F32), pltpu.VMEM((1, cols), F32),
            pltpu.VMEM((1, cols), F32), pltpu.VMEM((1, hd2 + SUM_ROWS, cols), F32),
        ],
    )
    return pl.pallas_call(
        _diff_kernel,
        grid_spec=grid_spec,
        out_shape=jax.ShapeDtypeStruct((bsz * nq, DIFF_V_COLS, CH), BF16),
        compiler_params=_params(("parallel", "parallel", "arbitrary")),
        name="diff_attention",
    )(scal, dq_t, srows, dk3, dv_t, kaug_plain, lam_p, gain)


def _merge_kernel(ya_ref, yb_ref, gates_ref, h_ref, wa_ref, wb_ref, wo_ref, o_ref):
    for j in range(ya_ref.shape[0]):
        rows = slice(j * CH, (j + 1) * CH)
        a = lax.dot_general(ya_ref[j], wa_ref[...], _TN, preferred_element_type=F32)
        b = lax.dot_general(yb_ref[j], wb_ref[...], _TN, preferred_element_type=F32)
        ga = jax.nn.sigmoid(gates_ref[rows, :D_MODEL].astype(F32))
        gb = jax.nn.sigmoid(gates_ref[rows, D_MODEL:].astype(F32))
        merged = (ga * a + gb * b).astype(BF16)
        o_ref[rows, :] = h_ref[rows, :] + _dot(merged, wo_ref[...])


def _merge(ya_t, yb_t, gates, h2d, wa, wb, wo):
    n = h2d.shape[0]
    tm = ROW_TILE
    return pl.pallas_call(
        _merge_kernel,
        grid=(n // tm,),
        in_specs=[pl.BlockSpec((tm // CH, NSA_Q_COLS, CH), lambda i: (i, 0, 0)),
                  pl.BlockSpec((tm // CH, DIFF_V_COLS, CH), lambda i: (i, 0, 0)),
                  pl.BlockSpec((tm, 2 * D_MODEL), lambda i: (i, 0)),
                  pl.BlockSpec((tm, D_MODEL), lambda i: (i, 0)),
                  _const_spec(wa.shape), _const_spec(wb.shape), _const_spec(wo.shape)],
        out_specs=pl.BlockSpec((tm, D_MODEL), lambda i: (i, 0)),
        out_shape=jax.ShapeDtypeStruct((n, D_MODEL), F32),
        compiler_params=_params(("parallel",)),
        name="merge_outproj",
    )(ya_t, yb_t, gates, h2d, wa, wb, wo)


def _first_argmax(x, rows, n):
    mx = jnp.max(x, axis=0, keepdims=True)
    idx = jnp.min(jnp.where(x == mx, rows, n), axis=0, keepdims=True)
    return mx, idx


def _moe_kernel(h_ref, g_ref, wr_ref, br_ref, eexp_ref, wg_ref, wu_ref, wd_ref, fg_ref, o_ref, *, final):
    hres = h_ref[...]
    xf = hres * lax.rsqrt(jnp.mean(hres * hres, axis=-1, keepdims=True) + RMS_EPS) * g_ref[...]
    xb = xf.astype(BF16)
    tm = hres.shape[0]

    x_lo = (xf - xb.astype(F32)).astype(BF16)
    n_r = wr_ref.shape[0] // 2
    part = lax.dot_general(wr_ref[...], xb, _NT, preferred_element_type=F32)
    logits = (part[0:n_r] + part[n_r:2 * n_r]
              + lax.dot_general(wr_ref[0:n_r, :], x_lo, _NT, preferred_element_type=F32)
              + br_ref[...])
    gl = logits[0:MOE_GROUPS]
    rows_g = lax.broadcasted_iota(jnp.int32, (MOE_GROUPS, tm), 0)
    gmax, gidx = _first_argmax(gl, rows_g, MOE_GROUPS)
    g_w = 1.0 / jnp.sum(jnp.exp(gl - gmax), axis=0, keepdims=True)
    esel = jnp.zeros((EXPERTS_PER_GROUP, tm), F32)
    for gg in range(MOE_GROUPS):
        lo = MOE_GROUPS + gg * EXPERTS_PER_GROUP
        esel = jnp.where(gidx == gg, logits[lo:lo + EXPERTS_PER_GROUP], esel)
    rows_e = lax.broadcasted_iota(jnp.int32, (EXPERTS_PER_GROUP, tm), 0)
    v1, i1 = _first_argmax(esel, rows_e, EXPERTS_PER_GROUP)
    rest = jnp.where(rows_e == i1, -jnp.inf, esel)
    v2, i2 = _first_argmax(rest, rows_e, EXPERTS_PER_GROUP)
    e21 = jnp.exp(v2 - v1)
    w1 = g_w / (1.0 + e21)
    w2 = g_w * e21 / (1.0 + e21)
    rows_c = lax.broadcasted_iota(jnp.int32, (N_EXPERTS, tm), 0)
    grp_c = jnp.right_shift(rows_c, EXPERTS_PER_GROUP.bit_length() - 1)
    exp_c = jnp.bitwise_and(rows_c, EXPERTS_PER_GROUP - 1)
    comb = jnp.where(grp_c == gidx,
                     jnp.where(exp_c == i1, w1, 0.0) + jnp.where(exp_c == i2, w2, 0.0), 0.0)
    comb_hi = comb.astype(BF16)
    comb_lo = (comb - comb_hi.astype(F32)).astype(BF16)

    comb_nat = (lax.dot_general(comb_hi, eexp_ref[...], _TN, preferred_element_type=F32)
                + lax.dot_general(comb_lo, eexp_ref[...], _TN, preferred_element_type=F32))

    acc = hres
    n_ff = wg_ref.shape[1]
    step = 512
    for c in range(0, n_ff, step):
        cols = slice(c, c + step)
        hg = _dot(xb, wg_ref[:, cols])
        hu = _dot(xb, wu_ref[:, cols])
        act = jax.nn.silu(hg) * hu
        parts = []
        for e0 in range(0, step, EXPERT_FF):
            e = (c + e0) // EXPERT_FF
            parts.append((act[:, e0:e0 + EXPERT_FF] * comb_nat[:, e:e + 1]).astype(BF16))
        acc = acc + _dot(jnp.concatenate(parts, axis=1), wd_ref[cols, :])
    if final:
        acc = acc * lax.rsqrt(jnp.mean(acc * acc, axis=-1, keepdims=True) + RMS_EPS) * fg_ref[...]
    o_ref[...] = acc


def _moe(h2d, g, wr_t, br, eexp, wg, wu, wd, fg, final):
    n = h2d.shape[0]
    tm = ROW_TILE
    return pl.pallas_call(
        functools.partial(_moe_kernel, final=final),
        grid=(n // tm,),
        in_specs=[pl.BlockSpec((tm, D_MODEL), lambda i: (i, 0)),
                  _const_spec(g.shape), _const_spec(wr_t.shape), _const_spec(br.shape),
                  _const_spec(eexp.shape), _const_spec(wg.shape), _const_spec(wu.shape),
                  _const_spec(wd.shape), _const_spec(fg.shape)],
        out_specs=pl.BlockSpec((tm, D_MODEL), lambda i: (i, 0)),
        out_shape=jax.ShapeDtypeStruct((n, D_MODEL), F32),
        compiler_params=_params(("parallel",)),
        name="moe_final" if final else "moe",
    )(h2d, g, wr_t, br, eexp, wg, wu, wd, fg)


def _split_points():
    sizes = ([NSA_Q_COLS] + [NSA_KV_COLS] * 6
             + [NSA_GATE_COLS, DIFF_QK_COLS, DIFF_QK_COLS, DIFF_V_COLS, D_MODEL, D_MODEL])
    return [int(v) for v in np.cumsum(sizes)[:-1]]


def _cmp_to_sel_t(n_rows, nc, nb):
    c0 = np.arange(nc)[:, None] * CMP_STRIDE
    s0 = np.arange(nb)[None, :] * SEL_BLOCK
    ov = np.maximum(0, np.minimum(c0 + CMP_BLOCK, s0 + SEL_BLOCK) - np.maximum(c0, s0)) / CMP_BLOCK
    out = np.zeros((nb, n_rows), np.float32)
    out[:, :nc] = ov.T
    return out


def kernel(x, norm1_g, w_in, cmp_pe, cmp_w1, cmp_b1, cmp_w2, cmp_b2, diff_lambda, diff_subln_g, w_branch_a, w_branch_b, w_out, norm2_g, router_grp_w, router_grp_b, router_exp_w, router_exp_b, exp_w_gate, exp_w_up, exp_w_down, final_norm_g):
    bsz, seq, d = x.shape
    depth = w_in.shape[0]
    n = bsz * seq
    nq = seq // CH
    n_half = seq // CMP_STRIDE
    nc = (seq - CMP_BLOCK) // CMP_STRIDE + 1
    nb = seq // SEL_BLOCK
    assert d == D_MODEL and seq % ROW_TILE == 0 and seq >= WINDOW and WINDOW % CH == 0
    assert nb % 16 == 0 and MASK_ROW0 + nb <= AUG_LANES and n_half <= 256
    g_kv = NSA_KV_GROUPS
    eye_g = jnp.eye(g_kv, dtype=F32)
    eye_c = jnp.eye(2, dtype=F32)

    (nq_w, kc_w, vc_w, ks_w, vs_w, kw_w, vw_w, ng_w, dq_w, dk_w, dv_w, ga_w, gb_w) = jnp.split(
        w_in, _split_points(), axis=-1)
    ng_w = ng_w.reshape(depth, d, g_kv, NSA_GROUP_SIZE, 3).transpose(0, 1, 2, 4, 3)
    ng_w = ng_w.reshape(depth, d, g_kv, 3 * NSA_GROUP_SIZE)
    ng_w = jnp.pad(ng_w, ((0, 0), (0, 0), (0, 0), (0, GATE_ROWS_PER_GROUP - 3 * NSA_GROUP_SIZE)))
    ng_w = ng_w.reshape(depth, d, g_kv * GATE_ROWS_PER_GROUP)
    wn_all = jnp.concatenate([ks_w, kw_w, dk_w, kc_w, vc_w, ga_w, gb_w], axis=-1).astype(BF16)
    wt_all = jnp.concatenate([nq_w, dq_w, vs_w, vw_w, dv_w, ng_w], axis=-1)
    wt_all = jnp.swapaxes(wt_all, 1, 2).astype(BF16)

    w1r = cmp_w1.reshape(depth, 2, 2, CMP_STRIDE, HEAD_DIM, CMP_HIDDEN)
    w1_big = jnp.einsum('Lchldf,cC,gG->LhlcgdCGf', w1r, eye_c, eye_g)
    w1_big = w1_big.reshape(depth, 2, CMP_STRIDE * 2 * NSA_KV_COLS, 2 * g_kv * CMP_HIDDEN).astype(BF16)
    per = cmp_pe.reshape(depth, 2, 2, CMP_STRIDE, HEAD_DIM)
    pe_hb = jnp.einsum('Lchld,g->Lhlcgd', per, jnp.ones((g_kv,), F32))
    pe_hb = pe_hb.reshape(depth, 2, 1, CMP_STRIDE * 2 * NSA_KV_COLS)
    b1p = jnp.broadcast_to(cmp_b1[:, :, None, :], (depth, 2, g_kv, CMP_HIDDEN)).reshape(depth, 1, -1)
    w2k = jnp.einsum('Lfd,gG->LgfGd', cmp_w2[:, 0], eye_g).reshape(depth, g_kv * CMP_HIDDEN, NSA_KV_COLS)
    w2vt = jnp.einsum('Lfd,gG->LGdgf', cmp_w2[:, 1], eye_g).reshape(depth, NSA_KV_COLS, g_kv * CMP_HIDDEN)
    w2k = w2k.astype(BF16)
    w2vt = w2vt.astype(BF16)
    b2k = jnp.tile(cmp_b2[:, 0], (1, g_kv))[:, None, :]
    b2v = jnp.tile(cmp_b2[:, 1], (1, g_kv))[:, :, None]

    slopes = _alibi_slopes()
    nsa_srows = _slope_rows(slopes[:NSA_HEADS], NSA_HEADS, CH)[0]
    diff_srows = _slope_rows(np.repeat(slopes[NSA_HEADS:], 2), 2, DIFF_TILE_CHUNKS * CH)
    kaug = _key_aug_tables(seq, nb)
    kcaug = _cmp_aug_table(n_half)
    asel_t = jnp.asarray(_cmp_to_sel_t(n_half, nc, nb))

    wa_all = w_branch_a.astype(BF16)
    wb_all = w_branch_b.astype(BF16)
    wo_all = w_out.astype(BF16)

    wr = jnp.concatenate([router_grp_w, router_exp_w.reshape(depth, d, N_EXPERTS)], axis=-1)
    n_r = MOE_GROUPS + N_EXPERTS
    wr_t = jnp.pad(jnp.swapaxes(wr, 1, 2), ((0, 0), (0, 32 - n_r), (0, 0)))
    wr_hi = wr_t.astype(BF16)
    wr_t = jnp.concatenate([wr_hi, (wr_t - wr_hi.astype(F32)).astype(BF16)], axis=1)
    br = jnp.concatenate([router_grp_b, router_exp_b.reshape(depth, N_EXPERTS)], axis=-1)
    br = jnp.pad(br, ((0, 0), (0, 32 - n_r)))[:, :, None]
    eexp = jnp.asarray(np.eye(N_EXPERTS, 128, dtype=np.float32), BF16)
    wg_all = jnp.swapaxes(exp_w_gate, 1, 2).reshape(depth, d, N_EXPERTS * EXPERT_FF).astype(BF16)
    wu_all = jnp.swapaxes(exp_w_up, 1, 2).reshape(depth, d, N_EXPERTS * EXPERT_FF).astype(BF16)
    wd_all = exp_w_down.reshape(depth, N_EXPERTS * EXPERT_FF, d).astype(BF16)

    h = x.reshape(n, d)
    for l in range(depth):
        (ks, kw, dk, kcvc, gates, nq_t, dq_t, vs_t, vw_t, dv_t, ng_t) = _inproj(
            h, norm1_g[l][None, :], wn_all[l], wt_all[l])
        hb = kcvc.reshape(bsz, n_half, CMP_STRIDE * 2 * NSA_KV_COLS)
        kc, vc_t = _compress(hb, pe_hb[l], w1_big[l], b1p[l], w2k[l], b2k[l], w2vt[l], b2v[l])
        ya_t = _nsa(nq_t, nsa_srows, ks.reshape(n // CH, CH, NSA_KV_COLS), kw.reshape(n // CH, CH, NSA_KV_COLS),
                    vs_t, vw_t, kaug, kc, kcaug, vc_t, ng_t, asel_t, bsz, nq)
        lam_init = 0.8 - 0.6 * float(np.exp(-0.3 * l))
        scal = jnp.asarray([lam_init, 1.0 - lam_init], F32)
        yb_t = _diff(scal, dq_t, diff_srows, dk.reshape(n // CH, CH, DIFF_QK_COLS), dv_t, kaug[1],
                     diff_lambda[l], diff_subln_g[l][:, None], bsz, nq)
        h = _merge(ya_t, yb_t, gates, h, wa_all[l], wb_all[l], wo_all[l])
        h = _moe(h, norm2_g[l][None, :], wr_t[l], br[l], eexp, wg_all[l], wu_all[l], wd_all[l],
                 final_norm_g[None, :], final=(l == depth - 1))
    return h.reshape(bsz, seq, d)
```

```python
import functools

import numpy as np
import jax
import jax.numpy as jnp
from jax import lax
from jax.experimental import pallas as pl
from jax.experimental.pallas import tpu as pltpu

F32 = jnp.float32
BF16 = jnp.bfloat16

D_MODEL = 1024
HEAD_DIM = 64
NSA_HEADS = 8
NSA_KV_GROUPS = 2
NSA_GROUP_SIZE = NSA_HEADS // NSA_KV_GROUPS
CMP_BLOCK = 32
CMP_STRIDE = 16
CMP_HIDDEN = 128
SEL_BLOCK = 64
SEL_TOPK = 8
WINDOW = 512
FORCED_SCORE = 1e9
DIFF_HEADS = 4
MOE_GROUPS = 4
EXPERTS_PER_GROUP = 4
N_EXPERTS = MOE_GROUPS * EXPERTS_PER_GROUP
EXPERT_FF = D_MODEL // 8
RMS_EPS = 1e-6
SUBLN_EPS = 1e-5
NEG_INF = -1e30
N_ALIBI_HEADS = NSA_HEADS + DIFF_HEADS

NSA_Q_COLS = NSA_HEADS * HEAD_DIM
NSA_KV_COLS = NSA_KV_GROUPS * HEAD_DIM
NSA_GATE_COLS = 3 * NSA_HEADS
DIFF_QK_COLS = DIFF_HEADS * 2 * HEAD_DIM
DIFF_V_COLS = DIFF_HEADS * 2 * HEAD_DIM
GATE_ROWS_PER_GROUP = 16

CH = 256
DIFF_TILE_CHUNKS = 2
ROW_TILE = 512
VMEM_LIMIT = 56 * 1024 * 1024

LOG2E = float(np.log2(np.e))
Q_SCALE = HEAD_DIM ** -0.5 * LOG2E

QK_LANES = 2 * HEAD_DIM
AUG_LANES = 128
SLOPE_PIECES = 3
PAD_ROW = 2 * SLOPE_PIECES
ALIBI_ROWS = 16
MASK_ROW0 = ALIBI_ROWS
MASK_BIG = 1e30
SUM_ROWS = 16
BLOCK_COLS = 256

_NT = (((1,), (1,)), ((), ()))
_TN = (((0,), (0,)), ((), ()))


def _dot(a, b):
    return jnp.dot(a, b, preferred_element_type=F32)


def _const_spec(shape):
    nd = len(shape)
    return pl.BlockSpec(shape, lambda *_: (0,) * nd, pipeline_mode=pl.Buffered(1))


def _layer_spec(stacked, layer):
    nd = stacked.ndim - 1
    return pl.BlockSpec((None,) + tuple(stacked.shape[1:]), lambda *_: (layer,) + (0,) * nd,
                        pipeline_mode=pl.Buffered(1))


def _params(sem):
    return pltpu.CompilerParams(dimension_semantics=sem, vmem_limit_bytes=VMEM_LIMIT)


def _alibi_slopes():
    return 2.0 ** (-8.0 * np.arange(1, N_ALIBI_HEADS + 1) / N_ALIBI_HEADS)


_NAT_WIDTHS = (NSA_KV_COLS, NSA_KV_COLS, DIFF_QK_COLS, 2 * NSA_KV_COLS, 2 * D_MODEL)
_TR_ROWS = (NSA_Q_COLS, DIFF_QK_COLS, NSA_KV_COLS, NSA_KV_COLS, DIFF_V_COLS, 2 * GATE_ROWS_PER_GROUP)
_TR_SCALE = (Q_SCALE, Q_SCALE, 1.0, 1.0, 1.0, 1.0)
_KCVC_INDEX = 3


def _inproj_kernel(x_ref, g_ref, wn_ref, wt_ref, *refs):
    n_out = len(_NAT_WIDTHS) + len(_TR_ROWS)
    nat_refs = refs[:len(_NAT_WIDTHS)]
    tr_refs = refs[len(_NAT_WIDTHS):n_out]
    rows_k, rows_v = refs[n_out:]
    x = x_ref[...]
    xn = (x * lax.rsqrt(jnp.mean(x * x, axis=-1, keepdims=True) + RMS_EPS) * g_ref[...]).astype(BF16)
    off = 0
    for idx, (ref, width) in enumerate(zip(nat_refs, _NAT_WIDTHS)):
        if idx == _KCVC_INDEX:
            res = _dot(xn, wn_ref[:, off:off + width])
            rows_k[...] = res[:, :NSA_KV_COLS]
            rows_v[...] = res[:, NSA_KV_COLS:]
            n_rows = res.shape[0] // CMP_STRIDE
            for tok in range(CMP_STRIDE):
                lo = tok * width
                ref[:, lo:lo + NSA_KV_COLS] = rows_k[pl.ds(tok, n_rows, stride=CMP_STRIDE), :]
                ref[:, lo + NSA_KV_COLS:lo + width] = rows_v[pl.ds(tok, n_rows, stride=CMP_STRIDE), :]
        else:
            for c in range(0, width, 512):
                cw = min(512, width - c)
                ref[:, c:c + cw] = _dot(xn, wn_ref[:, off + c:off + c + cw]).astype(ref.dtype)
        off += width
    n_sub = x.shape[0] // CH
    off = 0
    for ref, rows, scale in zip(tr_refs, _TR_ROWS, _TR_SCALE):
        for c in range(0, rows, 256):
            rw = min(256, rows - c)
            res = lax.dot_general(wt_ref[off + c:off + c + rw, :], xn, _NT, preferred_element_type=F32)
            if scale != 1.0:
                res = res * scale
            for j in range(n_sub):
                ref[j, c:c + rw, :] = res[:, j * CH:(j + 1) * CH].astype(ref.dtype)
        off += rows


def _inproj(h2d, g, wn, wt, layer):
    n = h2d.shape[0]
    tm = ROW_TILE
    nat_dtypes = (BF16, BF16, BF16, F32, BF16)
    tr_dtypes = (BF16, BF16, BF16, BF16, BF16, F32)
    out_shape = [jax.ShapeDtypeStruct((n, w), dt) for w, dt in zip(_NAT_WIDTHS, nat_dtypes)]
    out_shape += [jax.ShapeDtypeStruct((n // CH, r, CH), dt) for r, dt in zip(_TR_ROWS, tr_dtypes)]
    out_specs = [pl.BlockSpec((tm, w), lambda i: (i, 0)) for w in _NAT_WIDTHS]
    out_specs += [pl.BlockSpec((tm // CH, r, CH), lambda i: (i, 0, 0)) for r in _TR_ROWS]
    hb_width = CMP_STRIDE * _NAT_WIDTHS[_KCVC_INDEX]
    out_shape[_KCVC_INDEX] = jax.ShapeDtypeStruct((n // CMP_STRIDE, hb_width), F32)
    out_specs[_KCVC_INDEX] = pl.BlockSpec((tm // CMP_STRIDE, hb_width), lambda i: (i, 0))
    return pl.pallas_call(
        _inproj_kernel,
        grid=(n // tm,),
        in_specs=[pl.BlockSpec((tm, D_MODEL), lambda i: (i, 0)),
                  _layer_spec(g, layer), _layer_spec(wn, layer), _layer_spec(wt, layer)],
        out_specs=out_specs,
        out_shape=out_shape,
        scratch_shapes=[pltpu.VMEM((tm, NSA_KV_COLS), F32), pltpu.VMEM((tm, NSA_KV_COLS), F32)],
        compiler_params=_params(("parallel",)),
        name="inproj",
    )(h2d, g, wn, wt)


def _compress_kernel(hb_ref, pe_ref, w1_ref, b1_ref, w2k_ref, b2k_ref, w2v_ref, b2v_ref, kc_ref, vct_ref):
    hb = hb_ref[0]
    rows = hb.shape[0]
    top = (hb + pe_ref[0]).astype(BF16)
    bot = (hb + pe_ref[1]).astype(BF16)
    p = _dot(top, w1_ref[0])
    q = _dot(bot, w1_ref[1])
    q_next = pltpu.roll(q, rows - 1, 0)
    hid = jax.nn.gelu(p + q_next + b1_ref[...])
    width = hid.shape[1] // 2
    kc_ref[0] = (_dot(hid[:, :width].astype(BF16), w2k_ref[...]) + b2k_ref[...]).astype(kc_ref.dtype)
    vct = lax.dot_general(w2v_ref[...], hid[:, width:].astype(BF16), _NT, preferred_element_type=F32)
    vct_ref[0] = (vct + b2v_ref[...]).astype(vct_ref.dtype)


def _compress(hb, pe_hb, w1_big, b1p, w2k, b2k, w2vt, b2v, layer):
    bsz, rows, width = hb.shape
    gk = NSA_KV_COLS
    return pl.pallas_call(
        _compress_kernel,
        grid=(bsz,),
        in_specs=[pl.BlockSpec((1, rows, width), lambda b: (b, 0, 0)),
                  *[_layer_spec(a, layer) for a in (pe_hb, w1_big, b1p, w2k, b2k, w2vt, b2v)]],
        out_specs=[pl.BlockSpec((1, rows, gk), lambda b: (b, 0, 0)),
                   pl.BlockSpec((1, gk, rows), lambda b: (b, 0, 0))],
        out_shape=[jax.ShapeDtypeStruct((bsz, rows, gk), BF16),
                   jax.ShapeDtypeStruct((bsz, gk, rows), BF16)],
        compiler_params=_params(("parallel",)),
        name="compress",
    )(hb, pe_hb, w1_big, b1p, w2k, b2k, w2vt, b2v)


def _bf16_pieces(x):
    out = []
    rest = np.asarray(x, np.float32)
    for _ in range(SLOPE_PIECES):
        piece = rest.astype(BF16).astype(np.float32)
        out.append(piece)
        rest = rest - piece
    return out


def _slope_rows(slopes, heads_per_block, cols_per_head):
    sl2 = (np.asarray(slopes, np.float32).astype(np.float64) * LOG2E).astype(np.float32)
    pieces = np.stack(_bf16_pieces(sl2) * 2, axis=0)
    rows = np.zeros((ALIBI_ROWS, sl2.shape[0]), np.float32)
    rows[:pieces.shape[0]] = pieces
    rows[PAD_ROW] = -MASK_BIG
    rows = np.repeat(rows, cols_per_head, axis=1)
    rows = rows.reshape(ALIBI_ROWS, -1, heads_per_block * cols_per_head).transpose(1, 0, 2)
    return jnp.asarray(rows, BF16)


def _key_aug_tables(seq, nb):
    pos = np.arange(seq)
    aug = np.zeros((2, seq + CH, AUG_LANES), np.float32)
    aug[:, :seq, 0:SLOPE_PIECES] = (pos % CH)[None, :, None]
    aug[:, :seq, SLOPE_PIECES:2 * SLOPE_PIECES] = (pos // CH * CH)[None, :, None]
    aug[0, pos, MASK_ROW0 + pos // SEL_BLOCK] = 1.0
    aug[:, seq:, PAD_ROW] = 1.0
    return jnp.asarray(aug.reshape(2, seq // CH + 1, CH, AUG_LANES), BF16)


def _cmp_aug_table(n_rows):
    aug = np.zeros((n_rows, AUG_LANES), np.float32)
    aug[:, 0:SLOPE_PIECES] = (np.arange(n_rows) * CMP_STRIDE)[:, None]
    aug[:, SLOPE_PIECES:2 * SLOPE_PIECES] = CMP_BLOCK - 1
    return jnp.asarray(aug, BF16)


def _tile_lanes(x, reps):
    return jnp.concatenate([x] * reps, axis=1)


def _query_minus_key(reps):
    shape = (CH, reps * CH)
    q_off = jnp.bitwise_and(lax.broadcasted_iota(jnp.int32, shape, 1), CH - 1)
    return q_off - lax.broadcasted_iota(jnp.int32, shape, 0)


def _flash_init(m_ref, acc_ref):
    m_ref[...] = jnp.full(m_ref.shape, NEG_INF, F32)
    acc_ref[...] = jnp.zeros(acc_ref.shape, F32)


def _normalized(acc_ref, g):
    dv = acc_ref.shape[1] - SUM_ROWS
    return acc_ref[g, 0:dv, :] / acc_ref[g, dv:dv + 1, :]


def _chunk_scores(k_blk, aug_blk, qa_ref):
    return _dot(jnp.concatenate([k_blk, aug_blk], axis=1), qa_ref[...])


def _stage_and_consume(prod, cons, qa_ref, block_cols=BLOCK_COLS):
    if prod is not None:
        k_blk, aug_blk, (ps_ref, pmx_ref), pmask = prod
        k_full = jnp.concatenate([k_blk, aug_blk], axis=1)
    if cons is not None:
        (cs_ref, cmx_ref), v_t, (m_ref, acc_ref), cmask = cons
        _, rows, gcols = acc_ref.shape
        dv = rows - SUM_ROWS
        ones = jnp.ones((SUM_ROWS, v_t.shape[1]), BF16)
    for c0 in range(0, qa_ref.shape[1], block_cols):
        csl = slice(c0, c0 + block_cols)
        if prod is not None:
            s = _dot(k_full, qa_ref[:, csl])
            if pmask is not None:
                s = jnp.where(pmask(c0), s, NEG_INF)
            ps_ref[:, csl] = s
            pmx_ref[:, csl] = jnp.max(s, axis=0, keepdims=True)
        if cons is not None:
            s = cs_ref[:, csl]
            if cmask is None:
                mx = cmx_ref[:, csl]
            else:
                s = jnp.where(cmask(c0), s, NEG_INF)
                mx = jnp.max(s, axis=0, keepdims=True)
            m_prev = m_ref[:, csl]
            m_new = jnp.maximum(m_prev, mx)
            alpha = jnp.exp2(m_prev - m_new)
            p = jnp.exp2(s - m_new).astype(BF16)
            g = c0 // gcols
            gsl = slice(c0 - g * gcols, c0 - g * gcols + block_cols)
            v_ones = jnp.concatenate([v_t[g * dv:(g + 1) * dv], ones], axis=0)
            acc_ref[g, :, gsl] = alpha * acc_ref[g, :, gsl] + _dot(v_ones, p)
            m_ref[:, csl] = m_new


def _consume(buf, v_t, state, qa_ref, mask=None, block_cols=BLOCK_COLS):
    _stage_and_consume(None, (buf, v_t, state, mask), qa_ref, block_cols)


def _pad_or(aug_ref, c, is_pad):
    return aug_ref[jnp.where(is_pad, aug_ref.shape[0] - 1, c)]


def _causal_first(i, k_ref, aug_ref, buf0):
    return (k_ref[0], _pad_or(aug_ref, 0, jnp.bitwise_and(i, 1) == 1), buf0, None)


def _causal_pairs(i, k_ref, v_ref, aug_ref, qa_ref, bufs, state, block_cols=BLOCK_COLS):
    buf0, buf1 = bufs
    pad = jnp.bitwise_and(i, 1)

    def pair(k, carry):
        c = 2 * k - pad
        _stage_and_consume((k_ref[c + 1], aug_ref[c + 1], buf1, None),
                           (buf0, v_ref[jnp.maximum(c, 0)], state, None), qa_ref, block_cols)
        _stage_and_consume((k_ref[c + 2], aug_ref[c + 2], buf0, None),
                           (buf1, v_ref[c + 1], state, None), qa_ref, block_cols)
        return carry
    lax.fori_loop(0, jnp.right_shift(i + pad, 1), pair, 0)


def _unselected_mask_rows(imp, k_sel):
    n_blk, width = imp.shape
    rows_per = 8
    j_loc = lax.broadcasted_iota(jnp.int32, (rows_per, width), 0)
    mask_blocks = []
    for r0 in range(0, n_blk, rows_per):
        blk = imp[r0:r0 + rows_per, :]
        cnt = jnp.zeros((rows_per, width), jnp.int32)
        for jp in range(n_blk):
            row = imp[jp:jp + 1, :]
            gt = jnp.where(row > blk, 1, 0)
            ge = jnp.where(row >= blk, 1, 0)
            if jp >= r0 + rows_per - 1:
                cnt = cnt + gt
            elif jp < r0:
                cnt = cnt + ge
            else:
                cnt = cnt + jnp.where(j_loc + r0 > jp, ge, gt)
        mask_blocks.append(jnp.where(cnt < k_sel, 0.0, -MASK_BIG))
    return jnp.concatenate(mask_blocks, axis=0)


def _nsa_kernel(q_ref, srow_ref, ks_ref, kw_ref, vs_ref, vw_ref, kaug_ref, kc_ref, kcaug_ref, vc_ref,
                ng_ref, asel_ref, o_ref, qa_ref, ocmp_ref, s0, x0, s1, x1, w0, y0, w1, y1, w2, y2,
                m_s, acc_s, m_w, acc_w):
    i = pl.program_id(1)
    t0 = i * CH
    nh = NSA_GROUP_SIZE
    ng = NSA_KV_GROUPS
    gcols = nh * CH
    cols = ng * gcols
    n_cmp = kc_ref.shape[1]
    n_blk = asel_ref.shape[0]
    k_sel = min(SEL_TOPK, n_blk)
    bufs_s = ((s0, x0), (s1, x1))
    bufs_w = ((w0, y0), (w1, y1), (w2, y2))
    state_s = (m_s, acc_s)
    state_w = (m_w, acc_w)
    kaug_s = kaug_ref.at[0]
    kaug_w = kaug_ref.at[1]

    t_pos = t0 + lax.broadcasted_iota(jnp.int32, (1, CH), 1)
    d0_i = _query_minus_key(BLOCK_COLS // CH)
    causal = lambda c0: d0_i >= 0
    window_edge = lambda c0: d0_i < 0

    zeros_q = jnp.zeros((HEAD_DIM, CH), BF16)
    for g in range(ng):
        for hh in range(nh):
            h = g * nh + hh
            qh = q_ref[0, h * HEAD_DIM:(h + 1) * HEAD_DIM, :]
            for gg in range(ng):
                qa_ref[gg * HEAD_DIM:(gg + 1) * HEAD_DIM, h * CH:(h + 1) * CH] = qh if gg == g else zeros_q
    qa_ref[QK_LANES:QK_LANES + ALIBI_ROWS, :] = srow_ref[...]
    qa_ref[QK_LANES + MASK_ROW0:, :] = jnp.zeros((AUG_LANES - MASK_ROW0, cols), BF16)

    cmp_scores = _chunk_scores(kc_ref[0], kcaug_ref[...], qa_ref)

    n_back = WINDOW // CH
    stage_w = []
    chunks_w = []
    for back in range(n_back, 0, -1):
        c = jnp.maximum(i - back, 0)
        stage_w.append((kw_ref[c], _pad_or(kaug_w, c, i < back), bufs_w[n_back - back],
                        window_edge if back == n_back else None))
        chunks_w.append(c)
    stage_w.append((kw_ref[i], kaug_w[i], bufs_w[n_back], causal))
    chunks_w.append(i)
    _stage_and_consume(stage_w[0], None, qa_ref)

    n_idx = lax.broadcasted_iota(jnp.int32, (n_cmp, cols), 0)
    t_pos_all = t0 + jnp.bitwise_and(lax.broadcasted_iota(jnp.int32, (1, cols), 1), CH - 1)
    valid_c = n_idx * CMP_STRIDE + (CMP_BLOCK - 1) <= t_pos_all
    lg = jnp.where(valid_c, cmp_scores, NEG_INF)
    m = jnp.max(lg, axis=0, keepdims=True)
    p = jnp.where(valid_c, jnp.exp2(lg - m), 0.0)
    l = jnp.sum(p, axis=0, keepdims=True)
    pc = p * jnp.where(l > 0.0, 1.0 / l, 0.0)
    pc_b = pc.astype(BF16)
    vc = vc_ref[0]

    j_idx = lax.broadcasted_iota(jnp.int32, (n_blk, CH), 0)
    cur = jnp.right_shift(t_pos, SEL_BLOCK.bit_length() - 1)
    forced = (j_idx == 0) | (j_idx == cur) | (j_idx == cur - 1)
    in_past = j_idx * SEL_BLOCK <= t_pos
    for g in range(ng):
        gsl = slice(g * gcols, (g + 1) * gcols)
        ocmp_ref[g] = _dot(vc[g * HEAD_DIM:(g + 1) * HEAD_DIM], pc_b[:, gsl])
        psum = pc[:, g * gcols:g * gcols + CH]
        for hh in range(1, nh):
            psum = psum + pc[:, g * gcols + hh * CH:g * gcols + (hh + 1) * CH]
        imp = jnp.dot(asel_ref[...], psum, precision=lax.Precision.HIGHEST, preferred_element_type=F32)
        imp = jnp.where(in_past, jnp.where(forced, FORCED_SCORE, imp), -1.0)
        mask_rows = _unselected_mask_rows(imp, k_sel).astype(BF16)
        qa_ref[QK_LANES + MASK_ROW0:QK_LANES + MASK_ROW0 + n_blk, gsl] = _tile_lanes(mask_rows, nh)

    _flash_init(*state_w)
    for j in range(1, n_back + 1):
        _stage_and_consume(stage_w[j], (bufs_w[j - 1], vw_ref[chunks_w[j - 1]], state_w, None), qa_ref)
    _stage_and_consume(_causal_first(i, ks_ref, kaug_s, bufs_s[0]),
                       (bufs_w[n_back], vw_ref[i], state_w, None), qa_ref)
    _flash_init(*state_s)
    _causal_pairs(i, ks_ref, vs_ref, kaug_s, qa_ref, bufs_s, state_s)
    _consume(bufs_s[0], vs_ref[i], state_s, qa_ref, mask=causal)

    for g in range(ng):
        def gate(branch):
            r0 = g * GATE_ROWS_PER_GROUP + branch * nh
            return jax.nn.sigmoid(jnp.concatenate([ng_ref[0, r0 + hh:r0 + hh + 1, :] for hh in range(nh)], axis=1))
        out = gate(0) * ocmp_ref[g] + gate(1) * _normalized(acc_s, g) + gate(2) * _normalized(acc_w, g)
        for hh in range(nh):
            h = g * nh + hh
            o_ref[0, h * HEAD_DIM:(h + 1) * HEAD_DIM, :] = out[:, hh * CH:(hh + 1) * CH].astype(o_ref.dtype)


def _nsa(nq_t, srows, ks3, kw3, vs_t, vw_t, kaug, kc, kcaug, vc_t, ng_t, asel_t, bsz, nq):
    n_cmp = kc.shape[1]
    ng = NSA_KV_GROUPS
    gcols = NSA_GROUP_SIZE * CH
    cols = ng * gcols
    acc_shape = (ng, HEAD_DIM + SUM_ROWS, gcols)
    stage = [pltpu.VMEM((CH, cols), F32), pltpu.VMEM((1, cols), F32)]
    return pl.pallas_call(
        _nsa_kernel,
        grid=(bsz, nq),
        in_specs=[
            pl.BlockSpec((1, NSA_Q_COLS, CH), lambda b, i: (b * nq + i, 0, 0)),
            pl.BlockSpec(srows.shape, lambda b, i: (0, 0)),
            pl.BlockSpec((nq, CH, NSA_KV_COLS), lambda b, i: (b, 0, 0)),
            pl.BlockSpec((nq, CH, NSA_KV_COLS), lambda b, i: (b, 0, 0)),
            pl.BlockSpec((nq, NSA_KV_COLS, CH), lambda b, i: (b, 0, 0)),
            pl.BlockSpec((nq, NSA_KV_COLS, CH), lambda b, i: (b, 0, 0)),
            pl.BlockSpec(kaug.shape, lambda b, i: (0, 0, 0, 0)),
            pl.BlockSpec((1, n_cmp, NSA_KV_COLS), lambda b, i: (b, 0, 0)),
            pl.BlockSpec(kcaug.shape, lambda b, i: (0, 0)),
            pl.BlockSpec((1, NSA_KV_COLS, n_cmp), lambda b, i: (b, 0, 0)),
            pl.BlockSpec((1, ng * GATE_ROWS_PER_GROUP, CH), lambda b, i: (b * nq + i, 0, 0)),
            pl.BlockSpec(asel_t.shape, lambda b, i: (0, 0)),
        ],
        out_specs=pl.BlockSpec((1, NSA_Q_COLS, CH), lambda b, i: (b * nq + i, 0, 0)),
        out_shape=jax.ShapeDtypeStruct((bsz * nq, NSA_Q_COLS, CH), BF16),
        scratch_shapes=(
            [pltpu.VMEM((QK_LANES + AUG_LANES, cols), BF16), pltpu.VMEM((ng, HEAD_DIM, gcols), F32)]
            + stage * 5
            + [pltpu.VMEM((1, cols), F32), pltpu.VMEM(acc_shape, F32),
               pltpu.VMEM((1, cols), F32), pltpu.VMEM(acc_shape, F32)]),
        compiler_params=_params(("parallel", "arbitrary")),
        name="nsa_attention",
    )(nq_t, srows, ks3, kw3, vs_t, vw_t, kaug, kc, kcaug, vc_t, ng_t, asel_t)


def _diff_kernel(scal_ref, q_ref, srow_ref, k_ref, v_ref, kaug_ref, lam_ref, gain_ref, o_ref,
                 qa_ref, s0, x0, s1, x1, m_r, acc_r):
    nt = DIFF_TILE_CHUNKS
    tq = nt * CH
    c0 = pl.program_id(2) * nt
    lam_init = scal_ref[0]
    out_scale = scal_ref[1]
    sub2 = lax.broadcasted_iota(jnp.int32, (QK_LANES, CH), 0)
    zero = jnp.zeros((), BF16)
    for j in range(nt):
        q = q_ref[j]
        qa_ref[0:QK_LANES, j * CH:(j + 1) * CH] = jnp.where(sub2 < HEAD_DIM, q, zero)
        qa_ref[0:QK_LANES, tq + j * CH:tq + (j + 1) * CH] = jnp.where(sub2 >= HEAD_DIM, q, zero)
    qa_ref[QK_LANES:QK_LANES + ALIBI_ROWS, :] = srow_ref[0]
    qa_ref[QK_LANES + MASK_ROW0:, :] = jnp.zeros((AUG_LANES - MASK_ROW0, 2 * tq), BF16)
    bc = 2 * tq
    lane = lax.broadcasted_iota(jnp.int32, (CH, bc), 1)
    sub = lax.broadcasted_iota(jnp.int32, (CH, bc), 0)

    def on_or_after(first_key):
        return lambda col0: jnp.bitwise_and(lane + col0, tq - 1) - sub >= first_key

    state = (m_r, acc_r)
    bufs = ((s0, x0), (s1, x1))
    _stage_and_consume(_causal_first(c0, k_ref, kaug_ref, bufs[0]), None, qa_ref, bc)
    _flash_init(*state)
    _causal_pairs(c0, k_ref, v_ref, kaug_ref, qa_ref, bufs, state, bc)
    _stage_and_consume((k_ref[c0 + 1], kaug_ref[c0 + 1], bufs[1], on_or_after(CH)),
                       (bufs[0], v_ref[c0], state, on_or_after(0)), qa_ref, bc)
    _consume(bufs[1], v_ref[c0 + 1], state, qa_ref, block_cols=bc)

    lp = lam_ref[...]
    lam = (jnp.exp(jnp.sum(lp[0:1] * lp[1:2], axis=1, keepdims=True))
           - jnp.exp(jnp.sum(lp[2:3] * lp[3:4], axis=1, keepdims=True)) + lam_init)
    att = _normalized(acc_r, 0)
    o = att[:, 0:tq] - lam * att[:, tq:2 * tq]
    o = o * lax.rsqrt(jnp.mean(o * o, axis=0, keepdims=True) + SUBLN_EPS) * gain_ref[...]
    o = (o * out_scale).astype(o_ref.dtype)
    for j in range(nt):
        o_ref[j] = o[:, j * CH:(j + 1) * CH]


def _diff(scal, dq_t, srows, dk3, dv_t, kaug_plain, lam_p, gain, bsz, nq):
    hd2 = 2 * HEAD_DIM
    nt = DIFF_TILE_CHUNKS
    n_tiles = nq // nt
    cols = 2 * nt * CH
    grid_spec = pltpu.PrefetchScalarGridSpec(
        num_scalar_prefetch=1,
        grid=(bsz, DIFF_HEADS, n_tiles),
        in_specs=[
            pl.BlockSpec((nt, hd2, CH), lambda b, h, i, s: (b * n_tiles + i, h, 0)),
            pl.BlockSpec((1, ALIBI_ROWS, cols), lambda b, h, i, s: (h, 0, 0)),
            pl.BlockSpec((nq, CH, hd2), lambda b, h, i, s: (b, 0, h)),
            pl.BlockSpec((nq, hd2, CH), lambda b, h, i, s: (b, h, 0)),
            pl.BlockSpec(kaug_plain.shape, lambda b, h, i, s: (0, 0, 0)),
            pl.BlockSpec(lam_p.shape, lambda b, h, i, s: (0, 0)),
            pl.BlockSpec(gain.shape, lambda b, h, i, s: (0, 0)),
        ],
        out_specs=pl.BlockSpec((nt, hd2, CH), lambda b, h, i, s: (b * n_tiles + i, h, 0)),
        scratch_shapes=[
            pltpu.VMEM((QK_LANES + AUG_LANES, cols), BF16),
            pltpu.VMEM((CH, cols), F32), pltpu.VMEM((1, cols), F32),
            pltpu.VMEM((CH, cols), F32), pltpu.VMEM((1, cols), F32),
            pltpu.VMEM((1, cols), F32), pltpu.VMEM((1, hd2 + SUM_ROWS, cols), F32),
        ],
    )
    return pl.pallas_call(
        _diff_kernel,
        grid_spec=grid_spec,
        out_shape=jax.ShapeDtypeStruct((bsz * nq, DIFF_V_COLS, CH), BF16),
        compiler_params=_params(("parallel", "parallel", "arbitrary")),
        name="diff_attention",
    )(scal, dq_t, srows, dk3, dv_t, kaug_plain, lam_p, gain)


def _merge_kernel(ya_ref, yb_ref, gates_ref, h_ref, wa_ref, wb_ref, wo_ref, o_ref):
    for j in range(ya_ref.shape[0]):
        rows = slice(j * CH, (j + 1) * CH)
        a = lax.dot_general(ya_ref[j], wa_ref[...], _TN, preferred_element_type=F32)
        b = lax.dot_general(yb_ref[j], wb_ref[...], _TN, preferred_element_type=F32)
        ga = jax.nn.sigmoid(gates_ref[rows, :D_MODEL].astype(F32))
        gb = jax.nn.sigmoid(gates_ref[rows, D_MODEL:].astype(F32))
        merged = (ga * a + gb * b).astype(BF16)
        o_ref[rows, :] = h_ref[rows, :] + _dot(merged, wo_ref[...])


def _merge(ya_t, yb_t, gates, h2d, wa, wb, wo, layer):
    n = h2d.shape[0]
    tm = ROW_TILE
    return pl.pallas_call(
        _merge_kernel,
        grid=(n // tm,),
        in_specs=[pl.BlockSpec((tm // CH, NSA_Q_COLS, CH), lambda i: (i, 0, 0)),
                  pl.BlockSpec((tm // CH, DIFF_V_COLS, CH), lambda i: (i, 0, 0)),
                  pl.BlockSpec((tm, 2 * D_MODEL), lambda i: (i, 0)),
                  pl.BlockSpec((tm, D_MODEL), lambda i: (i, 0)),
                  _layer_spec(wa, layer), _layer_spec(wb, layer), _layer_spec(wo, layer)],
        out_specs=pl.BlockSpec((tm, D_MODEL), lambda i: (i, 0)),
        out_shape=jax.ShapeDtypeStruct((n, D_MODEL), F32),
        compiler_params=_params(("parallel",)),
        name="merge_outproj",
    )(ya_t, yb_t, gates, h2d, wa, wb, wo)


def _first_argmax(x, rows, n):
    mx = jnp.max(x, axis=0, keepdims=True)
    idx = jnp.min(jnp.where(x == mx, rows, n), axis=0, keepdims=True)
    return mx, idx


def _moe_kernel(h_ref, g_ref, wr_ref, br_ref, eexp_ref, wg_ref, wu_ref, wd_ref, fg_ref, o_ref, *, final):
    hres = h_ref[...]
    xf = hres * lax.rsqrt(jnp.mean(hres * hres, axis=-1, keepdims=True) + RMS_EPS) * g_ref[...]
    xb = xf.astype(BF16)
    tm = hres.shape[0]

    x_lo = (xf - xb.astype(F32)).astype(BF16)
    n_r = wr_ref.shape[0] // 2
    part = lax.dot_general(wr_ref[...], xb, _NT, preferred_element_type=F32)
    logits = (part[0:n_r] + part[n_r:2 * n_r]
              + lax.dot_general(wr_ref[0:n_r, :], x_lo, _NT, preferred_element_type=F32)
              + br_ref[...])
    gl = logits[0:MOE_GROUPS]
    rows_g = lax.broadcasted_iota(jnp.int32, (MOE_GROUPS, tm), 0)
    gmax, gidx = _first_argmax(gl, rows_g, MOE_GROUPS)
    g_w = 1.0 / jnp.sum(jnp.exp(gl - gmax), axis=0, keepdims=True)
    esel = jnp.zeros((EXPERTS_PER_GROUP, tm), F32)
    for gg in range(MOE_GROUPS):
        lo = MOE_GROUPS + gg * EXPERTS_PER_GROUP
        esel = jnp.where(gidx == gg, logits[lo:lo + EXPERTS_PER_GROUP], esel)
    rows_e = lax.broadcasted_iota(jnp.int32, (EXPERTS_PER_GROUP, tm), 0)
    v1, i1 = _first_argmax(esel, rows_e, EXPERTS_PER_GROUP)
    rest = jnp.where(rows_e == i1, -jnp.inf, esel)
    v2, i2 = _first_argmax(rest, rows_e, EXPERTS_PER_GROUP)
    e21 = jnp.exp(v2 - v1)
    w1 = g_w / (1.0 + e21)
    w2 = g_w * e21 / (1.0 + e21)
    rows_c = lax.broadcasted_iota(jnp.int32, (N_EXPERTS, tm), 0)
    grp_c = jnp.right_shift(rows_c, EXPERTS_PER_GROUP.bit_length() - 1)
    exp_c = jnp.bitwise_and(rows_c, EXPERTS_PER_GROUP - 1)
    comb = jnp.where(grp_c == gidx,
                     jnp.where(exp_c == i1, w1, 0.0) + jnp.where(exp_c == i2, w2, 0.0), 0.0)
    comb_hi = comb.astype(BF16)
    comb_lo = (comb - comb_hi.astype(F32)).astype(BF16)

    comb_nat = (lax.dot_general(comb_hi, eexp_ref[...], _TN, preferred_element_type=F32)
                + lax.dot_general(comb_lo, eexp_ref[...], _TN, preferred_element_type=F32))

    acc = hres
    n_ff = wg_ref.shape[1]
    step = 512
    for c in range(0, n_ff, step):
        cols = slice(c, c + step)
        hg = _dot(xb, wg_ref[:, cols])
        hu = _dot(xb, wu_ref[:, cols])
        act = jax.nn.silu(hg) * hu
        parts = []
        for e0 in range(0, step, EXPERT_FF):
            e = (c + e0) // EXPERT_FF
            parts.append((act[:, e0:e0 + EXPERT_FF] * comb_nat[:, e:e + 1]).astype(BF16))
        acc = acc + _dot(jnp.concatenate(parts, axis=1), wd_ref[cols, :])
    if final:
        acc = acc * lax.rsqrt(jnp.mean(acc * acc, axis=-1, keepdims=True) + RMS_EPS) * fg_ref[...]
    o_ref[...] = acc


def _moe(h2d, g, wr_t, br, eexp, wg, wu, wd, fg, layer, final):
    n = h2d.shape[0]
    tm = ROW_TILE
    return pl.pallas_call(
        functools.partial(_moe_kernel, final=final),
        grid=(n // tm,),
        in_specs=[pl.BlockSpec((tm, D_MODEL), lambda i: (i, 0)),
                  _layer_spec(g, layer), _layer_spec(wr_t, layer), _layer_spec(br, layer),
                  _const_spec(eexp.shape), _layer_spec(wg, layer), _layer_spec(wu, layer),
                  _layer_spec(wd, layer), _const_spec(fg.shape)],
        out_specs=pl.BlockSpec((tm, D_MODEL), lambda i: (i, 0)),
        out_shape=jax.ShapeDtypeStruct((n, D_MODEL), F32),
        compiler_params=_params(("parallel",)),
        name="moe_final" if final else "moe",
    )(h2d, g, wr_t, br, eexp, wg, wu, wd, fg)


def _split_points():
    sizes = ([NSA_Q_COLS] + [NSA_KV_COLS] * 6
             + [NSA_GATE_COLS, DIFF_QK_COLS, DIFF_QK_COLS, DIFF_V_COLS, D_MODEL, D_MODEL])
    return [int(v) for v in np.cumsum(sizes)[:-1]]


def _cmp_to_sel_t(n_rows, nc, nb):
    c0 = np.arange(nc)[:, None] * CMP_STRIDE
    s0 = np.arange(nb)[None, :] * SEL_BLOCK
    ov = np.maximum(0, np.minimum(c0 + CMP_BLOCK, s0 + SEL_BLOCK) - np.maximum(c0, s0)) / CMP_BLOCK
    out = np.zeros((nb, n_rows), np.float32)
    out[:, :nc] = ov.T
    return out


def kernel(x, norm1_g, w_in, cmp_pe, cmp_w1, cmp_b1, cmp_w2, cmp_b2, diff_lambda, diff_subln_g, w_branch_a, w_branch_b, w_out, norm2_g, router_grp_w, router_grp_b, router_exp_w, router_exp_b, exp_w_gate, exp_w_up, exp_w_down, final_norm_g):
    bsz, seq, d = x.shape
    depth = w_in.shape[0]
    n = bsz * seq
    nq = seq // CH
    n_half = seq // CMP_STRIDE
    nc = (seq - CMP_BLOCK) // CMP_STRIDE + 1
    nb = seq // SEL_BLOCK
    assert d == D_MODEL and seq % ROW_TILE == 0 and seq >= WINDOW and WINDOW % CH == 0
    assert nb % 16 == 0 and MASK_ROW0 + nb <= AUG_LANES and n_half <= 256
    g_kv = NSA_KV_GROUPS
    eye_g = jnp.eye(g_kv, dtype=F32)
    eye_c = jnp.eye(2, dtype=F32)

    (nq_w, kc_w, vc_w, ks_w, vs_w, kw_w, vw_w, ng_w, dq_w, dk_w, dv_w, ga_w, gb_w) = jnp.split(
        w_in, _split_points(), axis=-1)
    ng_w = ng_w.reshape(depth, d, g_kv, NSA_GROUP_SIZE, 3).transpose(0, 1, 2, 4, 3)
    ng_w = ng_w.reshape(depth, d, g_kv, 3 * NSA_GROUP_SIZE)
    ng_w = jnp.pad(ng_w, ((0, 0), (0, 0), (0, 0), (0, GATE_ROWS_PER_GROUP - 3 * NSA_GROUP_SIZE)))
    ng_w = ng_w.reshape(depth, d, g_kv * GATE_ROWS_PER_GROUP)
    wn_all = jnp.concatenate([ks_w, kw_w, dk_w, kc_w, vc_w, ga_w, gb_w], axis=-1).astype(BF16)
    wt_all = jnp.concatenate([nq_w, dq_w, vs_w, vw_w, dv_w, ng_w], axis=-1)
    wt_all = jnp.swapaxes(wt_all, 1, 2).astype(BF16)

    w1r = cmp_w1.reshape(depth, 2, 2, CMP_STRIDE, HEAD_DIM, CMP_HIDDEN)
    w1_big = jnp.einsum('Lchldf,cC,gG->LhlcgdCGf', w1r, eye_c, eye_g)
    w1_big = w1_big.reshape(depth, 2, CMP_STRIDE * 2 * NSA_KV_COLS, 2 * g_kv * CMP_HIDDEN).astype(BF16)
    per = cmp_pe.reshape(depth, 2, 2, CMP_STRIDE, HEAD_DIM)
    pe_hb = jnp.einsum('Lchld,g->Lhlcgd', per, jnp.ones((g_kv,), F32))
    pe_hb = pe_hb.reshape(depth, 2, 1, CMP_STRIDE * 2 * NSA_KV_COLS)
    b1p = jnp.broadcast_to(cmp_b1[:, :, None, :], (depth, 2, g_kv, CMP_HIDDEN)).reshape(depth, 1, -1)
    w2k = jnp.einsum('Lfd,gG->LgfGd', cmp_w2[:, 0], eye_g).reshape(depth, g_kv * CMP_HIDDEN, NSA_KV_COLS)
    w2vt = jnp.einsum('Lfd,gG->LGdgf', cmp_w2[:, 1], eye_g).reshape(depth, NSA_KV_COLS, g_kv * CMP_HIDDEN)
    w2k = w2k.astype(BF16)
    w2vt = w2vt.astype(BF16)
    b2k = jnp.tile(cmp_b2[:, 0], (1, g_kv))[:, None, :]
    b2v = jnp.tile(cmp_b2[:, 1], (1, g_kv))[:, :, None]

    slopes = _alibi_slopes()
    nsa_srows = _slope_rows(slopes[:NSA_HEADS], NSA_HEADS, CH)[0]
    diff_srows = _slope_rows(np.repeat(slopes[NSA_HEADS:], 2), 2, DIFF_TILE_CHUNKS * CH)
    kaug = _key_aug_tables(seq, nb)
    kcaug = _cmp_aug_table(n_half)
    asel_t = jnp.asarray(_cmp_to_sel_t(n_half, nc, nb))

    wa_all = w_branch_a.astype(BF16)
    wb_all = w_branch_b.astype(BF16)
    wo_all = w_out.astype(BF16)

    wr = jnp.concatenate([router_grp_w, router_exp_w.reshape(depth, d, N_EXPERTS)], axis=-1)
    n_r = MOE_GROUPS + N_EXPERTS
    wr_t = jnp.pad(jnp.swapaxes(wr, 1, 2), ((0, 0), (0, 32 - n_r), (0, 0)))
    wr_hi = wr_t.astype(BF16)
    wr_t = jnp.concatenate([wr_hi, (wr_t - wr_hi.astype(F32)).astype(BF16)], axis=1)
    br = jnp.concatenate([router_grp_b, router_exp_b.reshape(depth, N_EXPERTS)], axis=-1)
    br = jnp.pad(br, ((0, 0), (0, 32 - n_r)))[:, :, None]
    eexp = jnp.asarray(np.eye(N_EXPERTS, 128, dtype=np.float32), BF16)
    wg_all = jnp.swapaxes(exp_w_gate, 1, 2).reshape(depth, d, N_EXPERTS * EXPERT_FF).astype(BF16)
    wu_all = jnp.swapaxes(exp_w_up, 1, 2).reshape(depth, d, N_EXPERTS * EXPERT_FF).astype(BF16)
    wd_all = exp_w_down.reshape(depth, N_EXPERTS * EXPERT_FF, d).astype(BF16)

    h = x.reshape(n, d)
    for l in range(depth):
        (ks, kw, dk, kcvc_hb, gates, nq_t, dq_t, vs_t, vw_t, dv_t, ng_t) = _inproj(
            h, norm1_g[:, None, :], wn_all, wt_all, l)
        hb = kcvc_hb.reshape(bsz, n_half, CMP_STRIDE * 2 * NSA_KV_COLS)
        kc, vc_t = _compress(hb, pe_hb, w1_big, b1p, w2k, b2k, w2vt, b2v, l)
        ya_t = _nsa(nq_t, nsa_srows, ks.reshape(n // CH, CH, NSA_KV_COLS), kw.reshape(n // CH, CH, NSA_KV_COLS),
                    vs_t, vw_t, kaug, kc, kcaug, vc_t, ng_t, asel_t, bsz, nq)
        lam_init = 0.8 - 0.6 * float(np.exp(-0.3 * l))
        scal = jnp.asarray([lam_init, 1.0 - lam_init], F32)
        yb_t = _diff(scal, dq_t, diff_srows, dk.reshape(n // CH, CH, DIFF_QK_COLS), dv_t, kaug[1],
                     diff_lambda[l], diff_subln_g[l][:, None], bsz, nq)
        h = _merge(ya_t, yb_t, gates, h, wa_all, wb_all, wo_all, l)
        h = _moe(h, norm2_g[:, None, :], wr_t, br, eexp, wg_all, wu_all, wd_all,
                 final_norm_g[None, :], l, final=(l == depth - 1))
    return h.reshape(bsz, seq, d)
```

```python
import functools

import numpy as np
import jax
import jax.numpy as jnp
from jax import lax
from jax.experimental import pallas as pl
from jax.experimental.pallas import tpu as pltpu

F32 = jnp.float32
BF16 = jnp.bfloat16

D_MODEL = 1024
HEAD_DIM = 64
NSA_HEADS = 8
NSA_KV_GROUPS = 2
NSA_GROUP_SIZE = NSA_HEADS // NSA_KV_GROUPS
CMP_BLOCK = 32
CMP_STRIDE = 16
CMP_HIDDEN = 128
SEL_BLOCK = 64
SEL_TOPK = 8
WINDOW = 512
FORCED_SCORE = 1e9
DIFF_HEADS = 4
MOE_GROUPS = 4
EXPERTS_PER_GROUP = 4
N_EXPERTS = MOE_GROUPS * EXPERTS_PER_GROUP
EXPERT_FF = D_MODEL // 8
RMS_EPS = 1e-6
SUBLN_EPS = 1e-5
NEG_INF = -1e30
N_ALIBI_HEADS = NSA_HEADS + DIFF_HEADS

NSA_Q_COLS = NSA_HEADS * HEAD_DIM
NSA_KV_COLS = NSA_KV_GROUPS * HEAD_DIM
NSA_GATE_COLS = 3 * NSA_HEADS
DIFF_QK_COLS = DIFF_HEADS * 2 * HEAD_DIM
DIFF_V_COLS = DIFF_HEADS * 2 * HEAD_DIM
GATE_ROWS_PER_GROUP = 16

CH = 256
DIFF_TILE_CHUNKS = 2
ROW_TILE = 512
VMEM_LIMIT = 56 * 1024 * 1024

LOG2E = float(np.log2(np.e))
Q_SCALE = HEAD_DIM ** -0.5 * LOG2E

QK_LANES = 2 * HEAD_DIM
AUG_LANES = 128
SLOPE_PIECES = 3
PAD_ROW = 2 * SLOPE_PIECES
ALIBI_ROWS = 16
MASK_ROW0 = ALIBI_ROWS
MASK_BIG = 1e30
SUM_ROWS = 16
BLOCK_COLS = 256

_NT = (((1,), (1,)), ((), ()))
_TN = (((0,), (0,)), ((), ()))


def _dot(a, b):
    return jnp.dot(a, b, preferred_element_type=F32)


def _const_spec(shape):
    nd = len(shape)
    return pl.BlockSpec(shape, lambda *_: (0,) * nd, pipeline_mode=pl.Buffered(1))


def _layer_spec(stacked, layer):
    nd = stacked.ndim - 1
    return pl.BlockSpec((None,) + tuple(stacked.shape[1:]), lambda *_: (layer,) + (0,) * nd,
                        pipeline_mode=pl.Buffered(1))


def _params(sem):
    return pltpu.CompilerParams(dimension_semantics=sem, vmem_limit_bytes=VMEM_LIMIT)


def _alibi_slopes():
    return 2.0 ** (-8.0 * np.arange(1, N_ALIBI_HEADS + 1) / N_ALIBI_HEADS)


_NAT_WIDTHS = (NSA_KV_COLS, NSA_KV_COLS, DIFF_QK_COLS, 2 * NSA_KV_COLS, 2 * D_MODEL)
_TR_ROWS = (NSA_Q_COLS, DIFF_QK_COLS, NSA_KV_COLS, NSA_KV_COLS, DIFF_V_COLS, 2 * GATE_ROWS_PER_GROUP)
_TR_SCALE = (Q_SCALE, Q_SCALE, 1.0, 1.0, 1.0, 1.0)
_KCVC_INDEX = 3


def _inproj_kernel(x_ref, g_ref, wn_ref, wt_ref, *refs):
    n_out = len(_NAT_WIDTHS) + len(_TR_ROWS)
    nat_refs = refs[:len(_NAT_WIDTHS)]
    tr_refs = refs[len(_NAT_WIDTHS):n_out]
    rows_k, rows_v = refs[n_out:]
    x = x_ref[...]
    xn = (x * lax.rsqrt(jnp.mean(x * x, axis=-1, keepdims=True) + RMS_EPS) * g_ref[...]).astype(BF16)
    off = 0
    for idx, (ref, width) in enumerate(zip(nat_refs, _NAT_WIDTHS)):
        if idx == _KCVC_INDEX:
            res = _dot(xn, wn_ref[:, off:off + width])
            rows_k[...] = res[:, :NSA_KV_COLS]
            rows_v[...] = res[:, NSA_KV_COLS:]
            n_rows = res.shape[0] // CMP_STRIDE
            for tok in range(CMP_STRIDE):
                lo = tok * width
                ref[:, lo:lo + NSA_KV_COLS] = rows_k[pl.ds(tok, n_rows, stride=CMP_STRIDE), :]
                ref[:, lo + NSA_KV_COLS:lo + width] = rows_v[pl.ds(tok, n_rows, stride=CMP_STRIDE), :]
        else:
            for c in range(0, width, 512):
                cw = min(512, width - c)
                ref[:, c:c + cw] = _dot(xn, wn_ref[:, off + c:off + c + cw]).astype(ref.dtype)
        off += width
    n_sub = x.shape[0] // CH
    off = 0
    for ref, rows, scale in zip(tr_refs, _TR_ROWS, _TR_SCALE):
        for c in range(0, rows, 256):
            rw = min(256, rows - c)
            res = lax.dot_general(wt_ref[off + c:off + c + rw, :], xn, _NT, preferred_element_type=F32)
            if scale != 1.0:
                res = res * scale
            for j in range(n_sub):
                ref[j, c:c + rw, :] = res[:, j * CH:(j + 1) * CH].astype(ref.dtype)
        off += rows


def _inproj(h2d, g, wn, wt, layer):
    n = h2d.shape[0]
    tm = ROW_TILE
    nat_dtypes = (BF16, BF16, BF16, F32, BF16)
    tr_dtypes = (BF16, BF16, BF16, BF16, BF16, F32)
    out_shape = [jax.ShapeDtypeStruct((n, w), dt) for w, dt in zip(_NAT_WIDTHS, nat_dtypes)]
    out_shape += [jax.ShapeDtypeStruct((n // CH, r, CH), dt) for r, dt in zip(_TR_ROWS, tr_dtypes)]
    out_specs = [pl.BlockSpec((tm, w), lambda i: (i, 0)) for w in _NAT_WIDTHS]
    out_specs += [pl.BlockSpec((tm // CH, r, CH), lambda i: (i, 0, 0)) for r in _TR_ROWS]
    hb_width = CMP_STRIDE * _NAT_WIDTHS[_KCVC_INDEX]
    out_shape[_KCVC_INDEX] = jax.ShapeDtypeStruct((n // CMP_STRIDE, hb_width), F32)
    out_specs[_KCVC_INDEX] = pl.BlockSpec((tm // CMP_STRIDE, hb_width), lambda i: (i, 0))
    return pl.pallas_call(
        _inproj_kernel,
        grid=(n // tm,),
        in_specs=[pl.BlockSpec((tm, D_MODEL), lambda i: (i, 0)),
                  _layer_spec(g, layer), _layer_spec(wn, layer), _layer_spec(wt, layer)],
        out_specs=out_specs,
        out_shape=out_shape,
        scratch_shapes=[pltpu.VMEM((tm, NSA_KV_COLS), F32), pltpu.VMEM((tm, NSA_KV_COLS), F32)],
        compiler_params=_params(("parallel",)),
        name="inproj",
    )(h2d, g, wn, wt)


def _compress_kernel(hb_ref, pe_ref, w1_ref, b1_ref, w2k_ref, b2k_ref, w2v_ref, b2v_ref, kc_ref, vct_ref):
    hb = hb_ref[0]
    rows = hb.shape[0]
    top = (hb + pe_ref[0]).astype(BF16)
    bot = (hb + pe_ref[1]).astype(BF16)
    p = _dot(top, w1_ref[0])
    q = _dot(bot, w1_ref[1])
    q_next = pltpu.roll(q, rows - 1, 0)
    hid = jax.nn.gelu(p + q_next + b1_ref[...])
    width = hid.shape[1] // 2
    kc_ref[0] = (_dot(hid[:, :width].astype(BF16), w2k_ref[...]) + b2k_ref[...]).astype(kc_ref.dtype)
    vct = lax.dot_general(w2v_ref[...], hid[:, width:].astype(BF16), _NT, preferred_element_type=F32)
    vct_ref[0] = (vct + b2v_ref[...]).astype(vct_ref.dtype)


def _compress(hb, pe_hb, w1_big, b1p, w2k, b2k, w2vt, b2v, layer):
    bsz, rows, width = hb.shape
    gk = NSA_KV_COLS
    return pl.pallas_call(
        _compress_kernel,
        grid=(bsz,),
        in_specs=[pl.BlockSpec((1, rows, width), lambda b: (b, 0, 0)),
                  *[_layer_spec(a, layer) for a in (pe_hb, w1_big, b1p, w2k, b2k, w2vt, b2v)]],
        out_specs=[pl.BlockSpec((1, rows, gk), lambda b: (b, 0, 0)),
                   pl.BlockSpec((1, gk, rows), lambda b: (b, 0, 0))],
        out_shape=[jax.ShapeDtypeStruct((bsz, rows, gk), BF16),
                   jax.ShapeDtypeStruct((bsz, gk, rows), BF16)],
        compiler_params=_params(("parallel",)),
        name="compress",
    )(hb, pe_hb, w1_big, b1p, w2k, b2k, w2vt, b2v)


def _bf16_pieces(x):
    out = []
    rest = np.asarray(x, np.float32)
    for _ in range(SLOPE_PIECES):
        piece = rest.astype(BF16).astype(np.float32)
        out.append(piece)
        rest = rest - piece
    return out


def _slope_rows(slopes, heads_per_block, cols_per_head):
    sl2 = (np.asarray(slopes, np.float32).astype(np.float64) * LOG2E).astype(np.float32)
    pieces = np.stack(_bf16_pieces(sl2) * 2, axis=0)
    rows = np.zeros((ALIBI_ROWS, sl2.shape[0]), np.float32)
    rows[:pieces.shape[0]] = pieces
    rows[PAD_ROW] = -MASK_BIG
    rows = np.repeat(rows, cols_per_head, axis=1)
    rows = rows.reshape(ALIBI_ROWS, -1, heads_per_block * cols_per_head).transpose(1, 0, 2)
    return jnp.asarray(rows, BF16)


def _key_aug_tables(seq, nb):
    pos = np.arange(seq)
    aug = np.zeros((2, seq + CH, AUG_LANES), np.float32)
    aug[:, :seq, 0:SLOPE_PIECES] = (pos % CH)[None, :, None]
    aug[:, :seq, SLOPE_PIECES:2 * SLOPE_PIECES] = (pos // CH * CH)[None, :, None]
    aug[0, pos, MASK_ROW0 + pos // SEL_BLOCK] = 1.0
    aug[:, seq:, PAD_ROW] = 1.0
    return jnp.asarray(aug.reshape(2, seq // CH + 1, CH, AUG_LANES), BF16)


def _cmp_aug_table(n_rows):
    aug = np.zeros((n_rows, AUG_LANES), np.float32)
    aug[:, 0:SLOPE_PIECES] = (np.arange(n_rows) * CMP_STRIDE)[:, None]
    aug[:, SLOPE_PIECES:2 * SLOPE_PIECES] = CMP_BLOCK - 1
    return jnp.asarray(aug, BF16)


def _tile_lanes(x, reps):
    return jnp.concatenate([x] * reps, axis=1)


def _query_minus_key(reps):
    shape = (CH, reps * CH)
    q_off = jnp.bitwise_and(lax.broadcasted_iota(jnp.int32, shape, 1), CH - 1)
    return q_off - lax.broadcasted_iota(jnp.int32, shape, 0)


def _flash_init(m_ref, acc_ref):
    m_ref[...] = jnp.full(m_ref.shape, NEG_INF, F32)
    acc_ref[...] = jnp.zeros(acc_ref.shape, F32)


def _normalized(acc_ref, g):
    dv = acc_ref.shape[1] - SUM_ROWS
    return acc_ref[g, 0:dv, :] / acc_ref[g, dv:dv + 1, :]


def _chunk_scores(k_blk, aug_blk, qa_ref):
    return _dot(jnp.concatenate([k_blk, aug_blk], axis=1), qa_ref[...])


def _stage_and_consume(prod, cons, qa_ref, block_cols=BLOCK_COLS):
    if prod is not None:
        k_blk, aug_blk, (ps_ref, pmx_ref), pmask = prod
        k_full = jnp.concatenate([k_blk, aug_blk], axis=1)
    if cons is not None:
        (cs_ref, cmx_ref), v_t, (m_ref, acc_ref), cmask = cons
        _, rows, gcols = acc_ref.shape
        dv = rows - SUM_ROWS
        ones = jnp.ones((SUM_ROWS, v_t.shape[1]), BF16)
    for c0 in range(0, qa_ref.shape[1], block_cols):
        csl = slice(c0, c0 + block_cols)
        if prod is not None:
            s = _dot(k_full, qa_ref[:, csl])
            if pmask is not None:
                s = jnp.where(pmask(c0), s, NEG_INF)
            ps_ref[:, csl] = s
            pmx_ref[:, csl] = jnp.max(s, axis=0, keepdims=True)
        if cons is not None:
            s = cs_ref[:, csl]
            if cmask is None:
                mx = cmx_ref[:, csl]
            else:
                s = jnp.where(cmask(c0), s, NEG_INF)
                mx = jnp.max(s, axis=0, keepdims=True)
            m_prev = m_ref[:, csl]
            m_new = jnp.maximum(m_prev, mx)
            alpha = jnp.exp2(m_prev - m_new)
            p = jnp.exp2(s - m_new).astype(BF16)
            g = c0 // gcols
            gsl = slice(c0 - g * gcols, c0 - g * gcols + block_cols)
            v_ones = jnp.concatenate([v_t[g * dv:(g + 1) * dv], ones], axis=0)
            acc_ref[g, :, gsl] = alpha * acc_ref[g, :, gsl] + _dot(v_ones, p)
            m_ref[:, csl] = m_new


def _consume(buf, v_t, state, qa_ref, mask=None, block_cols=BLOCK_COLS):
    _stage_and_consume(None, (buf, v_t, state, mask), qa_ref, block_cols)


def _pad_or(aug_ref, c, is_pad):
    return aug_ref[jnp.where(is_pad, aug_ref.shape[0] - 1, c)]


def _identity(pos):
    return pos


def _causal_first(n, k_ref, aug_ref, buf0, chunk_at=_identity):
    c = chunk_at(0)
    return (k_ref[c], _pad_or(aug_ref, c, jnp.bitwise_and(n, 1) == 1), buf0, None)


def _causal_pairs(n, k_ref, v_ref, aug_ref, qa_ref, bufs, state, block_cols=BLOCK_COLS, chunk_at=_identity):
    buf0, buf1 = bufs
    pad = jnp.bitwise_and(n, 1)

    def pair(k, carry):
        pos = 2 * k - pad
        c0 = chunk_at(jnp.maximum(pos, 0))
        c1 = chunk_at(pos + 1)
        c2 = chunk_at(pos + 2)
        _stage_and_consume((k_ref[c1], aug_ref[c1], buf1, None), (buf0, v_ref[c0], state, None),
                           qa_ref, block_cols)
        _stage_and_consume((k_ref[c2], aug_ref[c2], buf0, None), (buf1, v_ref[c1], state, None),
                           qa_ref, block_cols)
        return carry
    lax.fori_loop(0, jnp.right_shift(n + pad, 1), pair, 0)


def _unselected_mask_rows(imp, k_sel):
    n_blk, width = imp.shape
    rows_per = 8
    j_loc = lax.broadcasted_iota(jnp.int32, (rows_per, width), 0)
    mask_blocks = []
    for r0 in range(0, n_blk, rows_per):
        blk = imp[r0:r0 + rows_per, :]
        cnt = jnp.zeros((rows_per, width), jnp.int32)
        for jp in range(n_blk):
            row = imp[jp:jp + 1, :]
            gt = jnp.where(row > blk, 1, 0)
            ge = jnp.where(row >= blk, 1, 0)
            if jp >= r0 + rows_per - 1:
                cnt = cnt + gt
            elif jp < r0:
                cnt = cnt + ge
            else:
                cnt = cnt + jnp.where(j_loc + r0 > jp, ge, gt)
        mask_blocks.append(jnp.where(cnt < k_sel, 0.0, -MASK_BIG))
    return jnp.concatenate(mask_blocks, axis=0)


def _nsa_kernel(q_ref, srow_ref, ks_ref, kw_ref, vs_ref, vw_ref, kaug_ref, kc_ref, kcaug_ref, vc_ref,
                ng_ref, asel_ref, o_ref, qa_ref, ocmp_ref, s0, x0, s1, x1, w0, y0, w1, y1, w2, y2,
                m_s, acc_s, m_w, acc_w, chunk_list):
    i = pl.program_id(1)
    t0 = i * CH
    nh = NSA_GROUP_SIZE
    ng = NSA_KV_GROUPS
    gcols = nh * CH
    cols = ng * gcols
    n_cmp = kc_ref.shape[1]
    n_blk = asel_ref.shape[0]
    k_sel = min(SEL_TOPK, n_blk)
    bufs_s = ((s0, x0), (s1, x1))
    bufs_w = ((w0, y0), (w1, y1), (w2, y2))
    state_s = (m_s, acc_s)
    state_w = (m_w, acc_w)
    kaug_s = kaug_ref.at[0]
    kaug_w = kaug_ref.at[1]

    t_pos = t0 + lax.broadcasted_iota(jnp.int32, (1, CH), 1)
    d0_i = _query_minus_key(BLOCK_COLS // CH)
    causal = lambda c0: d0_i >= 0
    window_edge = lambda c0: d0_i < 0

    zeros_q = jnp.zeros((HEAD_DIM, CH), BF16)
    for g in range(ng):
        for hh in range(nh):
            h = g * nh + hh
            qh = q_ref[0, h * HEAD_DIM:(h + 1) * HEAD_DIM, :]
            for gg in range(ng):
                qa_ref[gg * HEAD_DIM:(gg + 1) * HEAD_DIM, h * CH:(h + 1) * CH] = qh if gg == g else zeros_q
    qa_ref[QK_LANES:QK_LANES + ALIBI_ROWS, :] = srow_ref[...]
    qa_ref[QK_LANES + MASK_ROW0:, :] = jnp.zeros((AUG_LANES - MASK_ROW0, cols), BF16)

    cmp_scores = _chunk_scores(kc_ref[0], kcaug_ref[...], qa_ref)

    n_back = WINDOW // CH
    stage_w = []
    chunks_w = []
    for back in range(n_back, 0, -1):
        c = jnp.maximum(i - back, 0)
        stage_w.append((kw_ref[c], _pad_or(kaug_w, c, i < back), bufs_w[n_back - back],
                        window_edge if back == n_back else None))
        chunks_w.append(c)
    stage_w.append((kw_ref[i], kaug_w[i], bufs_w[n_back], causal))
    chunks_w.append(i)
    _stage_and_consume(stage_w[0], None, qa_ref)

    n_idx = lax.broadcasted_iota(jnp.int32, (n_cmp, cols), 0)
    t_pos_all = t0 + jnp.bitwise_and(lax.broadcasted_iota(jnp.int32, (1, cols), 1), CH - 1)
    valid_c = n_idx * CMP_STRIDE + (CMP_BLOCK - 1) <= t_pos_all
    lg = jnp.where(valid_c, cmp_scores, NEG_INF)
    m = jnp.max(lg, axis=0, keepdims=True)
    p = jnp.where(valid_c, jnp.exp2(lg - m), 0.0)
    l = jnp.sum(p, axis=0, keepdims=True)
    pc = p * jnp.where(l > 0.0, 1.0 / l, 0.0)
    pc_b = pc.astype(BF16)
    vc = vc_ref[0]

    j_idx = lax.broadcasted_iota(jnp.int32, (n_blk, CH), 0)
    cur = jnp.right_shift(t_pos, SEL_BLOCK.bit_length() - 1)
    forced = (j_idx == 0) | (j_idx == cur) | (j_idx == cur - 1)
    in_past = j_idx * SEL_BLOCK <= t_pos
    for g in range(ng):
        gsl = slice(g * gcols, (g + 1) * gcols)
        ocmp_ref[g] = _dot(vc[g * HEAD_DIM:(g + 1) * HEAD_DIM], pc_b[:, gsl])
        psum = pc[:, g * gcols:g * gcols + CH]
        for hh in range(1, nh):
            psum = psum + pc[:, g * gcols + hh * CH:g * gcols + (hh + 1) * CH]
        imp = jnp.dot(asel_ref[...], psum, precision=lax.Precision.HIGHEST, preferred_element_type=F32)
        imp = jnp.where(in_past, jnp.where(forced, FORCED_SCORE, imp), -1.0)
        mask_rows = _unselected_mask_rows(imp, k_sel)
        kept = mask_rows if g == 0 else jnp.maximum(kept, mask_rows)
        qa_ref[QK_LANES + MASK_ROW0:QK_LANES + MASK_ROW0 + n_blk, gsl] = _tile_lanes(mask_rows.astype(BF16), nh)

    blocks_per_chunk = CH // SEL_BLOCK
    n_sel = jnp.int32(0)
    for c in range(n_blk // blocks_per_chunk):
        hit = jnp.max(kept[c * blocks_per_chunk:(c + 1) * blocks_per_chunk, :]) > -1.0
        chunk_list[n_sel] = c
        n_sel = n_sel + jnp.logical_and(hit, c < i).astype(jnp.int32)
    chunk_list[n_sel] = i

    def chunk_at(pos):
        return chunk_list[pos]

    _flash_init(*state_w)
    for j in range(1, n_back + 1):
        _stage_and_consume(stage_w[j], (bufs_w[j - 1], vw_ref[chunks_w[j - 1]], state_w, None), qa_ref)
    _stage_and_consume(_causal_first(n_sel, ks_ref, kaug_s, bufs_s[0], chunk_at),
                       (bufs_w[n_back], vw_ref[i], state_w, None), qa_ref)
    _flash_init(*state_s)
    _causal_pairs(n_sel, ks_ref, vs_ref, kaug_s, qa_ref, bufs_s, state_s, chunk_at=chunk_at)
    _consume(bufs_s[0], vs_ref[i], state_s, qa_ref, mask=causal)

    for g in range(ng):
        def gate(branch):
            r0 = g * GATE_ROWS_PER_GROUP + branch * nh
            return jax.nn.sigmoid(jnp.concatenate([ng_ref[0, r0 + hh:r0 + hh + 1, :] for hh in range(nh)], axis=1))
        out = gate(0) * ocmp_ref[g] + gate(1) * _normalized(acc_s, g) + gate(2) * _normalized(acc_w, g)
        for hh in range(nh):
            h = g * nh + hh
            o_ref[0, h * HEAD_DIM:(h + 1) * HEAD_DIM, :] = out[:, hh * CH:(hh + 1) * CH].astype(o_ref.dtype)


def _nsa(nq_t, srows, ks3, kw3, vs_t, vw_t, kaug, kc, kcaug, vc_t, ng_t, asel_t, bsz, nq):
    n_cmp = kc.shape[1]
    ng = NSA_KV_GROUPS
    gcols = NSA_GROUP_SIZE * CH
    cols = ng * gcols
    acc_shape = (ng, HEAD_DIM + SUM_ROWS, gcols)
    stage = [pltpu.VMEM((CH, cols), F32), pltpu.VMEM((1, cols), F32)]
    return pl.pallas_call(
        _nsa_kernel,
        grid=(bsz, nq),
        in_specs=[
            pl.BlockSpec((1, NSA_Q_COLS, CH), lambda b, i: (b * nq + i, 0, 0)),
            pl.BlockSpec(srows.shape, lambda b, i: (0, 0)),
            pl.BlockSpec((nq, CH, NSA_KV_COLS), lambda b, i: (b, 0, 0)),
            pl.BlockSpec((nq, CH, NSA_KV_COLS), lambda b, i: (b, 0, 0)),
            pl.BlockSpec((nq, NSA_KV_COLS, CH), lambda b, i: (b, 0, 0)),
            pl.BlockSpec((nq, NSA_KV_COLS, CH), lambda b, i: (b, 0, 0)),
            pl.BlockSpec(kaug.shape, lambda b, i: (0, 0, 0, 0)),
            pl.BlockSpec((1, n_cmp, NSA_KV_COLS), lambda b, i: (b, 0, 0)),
            pl.BlockSpec(kcaug.shape, lambda b, i: (0, 0)),
            pl.BlockSpec((1, NSA_KV_COLS, n_cmp), lambda b, i: (b, 0, 0)),
            pl.BlockSpec((1, ng * GATE_ROWS_PER_GROUP, CH), lambda b, i: (b * nq + i, 0, 0)),
            pl.BlockSpec(asel_t.shape, lambda b, i: (0, 0)),
        ],
        out_specs=pl.BlockSpec((1, NSA_Q_COLS, CH), lambda b, i: (b * nq + i, 0, 0)),
        out_shape=jax.ShapeDtypeStruct((bsz * nq, NSA_Q_COLS, CH), BF16),
        scratch_shapes=(
            [pltpu.VMEM((QK_LANES + AUG_LANES, cols), BF16), pltpu.VMEM((ng, HEAD_DIM, gcols), F32)]
            + stage * 5
            + [pltpu.VMEM((1, cols), F32), pltpu.VMEM(acc_shape, F32),
               pltpu.VMEM((1, cols), F32), pltpu.VMEM(acc_shape, F32),
               pltpu.SMEM((nq + 1,), jnp.int32)]),
        compiler_params=_params(("parallel", "arbitrary")),
        name="nsa_attention",
    )(nq_t, srows, ks3, kw3, vs_t, vw_t, kaug, kc, kcaug, vc_t, ng_t, asel_t)


def _diff_kernel(scal_ref, q_ref, srow_ref, k_ref, v_ref, kaug_ref, lam_ref, gain_ref, o_ref,
                 qa_ref, s0, x0, s1, x1, m_r, acc_r):
    nt = DIFF_TILE_CHUNKS
    tq = nt * CH
    c0 = pl.program_id(2) * nt
    lam_init = scal_ref[0]
    out_scale = scal_ref[1]
    sub2 = lax.broadcasted_iota(jnp.int32, (QK_LANES, CH), 0)
    zero = jnp.zeros((), BF16)
    for j in range(nt):
        q = q_ref[j]
        qa_ref[0:QK_LANES, j * CH:(j + 1) * CH] = jnp.where(sub2 < HEAD_DIM, q, zero)
        qa_ref[0:QK_LANES, tq + j * CH:tq + (j + 1) * CH] = jnp.where(sub2 >= HEAD_DIM, q, zero)
    qa_ref[QK_LANES:QK_LANES + ALIBI_ROWS, :] = srow_ref[0]
    qa_ref[QK_LANES + MASK_ROW0:, :] = jnp.zeros((AUG_LANES - MASK_ROW0, 2 * tq), BF16)
    bc = 2 * tq
    lane = lax.broadcasted_iota(jnp.int32, (CH, bc), 1)
    sub = lax.broadcasted_iota(jnp.int32, (CH, bc), 0)

    def on_or_after(first_key):
        return lambda col0: jnp.bitwise_and(lane + col0, tq - 1) - sub >= first_key

    state = (m_r, acc_r)
    bufs = ((s0, x0), (s1, x1))
    _stage_and_consume(_causal_first(c0, k_ref, kaug_ref, bufs[0]), None, qa_ref, bc)
    _flash_init(*state)
    _causal_pairs(c0, k_ref, v_ref, kaug_ref, qa_ref, bufs, state, bc)
    _stage_and_consume((k_ref[c0 + 1], kaug_ref[c0 + 1], bufs[1], on_or_after(CH)),
                       (bufs[0], v_ref[c0], state, on_or_after(0)), qa_ref, bc)
    _consume(bufs[1], v_ref[c0 + 1], state, qa_ref, block_cols=bc)

    lp = lam_ref[...]
    lam = (jnp.exp(jnp.sum(lp[0:1] * lp[1:2], axis=1, keepdims=True))
           - jnp.exp(jnp.sum(lp[2:3] * lp[3:4], axis=1, keepdims=True)) + lam_init)
    att = _normalized(acc_r, 0)
    o = att[:, 0:tq] - lam * att[:, tq:2 * tq]
    o = o * lax.rsqrt(jnp.mean(o * o, axis=0, keepdims=True) + SUBLN_EPS) * gain_ref[...]
    o = (o * out_scale).astype(o_ref.dtype)
    for j in range(nt):
        o_ref[j] = o[:, j * CH:(j + 1) * CH]


def _diff(scal, dq_t, srows, dk3, dv_t, kaug_plain, lam_p, gain, bsz, nq):
    hd2 = 2 * HEAD_DIM
    nt = DIFF_TILE_CHUNKS
    n_tiles = nq // nt
    cols = 2 * nt * CH
    grid_spec = pltpu.PrefetchScalarGridSpec(
        num_scalar_prefetch=1,
        grid=(bsz, DIFF_HEADS, n_tiles),
        in_specs=[
            pl.BlockSpec((nt, hd2, CH), lambda b, h, i, s: (b * n_tiles + i, h, 0)),
            pl.BlockSpec((1, ALIBI_ROWS, cols), lambda b, h, i, s: (h, 0, 0)),
            pl.BlockSpec((nq, CH, hd2), lambda b, h, i, s: (b, 0, h)),
            pl.BlockSpec((nq, hd2, CH), lambda b, h, i, s: (b, h, 0)),
            pl.BlockSpec(kaug_plain.shape, lambda b, h, i, s: (0, 0, 0)),
            pl.BlockSpec(lam_p.shape, lambda b, h, i, s: (0, 0)),
            pl.BlockSpec(gain.shape, lambda b, h, i, s: (0, 0)),
        ],
        out_specs=pl.BlockSpec((nt, hd2, CH), lambda b, h, i, s: (b * n_tiles + i, h, 0)),
        scratch_shapes=[
            pltpu.VMEM((QK_LANES + AUG_LANES, cols), BF16),
            pltpu.VMEM((CH, cols), F32), pltpu.VMEM((1, cols), F32),
            pltpu.VMEM((CH, cols), F32), pltpu.VMEM((1, cols), F32),
            pltpu.VMEM((1, cols), F32), pltpu.VMEM((1, hd2 + SUM_ROWS, cols), F32),
        ],
    )
    return pl.pallas_call(
        _diff_kernel,
        grid_spec=grid_spec,
        out_shape=jax.ShapeDtypeStruct((bsz * nq, DIFF_V_COLS, CH), BF16),
        compiler_params=_params(("parallel", "parallel", "arbitrary")),
        name="diff_attention",
    )(scal, dq_t, srows, dk3, dv_t, kaug_plain, lam_p, gain)


def _merge_kernel(ya_ref, yb_ref, gates_ref, h_ref, wa_ref, wb_ref, wo_ref, o_ref):
    for j in range(ya_ref.shape[0]):
        rows = slice(j * CH, (j + 1) * CH)
        a = lax.dot_general(ya_ref[j], wa_ref[...], _TN, preferred_element_type=F32)
        b = lax.dot_general(yb_ref[j], wb_ref[...], _TN, preferred_element_type=F32)
        ga = jax.nn.sigmoid(gates_ref[rows, :D_MODEL].astype(F32))
        gb = jax.nn.sigmoid(gates_ref[rows, D_MODEL:].astype(F32))
        merged = (ga * a + gb * b).astype(BF16)
        o_ref[rows, :] = h_ref[rows, :] + _dot(merged, wo_ref[...])


def _merge(ya_t, yb_t, gates, h2d, wa, wb, wo, layer):
    n = h2d.shape[0]
    tm = ROW_TILE
    return pl.pallas_call(
        _merge_kernel,
        grid=(n // tm,),
        in_specs=[pl.BlockSpec((tm // CH, NSA_Q_COLS, CH), lambda i: (i, 0, 0)),
                  pl.BlockSpec((tm // CH, DIFF_V_COLS, CH), lambda i: (i, 0, 0)),
                  pl.BlockSpec((tm, 2 * D_MODEL), lambda i: (i, 0)),
                  pl.BlockSpec((tm, D_MODEL), lambda i: (i, 0)),
                  _layer_spec(wa, layer), _layer_spec(wb, layer), _layer_spec(wo, layer)],
        out_specs=pl.BlockSpec((tm, D_MODEL), lambda i: (i, 0)),
        out_shape=jax.ShapeDtypeStruct((n, D_MODEL), F32),
        compiler_params=_params(("parallel",)),
        name="merge_outproj",
    )(ya_t, yb_t, gates, h2d, wa, wb, wo)


def _first_argmax(x, rows, n):
    mx = jnp.max(x, axis=0, keepdims=True)
    idx = jnp.min(jnp.where(x == mx, rows, n), axis=0, keepdims=True)
    return mx, idx


def _moe_kernel(h_ref, g_ref, wr_ref, br_ref, eexp_ref, wg_ref, wu_ref, wd_ref, fg_ref, o_ref, *, final):
    hres = h_ref[...]
    xf = hres * lax.rsqrt(jnp.mean(hres * hres, axis=-1, keepdims=True) + RMS_EPS) * g_ref[...]
    xb = xf.astype(BF16)
    tm = hres.shape[0]

    x_lo = (xf - xb.astype(F32)).astype(BF16)
    n_r = wr_ref.shape[0] // 2
    part = lax.dot_general(wr_ref[...], xb, _NT, preferred_element_type=F32)
    logits = (part[0:n_r] + part[n_r:2 * n_r]
              + lax.dot_general(wr_ref[0:n_r, :], x_lo, _NT, preferred_element_type=F32)
              + br_ref[...])
    gl = logits[0:MOE_GROUPS]
    rows_g = lax.broadcasted_iota(jnp.int32, (MOE_GROUPS, tm), 0)
    gmax, gidx = _first_argmax(gl, rows_g, MOE_GROUPS)
    g_w = 1.0 / jnp.sum(jnp.exp(gl - gmax), axis=0, keepdims=True)
    esel = jnp.zeros((EXPERTS_PER_GROUP, tm), F32)
    for gg in range(MOE_GROUPS):
        lo = MOE_GROUPS + gg * EXPERTS_PER_GROUP
        esel = jnp.where(gidx == gg, logits[lo:lo + EXPERTS_PER_GROUP], esel)
    rows_e = lax.broadcasted_iota(jnp.int32, (EXPERTS_PER_GROUP, tm), 0)
    v1, i1 = _first_argmax(esel, rows_e, EXPERTS_PER_GROUP)
    rest = jnp.where(rows_e == i1, -jnp.inf, esel)
    v2, i2 = _first_argmax(rest, rows_e, EXPERTS_PER_GROUP)
    e21 = jnp.exp(v2 - v1)
    w1 = g_w / (1.0 + e21)
    w2 = g_w * e21 / (1.0 + e21)
    rows_c = lax.broadcasted_iota(jnp.int32, (N_EXPERTS, tm), 0)
    grp_c = jnp.right_shift(rows_c, EXPERTS_PER_GROUP.bit_length() - 1)
    exp_c = jnp.bitwise_and(rows_c, EXPERTS_PER_GROUP - 1)
    comb = jnp.where(grp_c == gidx,
                     jnp.where(exp_c == i1, w1, 0.0) + jnp.where(exp_c == i2, w2, 0.0), 0.0)
    comb_hi = comb.astype(BF16)
    comb_lo = (comb - comb_hi.astype(F32)).astype(BF16)

    comb_nat = (lax.dot_general(comb_hi, eexp_ref[...], _TN, preferred_element_type=F32)
                + lax.dot_general(comb_lo, eexp_ref[...], _TN, preferred_element_type=F32))

    acc = hres
    n_ff = wg_ref.shape[1]
    step = 512
    for c in range(0, n_ff, step):
        cols = slice(c, c + step)
        hg = _dot(xb, wg_ref[:, cols])
        hu = _dot(xb, wu_ref[:, cols])
        act = jax.nn.silu(hg) * hu
        parts = []
        for e0 in range(0, step, EXPERT_FF):
            e = (c + e0) // EXPERT_FF
            parts.append((act[:, e0:e0 + EXPERT_FF] * comb_nat[:, e:e + 1]).astype(BF16))
        acc = acc + _dot(jnp.concatenate(parts, axis=1), wd_ref[cols, :])
    if final:
        acc = acc * lax.rsqrt(jnp.mean(acc * acc, axis=-1, keepdims=True) + RMS_EPS) * fg_ref[...]
    o_ref[...] = acc


def _moe(h2d, g, wr_t, br, eexp, wg, wu, wd, fg, layer, final):
    n = h2d.shape[0]
    tm = ROW_TILE
    return pl.pallas_call(
        functools.partial(_moe_kernel, final=final),
        grid=(n // tm,),
        in_specs=[pl.BlockSpec((tm, D_MODEL), lambda i: (i, 0)),
                  _layer_spec(g, layer), _layer_spec(wr_t, layer), _layer_spec(br, layer),
                  _const_spec(eexp.shape), _layer_spec(wg, layer), _layer_spec(wu, layer),
                  _layer_spec(wd, layer), _const_spec(fg.shape)],
        out_specs=pl.BlockSpec((tm, D_MODEL), lambda i: (i, 0)),
        out_shape=jax.ShapeDtypeStruct((n, D_MODEL), F32),
        compiler_params=_params(("parallel",)),
        name="moe_final" if final else "moe",
    )(h2d, g, wr_t, br, eexp, wg, wu, wd, fg)


def _split_points():
    sizes = ([NSA_Q_COLS] + [NSA_KV_COLS] * 6
             + [NSA_GATE_COLS, DIFF_QK_COLS, DIFF_QK_COLS, DIFF_V_COLS, D_MODEL, D_MODEL])
    return [int(v) for v in np.cumsum(sizes)[:-1]]


def _cmp_to_sel_t(n_rows, nc, nb):
    c0 = np.arange(nc)[:, None] * CMP_STRIDE
    s0 = np.arange(nb)[None, :] * SEL_BLOCK
    ov = np.maximum(0, np.minimum(c0 + CMP_BLOCK, s0 + SEL_BLOCK) - np.maximum(c0, s0)) / CMP_BLOCK
    out = np.zeros((nb, n_rows), np.float32)
    out[:, :nc] = ov.T
    return out


def kernel(x, norm1_g, w_in, cmp_pe, cmp_w1, cmp_b1, cmp_w2, cmp_b2, diff_lambda, diff_subln_g, w_branch_a, w_branch_b, w_out, norm2_g, router_grp_w, router_grp_b, router_exp_w, router_exp_b, exp_w_gate, exp_w_up, exp_w_down, final_norm_g):
    bsz, seq, d = x.shape
    depth = w_in.shape[0]
    n = bsz * seq
    nq = seq // CH
    n_half = seq // CMP_STRIDE
    nc = (seq - CMP_BLOCK) // CMP_STRIDE + 1
    nb = seq // SEL_BLOCK
    assert d == D_MODEL and seq % ROW_TILE == 0 and seq >= WINDOW and WINDOW % CH == 0
    assert nb % 16 == 0 and MASK_ROW0 + nb <= AUG_LANES and n_half <= 256
    g_kv = NSA_KV_GROUPS
    eye_g = jnp.eye(g_kv, dtype=F32)

    (nq_w, kc_w, vc_w, ks_w, vs_w, kw_w, vw_w, ng_w, dq_w, dk_w, dv_w, ga_w, gb_w) = jnp.split(
        w_in, _split_points(), axis=-1)
    ng_w = ng_w.reshape(depth, d, g_kv, NSA_GROUP_SIZE, 3).transpose(0, 1, 2, 4, 3)
    ng_w = ng_w.reshape(depth, d, g_kv, 3 * NSA_GROUP_SIZE)
    ng_w = jnp.pad(ng_w, ((0, 0), (0, 0), (0, 0), (0, GATE_ROWS_PER_GROUP - 3 * NSA_GROUP_SIZE)))
    ng_w = ng_w.reshape(depth, d, g_kv * GATE_ROWS_PER_GROUP)
    wn_all = jnp.concatenate([ks_w, kw_w, dk_w, kc_w, vc_w, ga_w, gb_w], axis=-1).astype(BF16)
    wt_all = jnp.concatenate([nq_w, dq_w, vs_w, vw_w, dv_w, ng_w], axis=-1)
    wt_all = jnp.swapaxes(wt_all, 1, 2).astype(BF16)

    w1r = cmp_w1.reshape(depth, 2, 2, CMP_STRIDE, HEAD_DIM, CMP_HIDDEN).astype(BF16)
    w1_big = jnp.zeros((depth, 2, CMP_STRIDE, 2 * NSA_KV_COLS, 2 * g_kv * CMP_HIDDEN), BF16)
    for c in range(2):
        for g in range(g_kv):
            r0 = c * NSA_KV_COLS + g * HEAD_DIM
            c0 = (c * g_kv + g) * CMP_HIDDEN
            w1_big = w1_big.at[:, :, :, r0:r0 + HEAD_DIM, c0:c0 + CMP_HIDDEN].set(w1r[:, c])
    w1_big = w1_big.reshape(depth, 2, CMP_STRIDE * 2 * NSA_KV_COLS, 2 * g_kv * CMP_HIDDEN)
    per = cmp_pe.reshape(depth, 2, 2, CMP_STRIDE, HEAD_DIM)
    pe_hb = jnp.einsum('Lchld,g->Lhlcgd', per, jnp.ones((g_kv,), F32))
    pe_hb = pe_hb.reshape(depth, 2, 1, CMP_STRIDE * 2 * NSA_KV_COLS)
    b1p = jnp.broadcast_to(cmp_b1[:, :, None, :], (depth, 2, g_kv, CMP_HIDDEN)).reshape(depth, 1, -1)
    w2k = jnp.einsum('Lfd,gG->LgfGd', cmp_w2[:, 0], eye_g).reshape(depth, g_kv * CMP_HIDDEN, NSA_KV_COLS)
    w2vt = jnp.einsum('Lfd,gG->LGdgf', cmp_w2[:, 1], eye_g).reshape(depth, NSA_KV_COLS, g_kv * CMP_HIDDEN)
    w2k = w2k.astype(BF16)
    w2vt = w2vt.astype(BF16)
    b2k = jnp.tile(cmp_b2[:, 0], (1, g_kv))[:, None, :]
    b2v = jnp.tile(cmp_b2[:, 1], (1, g_kv))[:, :, None]

    slopes = _alibi_slopes()
    nsa_srows = _slope_rows(slopes[:NSA_HEADS], NSA_HEADS, CH)[0]
    diff_srows = _slope_rows(np.repeat(slopes[NSA_HEADS:], 2), 2, DIFF_TILE_CHUNKS * CH)
    kaug = _key_aug_tables(seq, nb)
    kcaug = _cmp_aug_table(n_half)
    asel_t = jnp.asarray(_cmp_to_sel_t(n_half, nc, nb))

    wa_all = w_branch_a.astype(BF16)
    wb_all = w_branch_b.astype(BF16)
    wo_all = w_out.astype(BF16)

    wr = jnp.concatenate([router_grp_w, router_exp_w.reshape(depth, d, N_EXPERTS)], axis=-1)
    n_r = MOE_GROUPS + N_EXPERTS
    wr_t = jnp.pad(jnp.swapaxes(wr, 1, 2), ((0, 0), (0, 32 - n_r), (0, 0)))
    wr_hi = wr_t.astype(BF16)
    wr_t = jnp.concatenate([wr_hi, (wr_t - wr_hi.astype(F32)).astype(BF16)], axis=1)
    br = jnp.concatenate([router_grp_b, router_exp_b.reshape(depth, N_EXPERTS)], axis=-1)
    br = jnp.pad(br, ((0, 0), (0, 32 - n_r)))[:, :, None]
    eexp = jnp.asarray(np.eye(N_EXPERTS, 128, dtype=np.float32), BF16)
    wg_all = jnp.swapaxes(exp_w_gate, 1, 2).reshape(depth, d, N_EXPERTS * EXPERT_FF).astype(BF16)
    wu_all = jnp.swapaxes(exp_w_up, 1, 2).reshape(depth, d, N_EXPERTS * EXPERT_FF).astype(BF16)
    wd_all = exp_w_down.reshape(depth, N_EXPERTS * EXPERT_FF, d).astype(BF16)

    h = x.reshape(n, d)
    for l in range(depth):
        (ks, kw, dk, kcvc_hb, gates, nq_t, dq_t, vs_t, vw_t, dv_t, ng_t) = _inproj(
            h, norm1_g[:, None, :], wn_all, wt_all, l)
        hb = kcvc_hb.reshape(bsz, n_half, CMP_STRIDE * 2 * NSA_KV_COLS)
        kc, vc_t = _compress(hb, pe_hb, w1_big, b1p, w2k, b2k, w2vt, b2v, l)
        ya_t = _nsa(nq_t, nsa_srows, ks.reshape(n // CH, CH, NSA_KV_COLS), kw.reshape(n // CH, CH, NSA_KV_COLS),
                    vs_t, vw_t, kaug, kc, kcaug, vc_t, ng_t, asel_t, bsz, nq)
        lam_init = 0.8 - 0.6 * float(np.exp(-0.3 * l))
        scal = jnp.asarray([lam_init, 1.0 - lam_init], F32)
        yb_t = _diff(scal, dq_t, diff_srows, dk.reshape(n // CH, CH, DIFF_QK_COLS), dv_t, kaug[1],
                     diff_lambda[l], diff_subln_g[l][:, None], bsz, nq)
        h = _merge(ya_t, yb_t, gates, h, wa_all, wb_all, wo_all, l)
        h = _moe(h, norm2_g[:, None, :], wr_t, br, eexp, wg_all, wu_all, wd_all,
                 final_norm_g[None, :], l, final=(l == depth - 1))
    return h.reshape(bsz, seq, d)
```

```python
import functools

import numpy as np
import jax
import jax.numpy as jnp
from jax import lax
from jax.experimental import pallas as pl
from jax.experimental.pallas import tpu as pltpu

F32 = jnp.float32
BF16 = jnp.bfloat16

D_MODEL = 1024
HEAD_DIM = 64
NSA_HEADS = 8
NSA_KV_GROUPS = 2
NSA_GROUP_SIZE = NSA_HEADS // NSA_KV_GROUPS
CMP_BLOCK = 32
CMP_STRIDE = 16
CMP_HIDDEN = 128
SEL_BLOCK = 64
SEL_TOPK = 8
WINDOW = 512
FORCED_SCORE = 1e9
DIFF_HEADS = 4
MOE_GROUPS = 4
EXPERTS_PER_GROUP = 4
N_EXPERTS = MOE_GROUPS * EXPERTS_PER_GROUP
EXPERT_FF = D_MODEL // 8
RMS_EPS = 1e-6
SUBLN_EPS = 1e-5
NEG_INF = -1e30
N_ALIBI_HEADS = NSA_HEADS + DIFF_HEADS

NSA_Q_COLS = NSA_HEADS * HEAD_DIM
NSA_KV_COLS = NSA_KV_GROUPS * HEAD_DIM
NSA_GATE_COLS = 3 * NSA_HEADS
DIFF_QK_COLS = DIFF_HEADS * 2 * HEAD_DIM
DIFF_V_COLS = DIFF_HEADS * 2 * HEAD_DIM
GATE_ROWS_PER_GROUP = 16

CH = 256
DIFF_TILE_CHUNKS = 2
DIFF_HEADS_PER_STEP = 4
ROW_TILE = 512
VMEM_LIMIT = 56 * 1024 * 1024

LOG2E = float(np.log2(np.e))
Q_SCALE = HEAD_DIM ** -0.5 * LOG2E

QK_LANES = 2 * HEAD_DIM
AUG_LANES = 128
SLOPE_PIECES = 3
PAD_ROW = 2 * SLOPE_PIECES
ALIBI_ROWS = 16
MASK_ROW0 = ALIBI_ROWS
MASK_BIG = 1e30
SUM_ROWS = 16
BLOCK_COLS = 256

_NT = (((1,), (1,)), ((), ()))
_TN = (((0,), (0,)), ((), ()))


def _dot(a, b):
    return jnp.dot(a, b, preferred_element_type=F32)


def _const_spec(shape):
    nd = len(shape)
    return pl.BlockSpec(shape, lambda *_: (0,) * nd, pipeline_mode=pl.Buffered(1))


def _layer_spec(stacked, layer):
    nd = stacked.ndim - 1
    return pl.BlockSpec((None,) + tuple(stacked.shape[1:]), lambda *_: (layer,) + (0,) * nd,
                        pipeline_mode=pl.Buffered(1))


def _params(sem):
    return pltpu.CompilerParams(dimension_semantics=sem, vmem_limit_bytes=VMEM_LIMIT)


def _alibi_slopes():
    return 2.0 ** (-8.0 * np.arange(1, N_ALIBI_HEADS + 1) / N_ALIBI_HEADS)


_NAT_WIDTHS = (NSA_KV_COLS, NSA_KV_COLS, DIFF_QK_COLS, 2 * NSA_KV_COLS, 2 * D_MODEL)
_TR_ROWS = (NSA_Q_COLS, DIFF_QK_COLS, NSA_KV_COLS, NSA_KV_COLS, DIFF_V_COLS, 2 * GATE_ROWS_PER_GROUP)
_TR_SCALE = (Q_SCALE, Q_SCALE, 1.0, 1.0, 1.0, 1.0)
_KCVC_INDEX = 3


def _inproj_kernel(x_ref, g_ref, wn_ref, wt_ref, *refs):
    n_out = len(_NAT_WIDTHS) + len(_TR_ROWS)
    nat_refs = refs[:len(_NAT_WIDTHS)]
    tr_refs = refs[len(_NAT_WIDTHS):n_out]
    rows_k, rows_v = refs[n_out:]
    x = x_ref[...]
    xn = (x * lax.rsqrt(jnp.mean(x * x, axis=-1, keepdims=True) + RMS_EPS) * g_ref[...]).astype(BF16)
    off = 0
    for idx, (ref, width) in enumerate(zip(nat_refs, _NAT_WIDTHS)):
        if idx == _KCVC_INDEX:
            res = _dot(xn, wn_ref[:, off:off + width])
            rows_k[...] = res[:, :NSA_KV_COLS]
            rows_v[...] = res[:, NSA_KV_COLS:]
            n_rows = res.shape[0] // CMP_STRIDE
            for tok in range(CMP_STRIDE):
                lo = tok * width
                ref[:, lo:lo + NSA_KV_COLS] = rows_k[pl.ds(tok, n_rows, stride=CMP_STRIDE), :]
                ref[:, lo + NSA_KV_COLS:lo + width] = rows_v[pl.ds(tok, n_rows, stride=CMP_STRIDE), :]
        else:
            for c in range(0, width, 512):
                cw = min(512, width - c)
                ref[:, c:c + cw] = _dot(xn, wn_ref[:, off + c:off + c + cw]).astype(ref.dtype)
        off += width
    n_sub = x.shape[0] // CH
    off = 0
    for ref, rows, scale in zip(tr_refs, _TR_ROWS, _TR_SCALE):
        for c in range(0, rows, 256):
            rw = min(256, rows - c)
            res = lax.dot_general(wt_ref[off + c:off + c + rw, :], xn, _NT, preferred_element_type=F32)
            if scale != 1.0:
                res = res * scale
            for j in range(n_sub):
                ref[j, c:c + rw, :] = res[:, j * CH:(j + 1) * CH].astype(ref.dtype)
        off += rows


def _inproj(h2d, g, wn, wt, layer):
    n = h2d.shape[0]
    tm = ROW_TILE
    nat_dtypes = (BF16, BF16, BF16, F32, BF16)
    tr_dtypes = (BF16, BF16, BF16, BF16, BF16, F32)
    out_shape = [jax.ShapeDtypeStruct((n, w), dt) for w, dt in zip(_NAT_WIDTHS, nat_dtypes)]
    out_shape += [jax.ShapeDtypeStruct((n // CH, r, CH), dt) for r, dt in zip(_TR_ROWS, tr_dtypes)]
    out_specs = [pl.BlockSpec((tm, w), lambda i: (i, 0)) for w in _NAT_WIDTHS]
    out_specs += [pl.BlockSpec((tm // CH, r, CH), lambda i: (i, 0, 0)) for r in _TR_ROWS]
    hb_width = CMP_STRIDE * _NAT_WIDTHS[_KCVC_INDEX]
    out_shape[_KCVC_INDEX] = jax.ShapeDtypeStruct((n // CMP_STRIDE, hb_width), F32)
    out_specs[_KCVC_INDEX] = pl.BlockSpec((tm // CMP_STRIDE, hb_width), lambda i: (i, 0))
    return pl.pallas_call(
        _inproj_kernel,
        grid=(n // tm,),
        in_specs=[pl.BlockSpec((tm, D_MODEL), lambda i: (i, 0)),
                  _layer_spec(g, layer), _layer_spec(wn, layer), _layer_spec(wt, layer)],
        out_specs=out_specs,
        out_shape=out_shape,
        scratch_shapes=[pltpu.VMEM((tm, NSA_KV_COLS), F32), pltpu.VMEM((tm, NSA_KV_COLS), F32)],
        compiler_params=_params(("parallel",)),
        name="inproj",
    )(h2d, g, wn, wt)


def _compress_kernel(hb_ref, pe_ref, w1_ref, b1_ref, w2k_ref, b2k_ref, w2v_ref, b2v_ref, kc_ref, vct_ref):
    hb = hb_ref[0]
    rows = hb.shape[0]
    top = (hb + pe_ref[0]).astype(BF16)
    bot = (hb + pe_ref[1]).astype(BF16)
    p = _dot(top, w1_ref[0])
    q = _dot(bot, w1_ref[1])
    q_next = pltpu.roll(q, rows - 1, 0)
    hid = jax.nn.gelu(p + q_next + b1_ref[...])
    width = hid.shape[1] // 2
    kc_ref[0] = (_dot(hid[:, :width].astype(BF16), w2k_ref[...]) + b2k_ref[...]).astype(kc_ref.dtype)
    vct = lax.dot_general(w2v_ref[...], hid[:, width:].astype(BF16), _NT, preferred_element_type=F32)
    vct_ref[0] = (vct + b2v_ref[...]).astype(vct_ref.dtype)


def _compress(hb, pe_hb, w1_big, b1p, w2k, b2k, w2vt, b2v, layer):
    bsz, rows, width = hb.shape
    gk = NSA_KV_COLS
    return pl.pallas_call(
        _compress_kernel,
        grid=(bsz,),
        in_specs=[pl.BlockSpec((1, rows, width), lambda b: (b, 0, 0)),
                  *[_layer_spec(a, layer) for a in (pe_hb, w1_big, b1p, w2k, b2k, w2vt, b2v)]],
        out_specs=[pl.BlockSpec((1, rows, gk), lambda b: (b, 0, 0)),
                   pl.BlockSpec((1, gk, rows), lambda b: (b, 0, 0))],
        out_shape=[jax.ShapeDtypeStruct((bsz, rows, gk), BF16),
                   jax.ShapeDtypeStruct((bsz, gk, rows), BF16)],
        compiler_params=_params(("parallel",)),
        name="compress",
    )(hb, pe_hb, w1_big, b1p, w2k, b2k, w2vt, b2v)


def _bf16_pieces(x):
    out = []
    rest = np.asarray(x, np.float32)
    for _ in range(SLOPE_PIECES):
        piece = rest.astype(BF16).astype(np.float32)
        out.append(piece)
        rest = rest - piece
    return out


def _slope_rows(slopes, heads_per_block, cols_per_head):
    sl2 = (np.asarray(slopes, np.float32).astype(np.float64) * LOG2E).astype(np.float32)
    pieces = np.stack(_bf16_pieces(sl2) * 2, axis=0)
    rows = np.zeros((ALIBI_ROWS, sl2.shape[0]), np.float32)
    rows[:pieces.shape[0]] = pieces
    rows[PAD_ROW] = -MASK_BIG
    rows = np.repeat(rows, cols_per_head, axis=1)
    rows = rows.reshape(ALIBI_ROWS, -1, heads_per_block * cols_per_head).transpose(1, 0, 2)
    return jnp.asarray(rows, BF16)


def _key_aug_tables(seq, nb):
    pos = np.arange(seq)
    aug = np.zeros((2, seq + CH, AUG_LANES), np.float32)
    aug[:, :seq, 0:SLOPE_PIECES] = (pos % CH)[None, :, None]
    aug[:, :seq, SLOPE_PIECES:2 * SLOPE_PIECES] = (pos // CH * CH)[None, :, None]
    aug[0, pos, MASK_ROW0 + pos // SEL_BLOCK] = 1.0
    aug[:, seq:, PAD_ROW] = 1.0
    return jnp.asarray(aug.reshape(2, seq // CH + 1, CH, AUG_LANES), BF16)


def _cmp_aug_table(n_rows):
    aug = np.zeros((n_rows, AUG_LANES), np.float32)
    aug[:, 0:SLOPE_PIECES] = (np.arange(n_rows) * CMP_STRIDE)[:, None]
    aug[:, SLOPE_PIECES:2 * SLOPE_PIECES] = CMP_BLOCK - 1
    return jnp.asarray(aug, BF16)


def _tile_lanes(x, reps):
    return jnp.concatenate([x] * reps, axis=1)


def _query_minus_key(reps):
    shape = (CH, reps * CH)
    q_off = jnp.bitwise_and(lax.broadcasted_iota(jnp.int32, shape, 1), CH - 1)
    return q_off - lax.broadcasted_iota(jnp.int32, shape, 0)


def _flash_init(m_ref, acc_ref):
    m_ref[...] = jnp.full(m_ref.shape, NEG_INF, F32)
    acc_ref[...] = jnp.zeros(acc_ref.shape, F32)


def _normalized(acc_ref, g):
    dv = acc_ref.shape[1] - SUM_ROWS
    return acc_ref[g, 0:dv, :] / acc_ref[g, dv:dv + 1, :]


def _chunk_scores(k_blk, aug_blk, qa_ref):
    return _dot(jnp.concatenate([k_blk, aug_blk], axis=1), qa_ref[...])


def _stage_and_consume(prod, cons, qa_ref, block_cols=BLOCK_COLS):
    if prod is not None:
        k_blk, aug_blk, (ps_ref, pmx_ref), pmask = prod
        k_full = jnp.concatenate([k_blk, aug_blk], axis=1)
    if cons is not None:
        (cs_ref, cmx_ref), v_t, (m_ref, acc_ref), cmask = cons
        _, rows, gcols = acc_ref.shape
        dv = rows - SUM_ROWS
        ones = jnp.ones((SUM_ROWS, v_t.shape[1]), BF16)
    for c0 in range(0, qa_ref.shape[1], block_cols):
        csl = slice(c0, c0 + block_cols)
        if prod is not None:
            s = _dot(k_full, qa_ref[:, csl])
            if pmask is not None:
                s = jnp.where(pmask(c0), s, NEG_INF)
            ps_ref[:, csl] = s
            pmx_ref[:, csl] = jnp.max(s, axis=0, keepdims=True)
        if cons is not None:
            s = cs_ref[:, csl]
            if cmask is None:
                mx = cmx_ref[:, csl]
            else:
                s = jnp.where(cmask(c0), s, NEG_INF)
                mx = jnp.max(s, axis=0, keepdims=True)
            m_prev = m_ref[:, csl]
            m_new = jnp.maximum(m_prev, mx)
            alpha = jnp.exp2(m_prev - m_new)
            p = jnp.exp2(s - m_new).astype(BF16)
            g = c0 // gcols
            gsl = slice(c0 - g * gcols, c0 - g * gcols + block_cols)
            v_ones = jnp.concatenate([v_t[g * dv:(g + 1) * dv], ones], axis=0)
            acc_ref[g, :, gsl] = alpha * acc_ref[g, :, gsl] + _dot(v_ones, p)
            m_ref[:, csl] = m_new


def _consume(buf, v_t, state, qa_ref, mask=None, block_cols=BLOCK_COLS):
    _stage_and_consume(None, (buf, v_t, state, mask), qa_ref, block_cols)


def _pad_or(aug_ref, c, is_pad):
    return aug_ref[jnp.where(is_pad, aug_ref.shape[0] - 1, c)]


def _identity(pos):
    return pos


def _causal_first(n, k_ref, aug_ref, buf0, chunk_at=_identity):
    c = chunk_at(0)
    return (k_ref[c], _pad_or(aug_ref, c, jnp.bitwise_and(n, 1) == 1), buf0, None)


def _causal_pairs(n, k_ref, v_ref, aug_ref, qa_ref, bufs, state, block_cols=BLOCK_COLS, chunk_at=_identity):
    _causal_pairs_multi(n, [(k_ref, v_ref, aug_ref, qa_ref, bufs, state)], block_cols, chunk_at)


def _causal_pairs_multi(n, streams, block_cols=BLOCK_COLS, chunk_at=_identity):
    pad = jnp.bitwise_and(n, 1)

    def pair(k, carry):
        pos = 2 * k - pad
        c0 = chunk_at(jnp.maximum(pos, 0))
        c1 = chunk_at(pos + 1)
        c2 = chunk_at(pos + 2)
        for k_ref, v_ref, aug_ref, qa_ref, (buf0, buf1), state in streams:
            _stage_and_consume((k_ref[c1], aug_ref[c1], buf1, None), (buf0, v_ref[c0], state, None),
                               qa_ref, block_cols)
        for k_ref, v_ref, aug_ref, qa_ref, (buf0, buf1), state in streams:
            _stage_and_consume((k_ref[c2], aug_ref[c2], buf0, None), (buf1, v_ref[c1], state, None),
                               qa_ref, block_cols)
        return carry
    lax.fori_loop(0, jnp.right_shift(n + pad, 1), pair, 0)


def _unselected_mask_rows(imp, k_sel):
    n_blk, width = imp.shape
    rows_per = 8
    j_loc = lax.broadcasted_iota(jnp.int32, (rows_per, width), 0)
    mask_blocks = []
    for r0 in range(0, n_blk, rows_per):
        blk = imp[r0:r0 + rows_per, :]
        cnt = jnp.zeros((rows_per, width), jnp.int32)
        for jp in range(n_blk):
            row = imp[jp:jp + 1, :]
            gt = jnp.where(row > blk, 1, 0)
            ge = jnp.where(row >= blk, 1, 0)
            if jp >= r0 + rows_per - 1:
                cnt = cnt + gt
            elif jp < r0:
                cnt = cnt + ge
            else:
                cnt = cnt + jnp.where(j_loc + r0 > jp, ge, gt)
        mask_blocks.append(jnp.where(cnt < k_sel, 0.0, -MASK_BIG))
    return jnp.concatenate(mask_blocks, axis=0)


def _nsa_kernel(q_ref, srow_ref, ks_ref, kw_ref, vs_ref, vw_ref, kaug_ref, kc_ref, kcaug_ref, vc_ref,
                ng_ref, asel_ref, o_ref, qa_ref, ocmp_ref, s0, x0, s1, x1, w0, y0, w1, y1, w2, y2,
                m_s, acc_s, m_w, acc_w, chunk_list):
    i = pl.program_id(1)
    t0 = i * CH
    nh = NSA_GROUP_SIZE
    ng = NSA_KV_GROUPS
    gcols = nh * CH
    cols = ng * gcols
    n_cmp = kc_ref.shape[1]
    n_blk = asel_ref.shape[0]
    k_sel = min(SEL_TOPK, n_blk)
    bufs_s = ((s0, x0), (s1, x1))
    bufs_w = ((w0, y0), (w1, y1), (w2, y2))
    state_s = (m_s, acc_s)
    state_w = (m_w, acc_w)
    kaug_s = kaug_ref.at[0]
    kaug_w = kaug_ref.at[1]

    t_pos = t0 + lax.broadcasted_iota(jnp.int32, (1, CH), 1)
    d0_i = _query_minus_key(BLOCK_COLS // CH)
    causal = lambda c0: d0_i >= 0
    window_edge = lambda c0: d0_i < 0

    zeros_q = jnp.zeros((HEAD_DIM, CH), BF16)
    for g in range(ng):
        for hh in range(nh):
            h = g * nh + hh
            qh = q_ref[0, h * HEAD_DIM:(h + 1) * HEAD_DIM, :]
            for gg in range(ng):
                qa_ref[gg * HEAD_DIM:(gg + 1) * HEAD_DIM, h * CH:(h + 1) * CH] = qh if gg == g else zeros_q
    qa_ref[QK_LANES:QK_LANES + ALIBI_ROWS, :] = srow_ref[...]
    qa_ref[QK_LANES + MASK_ROW0:, :] = jnp.zeros((AUG_LANES - MASK_ROW0, cols), BF16)

    cmp_scores = _chunk_scores(kc_ref[0], kcaug_ref[...], qa_ref)

    n_back = WINDOW // CH
    stage_w = []
    chunks_w = []
    for back in range(n_back, 0, -1):
        c = jnp.maximum(i - back, 0)
        stage_w.append((kw_ref[c], _pad_or(kaug_w, c, i < back), bufs_w[n_back - back],
                        window_edge if back == n_back else None))
        chunks_w.append(c)
    stage_w.append((kw_ref[i], kaug_w[i], bufs_w[n_back], causal))
    chunks_w.append(i)
    _stage_and_consume(stage_w[0], None, qa_ref)

    n_idx = lax.broadcasted_iota(jnp.int32, (n_cmp, cols), 0)
    t_pos_all = t0 + jnp.bitwise_and(lax.broadcasted_iota(jnp.int32, (1, cols), 1), CH - 1)
    valid_c = n_idx * CMP_STRIDE + (CMP_BLOCK - 1) <= t_pos_all
    lg = jnp.where(valid_c, cmp_scores, NEG_INF)
    m = jnp.max(lg, axis=0, keepdims=True)
    p = jnp.where(valid_c, jnp.exp2(lg - m), 0.0)
    l = jnp.sum(p, axis=0, keepdims=True)
    pc = p * jnp.where(l > 0.0, 1.0 / l, 0.0)
    pc_b = pc.astype(BF16)
    vc = vc_ref[0]

    j_idx = lax.broadcasted_iota(jnp.int32, (n_blk, CH), 0)
    cur = jnp.right_shift(t_pos, SEL_BLOCK.bit_length() - 1)
    forced = (j_idx == 0) | (j_idx == cur) | (j_idx == cur - 1)
    in_past = j_idx * SEL_BLOCK <= t_pos
    for g in range(ng):
        gsl = slice(g * gcols, (g + 1) * gcols)
        ocmp_ref[g] = _dot(vc[g * HEAD_DIM:(g + 1) * HEAD_DIM], pc_b[:, gsl])
        psum = pc[:, g * gcols:g * gcols + CH]
        for hh in range(1, nh):
            psum = psum + pc[:, g * gcols + hh * CH:g * gcols + (hh + 1) * CH]
        imp = jnp.dot(asel_ref[...], psum, precision=lax.Precision.HIGHEST, preferred_element_type=F32)
        imp = jnp.where(in_past, jnp.where(forced, FORCED_SCORE, imp), -1.0)
        mask_rows = _unselected_mask_rows(imp, k_sel)
        kept = mask_rows if g == 0 else jnp.maximum(kept, mask_rows)
        qa_ref[QK_LANES + MASK_ROW0:QK_LANES + MASK_ROW0 + n_blk, gsl] = _tile_lanes(mask_rows.astype(BF16), nh)

    blocks_per_chunk = CH // SEL_BLOCK
    n_sel = jnp.int32(0)
    for c in range(n_blk // blocks_per_chunk):
        hit = jnp.max(kept[c * blocks_per_chunk:(c + 1) * blocks_per_chunk, :]) > -1.0
        chunk_list[n_sel] = c
        n_sel = n_sel + jnp.logical_and(hit, c < i).astype(jnp.int32)
    chunk_list[n_sel] = i

    def chunk_at(pos):
        return chunk_list[pos]

    _flash_init(*state_w)
    for j in range(1, n_back + 1):
        _stage_and_consume(stage_w[j], (bufs_w[j - 1], vw_ref[chunks_w[j - 1]], state_w, None), qa_ref)
    _stage_and_consume(_causal_first(n_sel, ks_ref, kaug_s, bufs_s[0], chunk_at),
                       (bufs_w[n_back], vw_ref[i], state_w, None), qa_ref)
    _flash_init(*state_s)
    _causal_pairs(n_sel, ks_ref, vs_ref, kaug_s, qa_ref, bufs_s, state_s, chunk_at=chunk_at)
    _consume(bufs_s[0], vs_ref[i], state_s, qa_ref, mask=causal)

    for g in range(ng):
        def gate(branch):
            r0 = g * GATE_ROWS_PER_GROUP + branch * nh
            return jax.nn.sigmoid(jnp.concatenate([ng_ref[0, r0 + hh:r0 + hh + 1, :] for hh in range(nh)], axis=1))
        out = gate(0) * ocmp_ref[g] + gate(1) * _normalized(acc_s, g) + gate(2) * _normalized(acc_w, g)
        for hh in range(nh):
            h = g * nh + hh
            o_ref[0, h * HEAD_DIM:(h + 1) * HEAD_DIM, :] = out[:, hh * CH:(hh + 1) * CH].astype(o_ref.dtype)


def _nsa(nq_t, srows, ks3, kw3, vs_t, vw_t, kaug, kc, kcaug, vc_t, ng_t, asel_t, bsz, nq):
    n_cmp = kc.shape[1]
    ng = NSA_KV_GROUPS
    gcols = NSA_GROUP_SIZE * CH
    cols = ng * gcols
    acc_shape = (ng, HEAD_DIM + SUM_ROWS, gcols)
    stage = [pltpu.VMEM((CH, cols), F32), pltpu.VMEM((1, cols), F32)]
    return pl.pallas_call(
        _nsa_kernel,
        grid=(bsz, nq),
        in_specs=[
            pl.BlockSpec((1, NSA_Q_COLS, CH), lambda b, i: (b * nq + i, 0, 0)),
            pl.BlockSpec(srows.shape, lambda b, i: (0, 0)),
            pl.BlockSpec((nq, CH, NSA_KV_COLS), lambda b, i: (b, 0, 0)),
            pl.BlockSpec((nq, CH, NSA_KV_COLS), lambda b, i: (b, 0, 0)),
            pl.BlockSpec((nq, NSA_KV_COLS, CH), lambda b, i: (b, 0, 0)),
            pl.BlockSpec((nq, NSA_KV_COLS, CH), lambda b, i: (b, 0, 0)),
            pl.BlockSpec(kaug.shape, lambda b, i: (0, 0, 0, 0)),
            pl.BlockSpec((1, n_cmp, NSA_KV_COLS), lambda b, i: (b, 0, 0)),
            pl.BlockSpec(kcaug.shape, lambda b, i: (0, 0)),
            pl.BlockSpec((1, NSA_KV_COLS, n_cmp), lambda b, i: (b, 0, 0)),
            pl.BlockSpec((1, ng * GATE_ROWS_PER_GROUP, CH), lambda b, i: (b * nq + i, 0, 0)),
            pl.BlockSpec(asel_t.shape, lambda b, i: (0, 0)),
        ],
        out_specs=pl.BlockSpec((1, NSA_Q_COLS, CH), lambda b, i: (b * nq + i, 0, 0)),
        out_shape=jax.ShapeDtypeStruct((bsz * nq, NSA_Q_COLS, CH), BF16),
        scratch_shapes=(
            [pltpu.VMEM((QK_LANES + AUG_LANES, cols), BF16), pltpu.VMEM((ng, HEAD_DIM, gcols), F32)]
            + stage * 5
            + [pltpu.VMEM((1, cols), F32), pltpu.VMEM(acc_shape, F32),
               pltpu.VMEM((1, cols), F32), pltpu.VMEM(acc_shape, F32),
               pltpu.SMEM((nq + 1,), jnp.int32)]),
        compiler_params=_params(("parallel", "arbitrary")),
        name="nsa_attention",
    )(nq_t, srows, ks3, kw3, vs_t, vw_t, kaug, kc, kcaug, vc_t, ng_t, asel_t)


def _diff_kernel(scal_ref, q_ref, srow_ref, k_ref, v_ref, kaug_ref, lam_ref, gain_ref, o_ref, *scratch):
    nt = DIFF_TILE_CHUNKS
    tq = nt * CH
    hd2 = 2 * HEAD_DIM
    c0 = pl.program_id(2) * nt
    lam_init = scal_ref[0]
    out_scale = scal_ref[1]
    sub2 = lax.broadcasted_iota(jnp.int32, (QK_LANES, CH), 0)
    zero = jnp.zeros((), BF16)
    bc = 2 * tq
    lane = lax.broadcasted_iota(jnp.int32, (CH, bc), 1)
    sub = lax.broadcasted_iota(jnp.int32, (CH, bc), 0)

    def on_or_after(first_key):
        return lambda col0: jnp.bitwise_and(lane + col0, tq - 1) - sub >= first_key

    streams = []
    for hd in range(DIFF_HEADS_PER_STEP):
        qa_ref, s0, x0, s1, x1, m_r, acc_r = scratch[7 * hd:7 * (hd + 1)]
        rows = slice(hd * hd2, (hd + 1) * hd2)
        for j in range(nt):
            q = q_ref[j, rows, :]
            qa_ref[0:QK_LANES, j * CH:(j + 1) * CH] = jnp.where(sub2 < HEAD_DIM, q, zero)
            qa_ref[0:QK_LANES, tq + j * CH:tq + (j + 1) * CH] = jnp.where(sub2 >= HEAD_DIM, q, zero)
        qa_ref[QK_LANES:QK_LANES + ALIBI_ROWS, :] = srow_ref[hd]
        qa_ref[QK_LANES + MASK_ROW0:, :] = jnp.zeros((AUG_LANES - MASK_ROW0, bc), BF16)
        streams.append((k_ref.at[:, :, rows], v_ref.at[:, rows, :], kaug_ref, qa_ref,
                        ((s0, x0), (s1, x1)), (m_r, acc_r)))

    for k_at, v_at, aug, qa_ref, bufs, state in streams:
        _stage_and_consume(_causal_first(c0, k_at, aug, bufs[0]), None, qa_ref, bc)
        _flash_init(*state)
    _causal_pairs_multi(c0, streams, bc)
    for k_at, v_at, aug, qa_ref, bufs, state in streams:
        _stage_and_consume((k_at[c0 + 1], aug[c0 + 1], bufs[1], on_or_after(CH)),
                           (bufs[0], v_at[c0], state, on_or_after(0)), qa_ref, bc)
    for k_at, v_at, aug, qa_ref, bufs, state in streams:
        _consume(bufs[1], v_at[c0 + 1], state, qa_ref, block_cols=bc)

    lp = lam_ref[...]
    lam = (jnp.exp(jnp.sum(lp[0:1] * lp[1:2], axis=1, keepdims=True))
           - jnp.exp(jnp.sum(lp[2:3] * lp[3:4], axis=1, keepdims=True)) + lam_init)
    for hd, stream in enumerate(streams):
        att = _normalized(stream[5][1], 0)
        o = att[:, 0:tq] - lam * att[:, tq:2 * tq]
        o = o * lax.rsqrt(jnp.mean(o * o, axis=0, keepdims=True) + SUBLN_EPS) * gain_ref[...]
        o = (o * out_scale).astype(o_ref.dtype)
        for j in range(nt):
            o_ref[j, hd * hd2:(hd + 1) * hd2, :] = o[:, j * CH:(j + 1) * CH]


def _diff(scal, dq_t, srows, dk3, dv_t, kaug_plain, lam_p, gain, bsz, nq):
    hd2 = 2 * HEAD_DIM
    nt = DIFF_TILE_CHUNKS
    n_tiles = nq // nt
    cols = 2 * nt * CH
    hps = DIFF_HEADS_PER_STEP
    rows = hps * hd2
    per_head = [
        pltpu.VMEM((QK_LANES + AUG_LANES, cols), BF16),
        pltpu.VMEM((CH, cols), F32), pltpu.VMEM((1, cols), F32),
        pltpu.VMEM((CH, cols), F32), pltpu.VMEM((1, cols), F32),
        pltpu.VMEM((1, cols), F32), pltpu.VMEM((1, hd2 + SUM_ROWS, cols), F32),
    ]
    grid_spec = pltpu.PrefetchScalarGridSpec(
        num_scalar_prefetch=1,
        grid=(bsz, DIFF_HEADS // hps, n_tiles),
        in_specs=[
            pl.BlockSpec((nt, rows, CH), lambda b, h, i, s: (b * n_tiles + i, h, 0)),
            pl.BlockSpec((hps, ALIBI_ROWS, cols), lambda b, h, i, s: (h, 0, 0)),
            pl.BlockSpec((nq, CH, rows), lambda b, h, i, s: (b, 0, h)),
            pl.BlockSpec((nq, rows, CH), lambda b, h, i, s: (b, h, 0)),
            pl.BlockSpec(kaug_plain.shape, lambda b, h, i, s: (0, 0, 0)),
            pl.BlockSpec(lam_p.shape, lambda b, h, i, s: (0, 0)),
            pl.BlockSpec(gain.shape, lambda b, h, i, s: (0, 0)),
        ],
        out_specs=pl.BlockSpec((nt, rows, CH), lambda b, h, i, s: (b * n_tiles + i, h, 0)),
        scratch_shapes=per_head * hps,
    )
    return pl.pallas_call(
        _diff_kernel,
        grid_spec=grid_spec,
        out_shape=jax.ShapeDtypeStruct((bsz * nq, DIFF_V_COLS, CH), BF16),
        compiler_params=_params(("parallel", "parallel", "arbitrary")),
        name="diff_attention",
    )(scal, dq_t, srows, dk3, dv_t, kaug_plain, lam_p, gain)


def _merge_kernel(ya_ref, yb_ref, gates_ref, h_ref, wa_ref, wb_ref, wo_ref, o_ref):
    for j in range(ya_ref.shape[0]):
        rows = slice(j * CH, (j + 1) * CH)
        a = lax.dot_general(ya_ref[j], wa_ref[...], _TN, preferred_element_type=F32)
        b = lax.dot_general(yb_ref[j], wb_ref[...], _TN, preferred_element_type=F32)
        ga = jax.nn.sigmoid(gates_ref[rows, :D_MODEL].astype(F32))
        gb = jax.nn.sigmoid(gates_ref[rows, D_MODEL:].astype(F32))
        merged = (ga * a + gb * b).astype(BF16)
        o_ref[rows, :] = h_ref[rows, :] + _dot(merged, wo_ref[...])


def _merge(ya_t, yb_t, gates, h2d, wa, wb, wo, layer):
    n = h2d.shape[0]
    tm = ROW_TILE
    return pl.pallas_call(
        _merge_kernel,
        grid=(n // tm,),
        in_specs=[pl.BlockSpec((tm // CH, NSA_Q_COLS, CH), lambda i: (i, 0, 0)),
                  pl.BlockSpec((tm // CH, DIFF_V_COLS, CH), lambda i: (i, 0, 0)),
                  pl.BlockSpec((tm, 2 * D_MODEL), lambda i: (i, 0)),
                  pl.BlockSpec((tm, D_MODEL), lambda i: (i, 0)),
                  _layer_spec(wa, layer), _layer_spec(wb, layer), _layer_spec(wo, layer)],
        out_specs=pl.BlockSpec((tm, D_MODEL), lambda i: (i, 0)),
        out_shape=jax.ShapeDtypeStruct((n, D_MODEL), F32),
        compiler_params=_params(("parallel",)),
        name="merge_outproj",
    )(ya_t, yb_t, gates, h2d, wa, wb, wo)


def _first_argmax(x, rows, n):
    mx = jnp.max(x, axis=0, keepdims=True)
    idx = jnp.min(jnp.where(x == mx, rows, n), axis=0, keepdims=True)
    return mx, idx


def _moe_kernel(h_ref, g_ref, wr_ref, br_ref, eexp_ref, wg_ref, wu_ref, wd_ref, fg_ref, o_ref, *, final):
    hres = h_ref[...]
    xf = hres * lax.rsqrt(jnp.mean(hres * hres, axis=-1, keepdims=True) + RMS_EPS) * g_ref[...]
    xb = xf.astype(BF16)
    tm = hres.shape[0]

    x_lo = (xf - xb.astype(F32)).astype(BF16)
    n_r = wr_ref.shape[0] // 2
    part = lax.dot_general(wr_ref[...], xb, _NT, preferred_element_type=F32)
    logits = (part[0:n_r] + part[n_r:2 * n_r]
              + lax.dot_general(wr_ref[0:n_r, :], x_lo, _NT, preferred_element_type=F32)
              + br_ref[...])
    gl = logits[0:MOE_GROUPS]
    rows_g = lax.broadcasted_iota(jnp.int32, (MOE_GROUPS, tm), 0)
    gmax, gidx = _first_argmax(gl, rows_g, MOE_GROUPS)
    g_w = 1.0 / jnp.sum(jnp.exp(gl - gmax), axis=0, keepdims=True)
    esel = jnp.zeros((EXPERTS_PER_GROUP, tm), F32)
    for gg in range(MOE_GROUPS):
        lo = MOE_GROUPS + gg * EXPERTS_PER_GROUP
        esel = jnp.where(gidx == gg, logits[lo:lo + EXPERTS_PER_GROUP], esel)
    rows_e = lax.broadcasted_iota(jnp.int32, (EXPERTS_PER_GROUP, tm), 0)
    v1, i1 = _first_argmax(esel, rows_e, EXPERTS_PER_GROUP)
    rest = jnp.where(rows_e == i1, -jnp.inf, esel)
    v2, i2 = _first_argmax(rest, rows_e, EXPERTS_PER_GROUP)
    e21 = jnp.exp(v2 - v1)
    w1 = g_w / (1.0 + e21)
    w2 = g_w * e21 / (1.0 + e21)
    rows_c = lax.broadcasted_iota(jnp.int32, (N_EXPERTS, tm), 0)
    grp_c = jnp.right_shift(rows_c, EXPERTS_PER_GROUP.bit_length() - 1)
    exp_c = jnp.bitwise_and(rows_c, EXPERTS_PER_GROUP - 1)
    comb = jnp.where(grp_c == gidx,
                     jnp.where(exp_c == i1, w1, 0.0) + jnp.where(exp_c == i2, w2, 0.0), 0.0)
    comb_hi = comb.astype(BF16)
    comb_lo = (comb - comb_hi.astype(F32)).astype(BF16)

    comb_nat = (lax.dot_general(comb_hi, eexp_ref[...], _TN, preferred_element_type=F32)
                + lax.dot_general(comb_lo, eexp_ref[...], _TN, preferred_element_type=F32))

    acc = hres
    n_ff = wg_ref.shape[1]
    step = 512
    for c in range(0, n_ff, step):
        cols = slice(c, c + step)
        hg = _dot(xb, wg_ref[:, cols])
        hu = _dot(xb, wu_ref[:, cols])
        act = jax.nn.silu(hg) * hu
        parts = []
        for e0 in range(0, step, EXPERT_FF):
            e = (c + e0) // EXPERT_FF
            parts.append((act[:, e0:e0 + EXPERT_FF] * comb_nat[:, e:e + 1]).astype(BF16))
        acc = acc + _dot(jnp.concatenate(parts, axis=1), wd_ref[cols, :])
    if final:
        acc = acc * lax.rsqrt(jnp.mean(acc * acc, axis=-1, keepdims=True) + RMS_EPS) * fg_ref[...]
    o_ref[...] = acc


def _moe(h2d, g, wr_t, br, eexp, wg, wu, wd, fg, layer, final):
    n = h2d.shape[0]
    tm = ROW_TILE
    return pl.pallas_call(
        functools.partial(_moe_kernel, final=final),
        grid=(n // tm,),
        in_specs=[pl.BlockSpec((tm, D_MODEL), lambda i: (i, 0)),
                  _layer_spec(g, layer), _layer_spec(wr_t, layer), _layer_spec(br, layer),
                  _const_spec(eexp.shape), _layer_spec(wg, layer), _layer_spec(wu, layer),
                  _layer_spec(wd, layer), _const_spec(fg.shape)],
        out_specs=pl.BlockSpec((tm, D_MODEL), lambda i: (i, 0)),
        out_shape=jax.ShapeDtypeStruct((n, D_MODEL), F32),
        compiler_params=_params(("parallel",)),
        name="moe_final" if final else "moe",
    )(h2d, g, wr_t, br, eexp, wg, wu, wd, fg)


def _split_points():
    sizes = ([NSA_Q_COLS] + [NSA_KV_COLS] * 6
             + [NSA_GATE_COLS, DIFF_QK_COLS, DIFF_QK_COLS, DIFF_V_COLS, D_MODEL, D_MODEL])
    return [int(v) for v in np.cumsum(sizes)[:-1]]


def _cmp_to_sel_t(n_rows, nc, nb):
    c0 = np.arange(nc)[:, None] * CMP_STRIDE
    s0 = np.arange(nb)[None, :] * SEL_BLOCK
    ov = np.maximum(0, np.minimum(c0 + CMP_BLOCK, s0 + SEL_BLOCK) - np.maximum(c0, s0)) / CMP_BLOCK
    out = np.zeros((nb, n_rows), np.float32)
    out[:, :nc] = ov.T
    return out


def kernel(x, norm1_g, w_in, cmp_pe, cmp_w1, cmp_b1, cmp_w2, cmp_b2, diff_lambda, diff_subln_g, w_branch_a, w_branch_b, w_out, norm2_g, router_grp_w, router_grp_b, router_exp_w, router_exp_b, exp_w_gate, exp_w_up, exp_w_down, final_norm_g):
    bsz, seq, d = x.shape
    depth = w_in.shape[0]
    n = bsz * seq
    nq = seq // CH
    n_half = seq // CMP_STRIDE
    nc = (seq - CMP_BLOCK) // CMP_STRIDE + 1
    nb = seq // SEL_BLOCK
    assert d == D_MODEL and seq % ROW_TILE == 0 and seq >= WINDOW and WINDOW % CH == 0
    assert nb % 16 == 0 and MASK_ROW0 + nb <= AUG_LANES and n_half <= 256
    g_kv = NSA_KV_GROUPS
    eye_g = jnp.eye(g_kv, dtype=F32)

    (nq_w, kc_w, vc_w, ks_w, vs_w, kw_w, vw_w, ng_w, dq_w, dk_w, dv_w, ga_w, gb_w) = jnp.split(
        w_in, _split_points(), axis=-1)
    ng_w = ng_w.reshape(depth, d, g_kv, NSA_GROUP_SIZE, 3).transpose(0, 1, 2, 4, 3)
    ng_w = ng_w.reshape(depth, d, g_kv, 3 * NSA_GROUP_SIZE)
    ng_w = jnp.pad(ng_w, ((0, 0), (0, 0), (0, 0), (0, GATE_ROWS_PER_GROUP - 3 * NSA_GROUP_SIZE)))
    ng_w = ng_w.reshape(depth, d, g_kv * GATE_ROWS_PER_GROUP)
    wn_all = jnp.concatenate([ks_w, kw_w, dk_w, kc_w, vc_w, ga_w, gb_w], axis=-1).astype(BF16)
    wt_all = jnp.concatenate([nq_w, dq_w, vs_w, vw_w, dv_w, ng_w], axis=-1)
    wt_all = jnp.swapaxes(wt_all, 1, 2).astype(BF16)

    w1r = cmp_w1.reshape(depth, 2, 2, CMP_STRIDE, HEAD_DIM, CMP_HIDDEN).astype(BF16)
    w1_big = jnp.zeros((depth, 2, CMP_STRIDE, 2 * NSA_KV_COLS, 2 * g_kv * CMP_HIDDEN), BF16)
    for c in range(2):
        for g in range(g_kv):
            r0 = c * NSA_KV_COLS + g * HEAD_DIM
            c0 = (c * g_kv + g) * CMP_HIDDEN
            w1_big = w1_big.at[:, :, :, r0:r0 + HEAD_DIM, c0:c0 + CMP_HIDDEN].set(w1r[:, c])
    w1_big = w1_big.reshape(depth, 2, CMP_STRIDE * 2 * NSA_KV_COLS, 2 * g_kv * CMP_HIDDEN)
    per = cmp_pe.reshape(depth, 2, 2, CMP_STRIDE, HEAD_DIM)
    pe_hb = jnp.einsum('Lchld,g->Lhlcgd', per, jnp.ones((g_kv,), F32))
    pe_hb = pe_hb.reshape(depth, 2, 1, CMP_STRIDE * 2 * NSA_KV_COLS)
    b1p = jnp.broadcast_to(cmp_b1[:, :, None, :], (depth, 2, g_kv, CMP_HIDDEN)).reshape(depth, 1, -1)
    w2k = jnp.einsum('Lfd,gG->LgfGd', cmp_w2[:, 0], eye_g).reshape(depth, g_kv * CMP_HIDDEN, NSA_KV_COLS)
    w2vt = jnp.einsum('Lfd,gG->LGdgf', cmp_w2[:, 1], eye_g).reshape(depth, NSA_KV_COLS, g_kv * CMP_HIDDEN)
    w2k = w2k.astype(BF16)
    w2vt = w2vt.astype(BF16)
    b2k = jnp.tile(cmp_b2[:, 0], (1, g_kv))[:, None, :]
    b2v = jnp.tile(cmp_b2[:, 1], (1, g_kv))[:, :, None]

    slopes = _alibi_slopes()
    nsa_srows = _slope_rows(slopes[:NSA_HEADS], NSA_HEADS, CH)[0]
    diff_srows = _slope_rows(np.repeat(slopes[NSA_HEADS:], 2), 2, DIFF_TILE_CHUNKS * CH)
    kaug = _key_aug_tables(seq, nb)
    kcaug = _cmp_aug_table(n_half)
    asel_t = jnp.asarray(_cmp_to_sel_t(n_half, nc, nb))

    wa_all = w_branch_a.astype(BF16)
    wb_all = w_branch_b.astype(BF16)
    wo_all = w_out.astype(BF16)

    wr = jnp.concatenate([router_grp_w, router_exp_w.reshape(depth, d, N_EXPERTS)], axis=-1)
    n_r = MOE_GROUPS + N_EXPERTS
    wr_t = jnp.pad(jnp.swapaxes(wr, 1, 2), ((0, 0), (0, 32 - n_r), (0, 0)))
    wr_hi = wr_t.astype(BF16)
    wr_t = jnp.concatenate([wr_hi, (wr_t - wr_hi.astype(F32)).astype(BF16)], axis=1)
    br = jnp.concatenate([router_grp_b, router_exp_b.reshape(depth, N_EXPERTS)], axis=-1)
    br = jnp.pad(br, ((0, 0), (0, 32 - n_r)))[:, :, None]
    eexp = jnp.asarray(np.eye(N_EXPERTS, 128, dtype=np.float32), BF16)
    wg_all = jnp.swapaxes(exp_w_gate, 1, 2).reshape(depth, d, N_EXPERTS * EXPERT_FF).astype(BF16)
    wu_all = jnp.swapaxes(exp_w_up, 1, 2).reshape(depth, d, N_EXPERTS * EXPERT_FF).astype(BF16)
    wd_all = exp_w_down.reshape(depth, N_EXPERTS * EXPERT_FF, d).astype(BF16)

    h = x.reshape(n, d)
    for l in range(depth):
        (ks, kw, dk, kcvc_hb, gates, nq_t, dq_t, vs_t, vw_t, dv_t, ng_t) = _inproj(
            h, norm1_g[:, None, :], wn_all, wt_all, l)
        hb = kcvc_hb.reshape(bsz, n_half, CMP_STRIDE * 2 * NSA_KV_COLS)
        kc, vc_t = _compress(hb, pe_hb, w1_big, b1p, w2k, b2k, w2vt, b2v, l)
        ya_t = _nsa(nq_t, nsa_srows, ks.reshape(n // CH, CH, NSA_KV_COLS), kw.reshape(n // CH, CH, NSA_KV_COLS),
                    vs_t, vw_t, kaug, kc, kcaug, vc_t, ng_t, asel_t, bsz, nq)
        lam_init = 0.8 - 0.6 * float(np.exp(-0.3 * l))
        scal = jnp.asarray([lam_init, 1.0 - lam_init], F32)
        yb_t = _diff(scal, dq_t, diff_srows, dk.reshape(n // CH, CH, DIFF_QK_COLS), dv_t, kaug[1],
                     diff_lambda[l], diff_subln_g[l][:, None], bsz, nq)
        h = _merge(ya_t, yb_t, gates, h, wa_all, wb_all, wo_all, l)
        h = _moe(h, norm2_g[:, None, :], wr_t, br, eexp, wg_all, wu_all, wd_all,
                 final_norm_g[None, :], l, final=(l == depth - 1))
    return h.reshape(bsz, seq, d)
```

```python
import functools

import numpy as np
import jax
import jax.numpy as jnp
from jax import lax
from jax.experimental import pallas as pl
from jax.experimental.pallas import tpu as pltpu

F32 = jnp.float32
BF16 = jnp.bfloat16

D_MODEL = 1024
HEAD_DIM = 64
NSA_HEADS = 8
NSA_KV_GROUPS = 2
NSA_GROUP_SIZE = NSA_HEADS // NSA_KV_GROUPS
CMP_BLOCK = 32
CMP_STRIDE = 16
CMP_HIDDEN = 128
SEL_BLOCK = 64
SEL_TOPK = 8
WINDOW = 512
FORCED_SCORE = 1e9
DIFF_HEADS = 4
MOE_GROUPS = 4
EXPERTS_PER_GROUP = 4
N_EXPERTS = MOE_GROUPS * EXPERTS_PER_GROUP
EXPERT_FF = D_MODEL // 8
RMS_EPS = 1e-6
SUBLN_EPS = 1e-5
NEG_INF = -1e30
N_ALIBI_HEADS = NSA_HEADS + DIFF_HEADS

NSA_Q_COLS = NSA_HEADS * HEAD_DIM
NSA_KV_COLS = NSA_KV_GROUPS * HEAD_DIM
NSA_GATE_COLS = 3 * NSA_HEADS
DIFF_QK_COLS = DIFF_HEADS * 2 * HEAD_DIM
DIFF_V_COLS = DIFF_HEADS * 2 * HEAD_DIM
GATE_ROWS_PER_GROUP = 16

CH = 256
DIFF_TILE_CHUNKS = 2
DIFF_HEADS_PER_STEP = 4
ROW_TILE = 1024
VMEM_LIMIT = 56 * 1024 * 1024

LOG2E = float(np.log2(np.e))
Q_SCALE = HEAD_DIM ** -0.5 * LOG2E

QK_LANES = 2 * HEAD_DIM
AUG_LANES = 128
SLOPE_PIECES = 3
PAD_ROW = 2 * SLOPE_PIECES
ALIBI_ROWS = 16
MASK_ROW0 = ALIBI_ROWS
MASK_BIG = 1e30
SUM_ROWS = 16
BLOCK_COLS = 256

_NT = (((1,), (1,)), ((), ()))
_TN = (((0,), (0,)), ((), ()))


def _dot(a, b):
    return jnp.dot(a, b, preferred_element_type=F32)


def _const_spec(shape):
    nd = len(shape)
    return pl.BlockSpec(shape, lambda *_: (0,) * nd, pipeline_mode=pl.Buffered(1))


def _layer_spec(stacked, layer):
    nd = stacked.ndim - 1
    return pl.BlockSpec((None,) + tuple(stacked.shape[1:]), lambda *_: (layer,) + (0,) * nd,
                        pipeline_mode=pl.Buffered(1))


def _params(sem):
    return pltpu.CompilerParams(dimension_semantics=sem, vmem_limit_bytes=VMEM_LIMIT)


def _alibi_slopes():
    return 2.0 ** (-8.0 * np.arange(1, N_ALIBI_HEADS + 1) / N_ALIBI_HEADS)


_NAT_WIDTHS = (NSA_KV_COLS, NSA_KV_COLS, DIFF_QK_COLS, 2 * NSA_KV_COLS, 2 * D_MODEL)
_TR_ROWS = (NSA_Q_COLS, DIFF_QK_COLS, NSA_KV_COLS, NSA_KV_COLS, DIFF_V_COLS, 2 * GATE_ROWS_PER_GROUP)
_TR_SCALE = (Q_SCALE, Q_SCALE, 1.0, 1.0, 1.0, 1.0)
_KCVC_INDEX = 3


def _inproj_kernel(x_ref, g_ref, wn_ref, wt_ref, *refs):
    n_out = len(_NAT_WIDTHS) + len(_TR_ROWS)
    nat_refs = refs[:len(_NAT_WIDTHS)]
    tr_refs = refs[len(_NAT_WIDTHS):n_out]
    rows_k, rows_v = refs[n_out:]
    x = x_ref[...]
    xn = (x * lax.rsqrt(jnp.mean(x * x, axis=-1, keepdims=True) + RMS_EPS) * g_ref[...]).astype(BF16)
    off = 0
    for idx, (ref, width) in enumerate(zip(nat_refs, _NAT_WIDTHS)):
        if idx == _KCVC_INDEX:
            res = _dot(xn, wn_ref[:, off:off + width])
            rows_k[...] = res[:, :NSA_KV_COLS]
            rows_v[...] = res[:, NSA_KV_COLS:]
            n_rows = res.shape[0] // CMP_STRIDE
            for tok in range(CMP_STRIDE):
                lo = tok * width
                ref[:, lo:lo + NSA_KV_COLS] = rows_k[pl.ds(tok, n_rows, stride=CMP_STRIDE), :]
                ref[:, lo + NSA_KV_COLS:lo + width] = rows_v[pl.ds(tok, n_rows, stride=CMP_STRIDE), :]
        else:
            for c in range(0, width, 512):
                cw = min(512, width - c)
                ref[:, c:c + cw] = _dot(xn, wn_ref[:, off + c:off + c + cw]).astype(ref.dtype)
        off += width
    n_sub = x.shape[0] // CH
    off = 0
    for ref, rows, scale in zip(tr_refs, _TR_ROWS, _TR_SCALE):
        for c in range(0, rows, 256):
            rw = min(256, rows - c)
            res = lax.dot_general(wt_ref[off + c:off + c + rw, :], xn, _NT, preferred_element_type=F32)
            if scale != 1.0:
                res = res * scale
            for j in range(n_sub):
                ref[j, c:c + rw, :] = res[:, j * CH:(j + 1) * CH].astype(ref.dtype)
        off += rows


def _inproj(h2d, g, wn, wt, layer):
    n = h2d.shape[0]
    tm = ROW_TILE
    nat_dtypes = (BF16, BF16, BF16, F32, BF16)
    tr_dtypes = (BF16, BF16, BF16, BF16, BF16, F32)
    out_shape = [jax.ShapeDtypeStruct((n, w), dt) for w, dt in zip(_NAT_WIDTHS, nat_dtypes)]
    out_shape += [jax.ShapeDtypeStruct((n // CH, r, CH), dt) for r, dt in zip(_TR_ROWS, tr_dtypes)]
    out_specs = [pl.BlockSpec((tm, w), lambda i: (i, 0)) for w in _NAT_WIDTHS]
    out_specs += [pl.BlockSpec((tm // CH, r, CH), lambda i: (i, 0, 0)) for r in _TR_ROWS]
    hb_width = CMP_STRIDE * _NAT_WIDTHS[_KCVC_INDEX]
    out_shape[_KCVC_INDEX] = jax.ShapeDtypeStruct((n // CMP_STRIDE, hb_width), F32)
    out_specs[_KCVC_INDEX] = pl.BlockSpec((tm // CMP_STRIDE, hb_width), lambda i: (i, 0))
    return pl.pallas_call(
        _inproj_kernel,
        grid=(n // tm,),
        in_specs=[pl.BlockSpec((tm, D_MODEL), lambda i: (i, 0)),
                  _layer_spec(g, layer), _layer_spec(wn, layer), _layer_spec(wt, layer)],
        out_specs=out_specs,
        out_shape=out_shape,
        scratch_shapes=[pltpu.VMEM((tm, NSA_KV_COLS), F32), pltpu.VMEM((tm, NSA_KV_COLS), F32)],
        compiler_params=_params(("parallel",)),
        name="inproj",
    )(h2d, g, wn, wt)


def _compress_kernel(hb_ref, pe_ref, w1_ref, b1_ref, w2k_ref, b2k_ref, w2v_ref, b2v_ref, kc_ref, vct_ref):
    hb = hb_ref[0]
    rows = hb.shape[0]
    top = (hb + pe_ref[0]).astype(BF16)
    bot = (hb + pe_ref[1]).astype(BF16)
    p = _dot(top, w1_ref[0])
    q = _dot(bot, w1_ref[1])
    q_next = pltpu.roll(q, rows - 1, 0)
    hid = jax.nn.gelu(p + q_next + b1_ref[...])
    width = hid.shape[1] // 2
    kc_ref[0] = (_dot(hid[:, :width].astype(BF16), w2k_ref[...]) + b2k_ref[...]).astype(kc_ref.dtype)
    vct = lax.dot_general(w2v_ref[...], hid[:, width:].astype(BF16), _NT, preferred_element_type=F32)
    vct_ref[0] = (vct + b2v_ref[...]).astype(vct_ref.dtype)


def _compress(hb, pe_hb, w1_big, b1p, w2k, b2k, w2vt, b2v, layer):
    bsz, rows, width = hb.shape
    gk = NSA_KV_COLS
    return pl.pallas_call(
        _compress_kernel,
        grid=(bsz,),
        in_specs=[pl.BlockSpec((1, rows, width), lambda b: (b, 0, 0)),
                  *[_layer_spec(a, layer) for a in (pe_hb, w1_big, b1p, w2k, b2k, w2vt, b2v)]],
        out_specs=[pl.BlockSpec((1, rows, gk), lambda b: (b, 0, 0)),
                   pl.BlockSpec((1, gk, rows), lambda b: (b, 0, 0))],
        out_shape=[jax.ShapeDtypeStruct((bsz, rows, gk), BF16),
                   jax.ShapeDtypeStruct((bsz, gk, rows), BF16)],
        compiler_params=_params(("parallel",)),
        name="compress",
    )(hb, pe_hb, w1_big, b1p, w2k, b2k, w2vt, b2v)


def _bf16_pieces(x):
    out = []
    rest = np.asarray(x, np.float32)
    for _ in range(SLOPE_PIECES):
        piece = rest.astype(BF16).astype(np.float32)
        out.append(piece)
        rest = rest - piece
    return out


def _slope_rows(slopes, heads_per_block, cols_per_head):
    sl2 = (np.asarray(slopes, np.float32).astype(np.float64) * LOG2E).astype(np.float32)
    pieces = np.stack(_bf16_pieces(sl2) * 2, axis=0)
    rows = np.zeros((ALIBI_ROWS, sl2.shape[0]), np.float32)
    rows[:pieces.shape[0]] = pieces
    rows[PAD_ROW] = -MASK_BIG
    rows = np.repeat(rows, cols_per_head, axis=1)
    rows = rows.reshape(ALIBI_ROWS, -1, heads_per_block * cols_per_head).transpose(1, 0, 2)
    return jnp.asarray(rows, BF16)


def _key_aug_tables(seq, nb):
    pos = np.arange(seq)
    aug = np.zeros((2, seq + CH, AUG_LANES), np.float32)
    aug[:, :seq, 0:SLOPE_PIECES] = (pos % CH)[None, :, None]
    aug[:, :seq, SLOPE_PIECES:2 * SLOPE_PIECES] = (pos // CH * CH)[None, :, None]
    aug[0, pos, MASK_ROW0 + pos // SEL_BLOCK] = 1.0
    aug[:, seq:, PAD_ROW] = 1.0
    return jnp.asarray(aug.reshape(2, seq // CH + 1, CH, AUG_LANES), BF16)


def _cmp_aug_table(n_rows):
    aug = np.zeros((n_rows, AUG_LANES), np.float32)
    aug[:, 0:SLOPE_PIECES] = (np.arange(n_rows) * CMP_STRIDE)[:, None]
    aug[:, SLOPE_PIECES:2 * SLOPE_PIECES] = CMP_BLOCK - 1
    return jnp.asarray(aug, BF16)


def _tile_lanes(x, reps):
    return jnp.concatenate([x] * reps, axis=1)


def _query_minus_key(reps):
    shape = (CH, reps * CH)
    q_off = jnp.bitwise_and(lax.broadcasted_iota(jnp.int32, shape, 1), CH - 1)
    return q_off - lax.broadcasted_iota(jnp.int32, shape, 0)


def _flash_init(m_ref, acc_ref):
    m_ref[...] = jnp.full(m_ref.shape, NEG_INF, F32)
    acc_ref[...] = jnp.zeros(acc_ref.shape, F32)


def _normalized(acc_ref, g):
    dv = acc_ref.shape[1] - SUM_ROWS
    return acc_ref[g, 0:dv, :] / acc_ref[g, dv:dv + 1, :]


def _chunk_scores(k_blk, aug_blk, qa_ref):
    return _dot(jnp.concatenate([k_blk, aug_blk], axis=1), qa_ref[...])


def _stage_and_consume(prod, cons, qa_ref, block_cols=BLOCK_COLS):
    if prod is not None:
        k_blk, aug_blk, (ps_ref, pmx_ref), pmask = prod
        k_full = jnp.concatenate([k_blk, aug_blk], axis=1)
    if cons is not None:
        (cs_ref, cmx_ref), v_t, (m_ref, acc_ref), cmask = cons
        _, rows, gcols = acc_ref.shape
        dv = rows - SUM_ROWS
        ones = jnp.ones((SUM_ROWS, v_t.shape[1]), BF16)
    for c0 in range(0, qa_ref.shape[1], block_cols):
        csl = slice(c0, c0 + block_cols)
        if prod is not None:
            s = _dot(k_full, qa_ref[:, csl])
            if pmask is not None:
                s = jnp.where(pmask(c0), s, NEG_INF)
            ps_ref[:, csl] = s
            pmx_ref[:, csl] = jnp.max(s, axis=0, keepdims=True)
        if cons is not None:
            s = cs_ref[:, csl]
            if cmask is None:
                mx = cmx_ref[:, csl]
            else:
                s = jnp.where(cmask(c0), s, NEG_INF)
                mx = jnp.max(s, axis=0, keepdims=True)
            m_prev = m_ref[:, csl]
            m_new = jnp.maximum(m_prev, mx)
            alpha = jnp.exp2(m_prev - m_new)
            p = jnp.exp2(s - m_new).astype(BF16)
            g = c0 // gcols
            gsl = slice(c0 - g * gcols, c0 - g * gcols + block_cols)
            v_ones = jnp.concatenate([v_t[g * dv:(g + 1) * dv], ones], axis=0)
            acc_ref[g, :, gsl] = alpha * acc_ref[g, :, gsl] + _dot(v_ones, p)
            m_ref[:, csl] = m_new


def _consume(buf, v_t, state, qa_ref, mask=None, block_cols=BLOCK_COLS):
    _stage_and_consume(None, (buf, v_t, state, mask), qa_ref, block_cols)


def _pad_or(aug_ref, c, is_pad):
    return aug_ref[jnp.where(is_pad, aug_ref.shape[0] - 1, c)]


def _identity(pos):
    return pos


def _causal_first(n, k_ref, aug_ref, buf0, chunk_at=_identity):
    c = chunk_at(0)
    return (k_ref[c], _pad_or(aug_ref, c, jnp.bitwise_and(n, 1) == 1), buf0, None)


def _causal_pairs(n, k_ref, v_ref, aug_ref, qa_ref, bufs, state, block_cols=BLOCK_COLS, chunk_at=_identity):
    _causal_pairs_multi(n, [(k_ref, v_ref, aug_ref, qa_ref, bufs, state)], block_cols, chunk_at)


def _causal_pairs_multi(n, streams, block_cols=BLOCK_COLS, chunk_at=_identity):
    pad = jnp.bitwise_and(n, 1)

    def pair(k, carry):
        pos = 2 * k - pad
        c0 = chunk_at(jnp.maximum(pos, 0))
        c1 = chunk_at(pos + 1)
        c2 = chunk_at(pos + 2)
        for k_ref, v_ref, aug_ref, qa_ref, (buf0, buf1), state in streams:
            _stage_and_consume((k_ref[c1], aug_ref[c1], buf1, None), (buf0, v_ref[c0], state, None),
                               qa_ref, block_cols)
        for k_ref, v_ref, aug_ref, qa_ref, (buf0, buf1), state in streams:
            _stage_and_consume((k_ref[c2], aug_ref[c2], buf0, None), (buf1, v_ref[c1], state, None),
                               qa_ref, block_cols)
        return carry
    lax.fori_loop(0, jnp.right_shift(n + pad, 1), pair, 0)


def _unselected_mask_rows(imp, k_sel):
    n_blk, width = imp.shape
    rows_per = 8
    j_loc = lax.broadcasted_iota(jnp.int32, (rows_per, width), 0)
    mask_blocks = []
    for r0 in range(0, n_blk, rows_per):
        blk = imp[r0:r0 + rows_per, :]
        cnt = jnp.zeros((rows_per, width), jnp.int32)
        for jp in range(n_blk):
            row = imp[jp:jp + 1, :]
            gt = jnp.where(row > blk, 1, 0)
            ge = jnp.where(row >= blk, 1, 0)
            if jp >= r0 + rows_per - 1:
                cnt = cnt + gt
            elif jp < r0:
                cnt = cnt + ge
            else:
                cnt = cnt + jnp.where(j_loc + r0 > jp, ge, gt)
        mask_blocks.append(jnp.where(cnt < k_sel, 0.0, -MASK_BIG))
    return jnp.concatenate(mask_blocks, axis=0)


def _nsa_kernel(q_ref, srow_ref, ks_ref, kw_ref, vs_ref, vw_ref, kaug_ref, kc_ref, kcaug_ref, vc_ref,
                ng_ref, asel_ref, o_ref, qa_ref, ocmp_ref, s0, x0, s1, x1, w0, y0, w1, y1, w2, y2,
                m_s, acc_s, m_w, acc_w, chunk_list):
    i = pl.program_id(1)
    t0 = i * CH
    nh = NSA_GROUP_SIZE
    ng = NSA_KV_GROUPS
    gcols = nh * CH
    cols = ng * gcols
    n_cmp = kc_ref.shape[1]
    n_blk = asel_ref.shape[0]
    k_sel = min(SEL_TOPK, n_blk)
    bufs_s = ((s0, x0), (s1, x1))
    bufs_w = ((w0, y0), (w1, y1), (w2, y2))
    state_s = (m_s, acc_s)
    state_w = (m_w, acc_w)
    kaug_s = kaug_ref.at[0]
    kaug_w = kaug_ref.at[1]

    t_pos = t0 + lax.broadcasted_iota(jnp.int32, (1, CH), 1)
    d0_i = _query_minus_key(BLOCK_COLS // CH)
    causal = lambda c0: d0_i >= 0
    window_edge = lambda c0: d0_i < 0

    zeros_q = jnp.zeros((HEAD_DIM, CH), BF16)
    for g in range(ng):
        for hh in range(nh):
            h = g * nh + hh
            qh = q_ref[0, h * HEAD_DIM:(h + 1) * HEAD_DIM, :]
            for gg in range(ng):
                qa_ref[gg * HEAD_DIM:(gg + 1) * HEAD_DIM, h * CH:(h + 1) * CH] = qh if gg == g else zeros_q
    qa_ref[QK_LANES:QK_LANES + ALIBI_ROWS, :] = srow_ref[...]
    qa_ref[QK_LANES + MASK_ROW0:, :] = jnp.zeros((AUG_LANES - MASK_ROW0, cols), BF16)

    cmp_scores = _chunk_scores(kc_ref[0], kcaug_ref[...], qa_ref)

    n_back = WINDOW // CH
    stage_w = []
    chunks_w = []
    for back in range(n_back, 0, -1):
        c = jnp.maximum(i - back, 0)
        stage_w.append((kw_ref[c], _pad_or(kaug_w, c, i < back), bufs_w[n_back - back],
                        window_edge if back == n_back else None))
        chunks_w.append(c)
    stage_w.append((kw_ref[i], kaug_w[i], bufs_w[n_back], causal))
    chunks_w.append(i)
    _stage_and_consume(stage_w[0], None, qa_ref)

    n_idx = lax.broadcasted_iota(jnp.int32, (n_cmp, cols), 0)
    t_pos_all = t0 + jnp.bitwise_and(lax.broadcasted_iota(jnp.int32, (1, cols), 1), CH - 1)
    valid_c = n_idx * CMP_STRIDE + (CMP_BLOCK - 1) <= t_pos_all
    lg = jnp.where(valid_c, cmp_scores, NEG_INF)
    m = jnp.max(lg, axis=0, keepdims=True)
    p = jnp.where(valid_c, jnp.exp2(lg - m), 0.0)
    l = jnp.sum(p, axis=0, keepdims=True)
    pc = p * jnp.where(l > 0.0, 1.0 / l, 0.0)
    pc_b = pc.astype(BF16)
    vc = vc_ref[0]

    j_idx = lax.broadcasted_iota(jnp.int32, (n_blk, CH), 0)
    cur = jnp.right_shift(t_pos, SEL_BLOCK.bit_length() - 1)
    forced = (j_idx == 0) | (j_idx == cur) | (j_idx == cur - 1)
    in_past = j_idx * SEL_BLOCK <= t_pos
    for g in range(ng):
        gsl = slice(g * gcols, (g + 1) * gcols)
        ocmp_ref[g] = _dot(vc[g * HEAD_DIM:(g + 1) * HEAD_DIM], pc_b[:, gsl])
        psum = pc[:, g * gcols:g * gcols + CH]
        for hh in range(1, nh):
            psum = psum + pc[:, g * gcols + hh * CH:g * gcols + (hh + 1) * CH]
        imp = jnp.dot(asel_ref[...], psum, precision=lax.Precision.HIGHEST, preferred_element_type=F32)
        imp = jnp.where(in_past, jnp.where(forced, FORCED_SCORE, imp), -1.0)
        mask_rows = _unselected_mask_rows(imp, k_sel)
        kept = mask_rows if g == 0 else jnp.maximum(kept, mask_rows)
        qa_ref[QK_LANES + MASK_ROW0:QK_LANES + MASK_ROW0 + n_blk, gsl] = _tile_lanes(mask_rows.astype(BF16), nh)

    blocks_per_chunk = CH // SEL_BLOCK
    n_sel = jnp.int32(0)
    for c in range(n_blk // blocks_per_chunk):
        hit = jnp.max(kept[c * blocks_per_chunk:(c + 1) * blocks_per_chunk, :]) > -1.0
        chunk_list[n_sel] = c
        n_sel = n_sel + jnp.logical_and(hit, c < i).astype(jnp.int32)
    chunk_list[n_sel] = i

    def chunk_at(pos):
        return chunk_list[pos]

    _flash_init(*state_w)
    for j in range(1, n_back + 1):
        _stage_and_consume(stage_w[j], (bufs_w[j - 1], vw_ref[chunks_w[j - 1]], state_w, None), qa_ref)
    _stage_and_consume(_causal_first(n_sel, ks_ref, kaug_s, bufs_s[0], chunk_at),
                       (bufs_w[n_back], vw_ref[i], state_w, None), qa_ref)
    _flash_init(*state_s)
    _causal_pairs(n_sel, ks_ref, vs_ref, kaug_s, qa_ref, bufs_s, state_s, chunk_at=chunk_at)
    _consume(bufs_s[0], vs_ref[i], state_s, qa_ref, mask=causal)

    for g in range(ng):
        def gate(branch):
            r0 = g * GATE_ROWS_PER_GROUP + branch * nh
            return jax.nn.sigmoid(jnp.concatenate([ng_ref[0, r0 + hh:r0 + hh + 1, :] for hh in range(nh)], axis=1))
        out = gate(0) * ocmp_ref[g] + gate(1) * _normalized(acc_s, g) + gate(2) * _normalized(acc_w, g)
        for hh in range(nh):
            h = g * nh + hh
            o_ref[0, h * HEAD_DIM:(h + 1) * HEAD_DIM, :] = out[:, hh * CH:(hh + 1) * CH].astype(o_ref.dtype)


def _nsa(nq_t, srows, ks3, kw3, vs_t, vw_t, kaug, kc, kcaug, vc_t, ng_t, asel_t, bsz, nq):
    n_cmp = kc.shape[1]
    ng = NSA_KV_GROUPS
    gcols = NSA_GROUP_SIZE * CH
    cols = ng * gcols
    acc_shape = (ng, HEAD_DIM + SUM_ROWS, gcols)
    stage = [pltpu.VMEM((CH, cols), F32), pltpu.VMEM((1, cols), F32)]
    return pl.pallas_call(
        _nsa_kernel,
        grid=(bsz, nq),
        in_specs=[
            pl.BlockSpec((1, NSA_Q_COLS, CH), lambda b, i: (b * nq + i, 0, 0)),
            pl.BlockSpec(srows.shape, lambda b, i: (0, 0)),
            pl.BlockSpec((nq, CH, NSA_KV_COLS), lambda b, i: (b, 0, 0)),
            pl.BlockSpec((nq, CH, NSA_KV_COLS), lambda b, i: (b, 0, 0)),
            pl.BlockSpec((nq, NSA_KV_COLS, CH), lambda b, i: (b, 0, 0)),
            pl.BlockSpec((nq, NSA_KV_COLS, CH), lambda b, i: (b, 0, 0)),
            pl.BlockSpec(kaug.shape, lambda b, i: (0, 0, 0, 0)),
            pl.BlockSpec((1, n_cmp, NSA_KV_COLS), lambda b, i: (b, 0, 0)),
            pl.BlockSpec(kcaug.shape, lambda b, i: (0, 0)),
            pl.BlockSpec((1, NSA_KV_COLS, n_cmp), lambda b, i: (b, 0, 0)),
            pl.BlockSpec((1, ng * GATE_ROWS_PER_GROUP, CH), lambda b, i: (b * nq + i, 0, 0)),
            pl.BlockSpec(asel_t.shape, lambda b, i: (0, 0)),
        ],
        out_specs=pl.BlockSpec((1, NSA_Q_COLS, CH), lambda b, i: (b * nq + i, 0, 0)),
        out_shape=jax.ShapeDtypeStruct((bsz * nq, NSA_Q_COLS, CH), BF16),
        scratch_shapes=(
            [pltpu.VMEM((QK_LANES + AUG_LANES, cols), BF16), pltpu.VMEM((ng, HEAD_DIM, gcols), F32)]
            + stage * 5
            + [pltpu.VMEM((1, cols), F32), pltpu.VMEM(acc_shape, F32),
               pltpu.VMEM((1, cols), F32), pltpu.VMEM(acc_shape, F32),
               pltpu.SMEM((nq + 1,), jnp.int32)]),
        compiler_params=_params(("parallel", "arbitrary")),
        name="nsa_attention",
    )(nq_t, srows, ks3, kw3, vs_t, vw_t, kaug, kc, kcaug, vc_t, ng_t, asel_t)


def _diff_kernel(scal_ref, q_ref, srow_ref, k_ref, v_ref, kaug_ref, lam_ref, gain_ref, o_ref, *scratch):
    nt = DIFF_TILE_CHUNKS
    tq = nt * CH
    hd2 = 2 * HEAD_DIM
    c0 = pl.program_id(2) * nt
    lam_init = scal_ref[0]
    out_scale = scal_ref[1]
    sub2 = lax.broadcasted_iota(jnp.int32, (QK_LANES, CH), 0)
    zero = jnp.zeros((), BF16)
    bc = 2 * tq
    lane = lax.broadcasted_iota(jnp.int32, (CH, bc), 1)
    sub = lax.broadcasted_iota(jnp.int32, (CH, bc), 0)

    def on_or_after(first_key):
        return lambda col0: jnp.bitwise_and(lane + col0, tq - 1) - sub >= first_key

    streams = []
    for hd in range(DIFF_HEADS_PER_STEP):
        qa_ref, s0, x0, s1, x1, m_r, acc_r = scratch[7 * hd:7 * (hd + 1)]
        rows = slice(hd * hd2, (hd + 1) * hd2)
        for j in range(nt):
            q = q_ref[j, rows, :]
            qa_ref[0:QK_LANES, j * CH:(j + 1) * CH] = jnp.where(sub2 < HEAD_DIM, q, zero)
            qa_ref[0:QK_LANES, tq + j * CH:tq + (j + 1) * CH] = jnp.where(sub2 >= HEAD_DIM, q, zero)
        qa_ref[QK_LANES:QK_LANES + ALIBI_ROWS, :] = srow_ref[hd]
        qa_ref[QK_LANES + MASK_ROW0:, :] = jnp.zeros((AUG_LANES - MASK_ROW0, bc), BF16)
        streams.append((k_ref.at[:, :, rows], v_ref.at[:, rows, :], kaug_ref, qa_ref,
                        ((s0, x0), (s1, x1)), (m_r, acc_r)))

    for k_at, v_at, aug, qa_ref, bufs, state in streams:
        _stage_and_consume(_causal_first(c0, k_at, aug, bufs[0]), None, qa_ref, bc)
        _flash_init(*state)
    _causal_pairs_multi(c0, streams, bc)
    for k_at, v_at, aug, qa_ref, bufs, state in streams:
        _stage_and_consume((k_at[c0 + 1], aug[c0 + 1], bufs[1], on_or_after(CH)),
                           (bufs[0], v_at[c0], state, on_or_after(0)), qa_ref, bc)
    for k_at, v_at, aug, qa_ref, bufs, state in streams:
        _consume(bufs[1], v_at[c0 + 1], state, qa_ref, block_cols=bc)

    lp = lam_ref[...]
    lam = (jnp.exp(jnp.sum(lp[0:1] * lp[1:2], axis=1, keepdims=True))
           - jnp.exp(jnp.sum(lp[2:3] * lp[3:4], axis=1, keepdims=True)) + lam_init)
    for hd, stream in enumerate(streams):
        att = _normalized(stream[5][1], 0)
        o = att[:, 0:tq] - lam * att[:, tq:2 * tq]
        o = o * lax.rsqrt(jnp.mean(o * o, axis=0, keepdims=True) + SUBLN_EPS) * gain_ref[...]
        o = (o * out_scale).astype(o_ref.dtype)
        for j in range(nt):
            o_ref[j, hd * hd2:(hd + 1) * hd2, :] = o[:, j * CH:(j + 1) * CH]


def _diff(scal, dq_t, srows, dk3, dv_t, kaug_plain, lam_p, gain, bsz, nq):
    hd2 = 2 * HEAD_DIM
    nt = DIFF_TILE_CHUNKS
    n_tiles = nq // nt
    cols = 2 * nt * CH
    hps = DIFF_HEADS_PER_STEP
    rows = hps * hd2
    per_head = [
        pltpu.VMEM((QK_LANES + AUG_LANES, cols), BF16),
        pltpu.VMEM((CH, cols), F32), pltpu.VMEM((1, cols), F32),
        pltpu.VMEM((CH, cols), F32), pltpu.VMEM((1, cols), F32),
        pltpu.VMEM((1, cols), F32), pltpu.VMEM((1, hd2 + SUM_ROWS, cols), F32),
    ]
    grid_spec = pltpu.PrefetchScalarGridSpec(
        num_scalar_prefetch=1,
        grid=(bsz, DIFF_HEADS // hps, n_tiles),
        in_specs=[
            pl.BlockSpec((nt, rows, CH), lambda b, h, i, s: (b * n_tiles + i, h, 0)),
            pl.BlockSpec((hps, ALIBI_ROWS, cols), lambda b, h, i, s: (h, 0, 0)),
            pl.BlockSpec((nq, CH, rows), lambda b, h, i, s: (b, 0, h)),
            pl.BlockSpec((nq, rows, CH), lambda b, h, i, s: (b, h, 0)),
            pl.BlockSpec(kaug_plain.shape, lambda b, h, i, s: (0, 0, 0)),
            pl.BlockSpec(lam_p.shape, lambda b, h, i, s: (0, 0)),
            pl.BlockSpec(gain.shape, lambda b, h, i, s: (0, 0)),
        ],
        out_specs=pl.BlockSpec((nt, rows, CH), lambda b, h, i, s: (b * n_tiles + i, h, 0)),
        scratch_shapes=per_head * hps,
    )
    return pl.pallas_call(
        _diff_kernel,
        grid_spec=grid_spec,
        out_shape=jax.ShapeDtypeStruct((bsz * nq, DIFF_V_COLS, CH), BF16),
        compiler_params=_params(("parallel", "parallel", "arbitrary")),
        name="diff_attention",
    )(scal, dq_t, srows, dk3, dv_t, kaug_plain, lam_p, gain)


def _merge_kernel(ya_ref, yb_ref, gates_ref, h_ref, wa_ref, wb_ref, wo_ref, o_ref):
    for j in range(ya_ref.shape[0]):
        rows = slice(j * CH, (j + 1) * CH)
        a = lax.dot_general(ya_ref[j], wa_ref[...], _TN, preferred_element_type=F32)
        b = lax.dot_general(yb_ref[j], wb_ref[...], _TN, preferred_element_type=F32)
        ga = jax.nn.sigmoid(gates_ref[rows, :D_MODEL].astype(F32))
        gb = jax.nn.sigmoid(gates_ref[rows, D_MODEL:].astype(F32))
        merged = (ga * a + gb * b).astype(BF16)
        o_ref[rows, :] = h_ref[rows, :] + _dot(merged, wo_ref[...])


def _merge(ya_t, yb_t, gates, h2d, wa, wb, wo, layer):
    n = h2d.shape[0]
    tm = ROW_TILE
    return pl.pallas_call(
        _merge_kernel,
        grid=(n // tm,),
        in_specs=[pl.BlockSpec((tm // CH, NSA_Q_COLS, CH), lambda i: (i, 0, 0)),
                  pl.BlockSpec((tm // CH, DIFF_V_COLS, CH), lambda i: (i, 0, 0)),
                  pl.BlockSpec((tm, 2 * D_MODEL), lambda i: (i, 0)),
                  pl.BlockSpec((tm, D_MODEL), lambda i: (i, 0)),
                  _layer_spec(wa, layer), _layer_spec(wb, layer), _layer_spec(wo, layer)],
        out_specs=pl.BlockSpec((tm, D_MODEL), lambda i: (i, 0)),
        out_shape=jax.ShapeDtypeStruct((n, D_MODEL), F32),
        compiler_params=_params(("parallel",)),
        name="merge_outproj",
    )(ya_t, yb_t, gates, h2d, wa, wb, wo)


def _first_argmax(x, rows, n):
    mx = jnp.max(x, axis=0, keepdims=True)
    idx = jnp.min(jnp.where(x == mx, rows, n), axis=0, keepdims=True)
    return mx, idx


def _moe_kernel(h_ref, g_ref, wr_ref, br_ref, eexp_ref, wg_ref, wu_ref, wd_ref, fg_ref, o_ref, *, final):
    hres = h_ref[...]
    xf = hres * lax.rsqrt(jnp.mean(hres * hres, axis=-1, keepdims=True) + RMS_EPS) * g_ref[...]
    xb = xf.astype(BF16)
    tm = hres.shape[0]

    x_lo = (xf - xb.astype(F32)).astype(BF16)
    n_r = wr_ref.shape[0] // 2
    part = lax.dot_general(wr_ref[...], xb, _NT, preferred_element_type=F32)
    logits = (part[0:n_r] + part[n_r:2 * n_r]
              + lax.dot_general(wr_ref[0:n_r, :], x_lo, _NT, preferred_element_type=F32)
              + br_ref[...])
    gl = logits[0:MOE_GROUPS]
    rows_g = lax.broadcasted_iota(jnp.int32, (MOE_GROUPS, tm), 0)
    gmax, gidx = _first_argmax(gl, rows_g, MOE_GROUPS)
    g_w = 1.0 / jnp.sum(jnp.exp(gl - gmax), axis=0, keepdims=True)
    esel = jnp.zeros((EXPERTS_PER_GROUP, tm), F32)
    for gg in range(MOE_GROUPS):
        lo = MOE_GROUPS + gg * EXPERTS_PER_GROUP
        esel = jnp.where(gidx == gg, logits[lo:lo + EXPERTS_PER_GROUP], esel)
    rows_e = lax.broadcasted_iota(jnp.int32, (EXPERTS_PER_GROUP, tm), 0)
    v1, i1 = _first_argmax(esel, rows_e, EXPERTS_PER_GROUP)
    rest = jnp.where(rows_e == i1, -jnp.inf, esel)
    v2, i2 = _first_argmax(rest, rows_e, EXPERTS_PER_GROUP)
    e21 = jnp.exp(v2 - v1)
    w1 = g_w / (1.0 + e21)
    w2 = g_w * e21 / (1.0 + e21)
    rows_c = lax.broadcasted_iota(jnp.int32, (N_EXPERTS, tm), 0)
    grp_c = jnp.right_shift(rows_c, EXPERTS_PER_GROUP.bit_length() - 1)
    exp_c = jnp.bitwise_and(rows_c, EXPERTS_PER_GROUP - 1)
    comb = jnp.where(grp_c == gidx,
                     jnp.where(exp_c == i1, w1, 0.0) + jnp.where(exp_c == i2, w2, 0.0), 0.0)
    comb_hi = comb.astype(BF16)
    comb_lo = (comb - comb_hi.astype(F32)).astype(BF16)

    comb_nat = (lax.dot_general(comb_hi, eexp_ref[...], _TN, preferred_element_type=F32)
                + lax.dot_general(comb_lo, eexp_ref[...], _TN, preferred_element_type=F32))

    acc = hres
    n_ff = wg_ref.shape[1]
    step = 512
    for c in range(0, n_ff, step):
        cols = slice(c, c + step)
        hg = _dot(xb, wg_ref[:, cols])
        hu = _dot(xb, wu_ref[:, cols])
        act = jax.nn.silu(hg) * hu
        parts = []
        for e0 in range(0, step, EXPERT_FF):
            e = (c + e0) // EXPERT_FF
            parts.append((act[:, e0:e0 + EXPERT_FF] * comb_nat[:, e:e + 1]).astype(BF16))
        acc = acc + _dot(jnp.concatenate(parts, axis=1), wd_ref[cols, :])
    if final:
        acc = acc * lax.rsqrt(jnp.mean(acc * acc, axis=-1, keepdims=True) + RMS_EPS) * fg_ref[...]
    o_ref[...] = acc


def _moe(h2d, g, wr_t, br, eexp, wg, wu, wd, fg, layer, final):
    n = h2d.shape[0]
    tm = ROW_TILE
    return pl.pallas_call(
        functools.partial(_moe_kernel, final=final),
        grid=(n // tm,),
        in_specs=[pl.BlockSpec((tm, D_MODEL), lambda i: (i, 0)),
                  _layer_spec(g, layer), _layer_spec(wr_t, layer), _layer_spec(br, layer),
                  _const_spec(eexp.shape), _layer_spec(wg, layer), _layer_spec(wu, layer),
                  _layer_spec(wd, layer), _const_spec(fg.shape)],
        out_specs=pl.BlockSpec((tm, D_MODEL), lambda i: (i, 0)),
        out_shape=jax.ShapeDtypeStruct((n, D_MODEL), F32),
        compiler_params=_params(("parallel",)),
        name="moe_final" if final else "moe",
    )(h2d, g, wr_t, br, eexp, wg, wu, wd, fg)


def _split_points():
    sizes = ([NSA_Q_COLS] + [NSA_KV_COLS] * 6
             + [NSA_GATE_COLS, DIFF_QK_COLS, DIFF_QK_COLS, DIFF_V_COLS, D_MODEL, D_MODEL])
    return [int(v) for v in np.cumsum(sizes)[:-1]]


def _cmp_to_sel_t(n_rows, nc, nb):
    c0 = np.arange(nc)[:, None] * CMP_STRIDE
    s0 = np.arange(nb)[None, :] * SEL_BLOCK
    ov = np.maximum(0, np.minimum(c0 + CMP_BLOCK, s0 + SEL_BLOCK) - np.maximum(c0, s0)) / CMP_BLOCK
    out = np.zeros((nb, n_rows), np.float32)
    out[:, :nc] = ov.T
    return out


def kernel(x, norm1_g, w_in, cmp_pe, cmp_w1, cmp_b1, cmp_w2, cmp_b2, diff_lambda, diff_subln_g, w_branch_a, w_branch_b, w_out, norm2_g, router_grp_w, router_grp_b, router_exp_w, router_exp_b, exp_w_gate, exp_w_up, exp_w_down, final_norm_g):
    bsz, seq, d = x.shape
    depth = w_in.shape[0]
    n = bsz * seq
    nq = seq // CH
    n_half = seq // CMP_STRIDE
    nc = (seq - CMP_BLOCK) // CMP_STRIDE + 1
    nb = seq // SEL_BLOCK
    assert d == D_MODEL and seq % ROW_TILE == 0 and seq >= WINDOW and WINDOW % CH == 0
    assert nb % 16 == 0 and MASK_ROW0 + nb <= AUG_LANES and n_half <= 256
    g_kv = NSA_KV_GROUPS
    eye_g = jnp.eye(g_kv, dtype=F32)

    (nq_w, kc_w, vc_w, ks_w, vs_w, kw_w, vw_w, ng_w, dq_w, dk_w, dv_w, ga_w, gb_w) = jnp.split(
        w_in, _split_points(), axis=-1)
    ng_w = ng_w.reshape(depth, d, g_kv, NSA_GROUP_SIZE, 3).transpose(0, 1, 2, 4, 3)
    ng_w = ng_w.reshape(depth, d, g_kv, 3 * NSA_GROUP_SIZE)
    ng_w = jnp.pad(ng_w, ((0, 0), (0, 0), (0, 0), (0, GATE_ROWS_PER_GROUP - 3 * NSA_GROUP_SIZE)))
    ng_w = ng_w.reshape(depth, d, g_kv * GATE_ROWS_PER_GROUP)
    wn_all = jnp.concatenate([ks_w, kw_w, dk_w, kc_w, vc_w, ga_w, gb_w], axis=-1).astype(BF16)
    wt_all = jnp.concatenate([nq_w, dq_w, vs_w, vw_w, dv_w, ng_w], axis=-1)
    wt_all = jnp.swapaxes(wt_all, 1, 2).astype(BF16)

    w1r = cmp_w1.reshape(depth, 2, 2, CMP_STRIDE, HEAD_DIM, CMP_HIDDEN).astype(BF16)
    w1_big = jnp.zeros((depth, 2, CMP_STRIDE, 2 * NSA_KV_COLS, 2 * g_kv * CMP_HIDDEN), BF16)
    for c in range(2):
        for g in range(g_kv):
            r0 = c * NSA_KV_COLS + g * HEAD_DIM
            c0 = (c * g_kv + g) * CMP_HIDDEN
            w1_big = w1_big.at[:, :, :, r0:r0 + HEAD_DIM, c0:c0 + CMP_HIDDEN].set(w1r[:, c])
    w1_big = w1_big.reshape(depth, 2, CMP_STRIDE * 2 * NSA_KV_COLS, 2 * g_kv * CMP_HIDDEN)
    per = cmp_pe.reshape(depth, 2, 2, CMP_STRIDE, HEAD_DIM)
    pe_hb = jnp.einsum('Lchld,g->Lhlcgd', per, jnp.ones((g_kv,), F32))
    pe_hb = pe_hb.reshape(depth, 2, 1, CMP_STRIDE * 2 * NSA_KV_COLS)
    b1p = jnp.broadcast_to(cmp_b1[:, :, None, :], (depth, 2, g_kv, CMP_HIDDEN)).reshape(depth, 1, -1)
    w2k = jnp.einsum('Lfd,gG->LgfGd', cmp_w2[:, 0], eye_g).reshape(depth, g_kv * CMP_HIDDEN, NSA_KV_COLS)
    w2vt = jnp.einsum('Lfd,gG->LGdgf', cmp_w2[:, 1], eye_g).reshape(depth, NSA_KV_COLS, g_kv * CMP_HIDDEN)
    w2k = w2k.astype(BF16)
    w2vt = w2vt.astype(BF16)
    b2k = jnp.tile(cmp_b2[:, 0], (1, g_kv))[:, None, :]
    b2v = jnp.tile(cmp_b2[:, 1], (1, g_kv))[:, :, None]

    slopes = _alibi_slopes()
    nsa_srows = _slope_rows(slopes[:NSA_HEADS], NSA_HEADS, CH)[0]
    diff_srows = _slope_rows(np.repeat(slopes[NSA_HEADS:], 2), 2, DIFF_TILE_CHUNKS * CH)
    kaug = _key_aug_tables(seq, nb)
    kcaug = _cmp_aug_table(n_half)
    asel_t = jnp.asarray(_cmp_to_sel_t(n_half, nc, nb))

    wa_all = w_branch_a.astype(BF16)
    wb_all = w_branch_b.astype(BF16)
    wo_all = w_out.astype(BF16)

    wr = jnp.concatenate([router_grp_w, router_exp_w.reshape(depth, d, N_EXPERTS)], axis=-1)
    n_r = MOE_GROUPS + N_EXPERTS
    wr_t = jnp.pad(jnp.swapaxes(wr, 1, 2), ((0, 0), (0, 32 - n_r), (0, 0)))
    wr_hi = wr_t.astype(BF16)
    wr_t = jnp.concatenate([wr_hi, (wr_t - wr_hi.astype(F32)).astype(BF16)], axis=1)
    br = jnp.concatenate([router_grp_b, router_exp_b.reshape(depth, N_EXPERTS)], axis=-1)
    br = jnp.pad(br, ((0, 0), (0, 32 - n_r)))[:, :, None]
    eexp = jnp.asarray(np.eye(N_EXPERTS, 128, dtype=np.float32), BF16)
    wg_all = jnp.swapaxes(exp_w_gate, 1, 2).reshape(depth, d, N_EXPERTS * EXPERT_FF).astype(BF16)
    wu_all = jnp.swapaxes(exp_w_up, 1, 2).reshape(depth, d, N_EXPERTS * EXPERT_FF).astype(BF16)
    wd_all = exp_w_down.reshape(depth, N_EXPERTS * EXPERT_FF, d).astype(BF16)

    h = x.reshape(n, d)
    for l in range(depth):
        (ks, kw, dk, kcvc_hb, gates, nq_t, dq_t, vs_t, vw_t, dv_t, ng_t) = _inproj(
            h, norm1_g[:, None, :], wn_all, wt_all, l)
        hb = kcvc_hb.reshape(bsz, n_half, CMP_STRIDE * 2 * NSA_KV_COLS)
        kc, vc_t = _compress(hb, pe_hb, w1_big, b1p, w2k, b2k, w2vt, b2v, l)
        ya_t = _nsa(nq_t, nsa_srows, ks.reshape(n // CH, CH, NSA_KV_COLS), kw.reshape(n // CH, CH, NSA_KV_COLS),
                    vs_t, vw_t, kaug, kc, kcaug, vc_t, ng_t, asel_t, bsz, nq)
        lam_init = 0.8 - 0.6 * float(np.exp(-0.3 * l))
        scal = jnp.asarray([lam_init, 1.0 - lam_init], F32)
        yb_t = _diff(scal, dq_t, diff_srows, dk.reshape(n // CH, CH, DIFF_QK_COLS), dv_t, kaug[1],
                     diff_lambda[l], diff_subln_g[l][:, None], bsz, nq)
        h = _merge(ya_t, yb_t, gates, h, wa_all, wb_all, wo_all, l)
        h = _moe(h, norm2_g[:, None, :], wr_t, br, eexp, wg_all, wu_all, wd_all,
                 final_norm_g[None, :], l, final=(l == depth - 1))
    return h.reshape(bsz, seq, d)
```

```python
import functools

import numpy as np
import jax
import jax.numpy as jnp
from jax import lax
from jax.experimental import pallas as pl
from jax.experimental.pallas import tpu as pltpu

F32 = jnp.float32
BF16 = jnp.bfloat16

D_MODEL = 1024
HEAD_DIM = 64
NSA_HEADS = 8
NSA_KV_GROUPS = 2
NSA_GROUP_SIZE = NSA_HEADS // NSA_KV_GROUPS
CMP_BLOCK = 32
CMP_STRIDE = 16
CMP_HIDDEN = 128
SEL_BLOCK = 64
SEL_TOPK = 8
WINDOW = 512
FORCED_SCORE = 1e9
DIFF_HEADS = 4
MOE_GROUPS = 4
EXPERTS_PER_GROUP = 4
N_EXPERTS = MOE_GROUPS * EXPERTS_PER_GROUP
EXPERT_FF = D_MODEL // 8
RMS_EPS = 1e-6
SUBLN_EPS = 1e-5
NEG_INF = -1e30
N_ALIBI_HEADS = NSA_HEADS + DIFF_HEADS

NSA_Q_COLS = NSA_HEADS * HEAD_DIM
NSA_KV_COLS = NSA_KV_GROUPS * HEAD_DIM
NSA_GATE_COLS = 3 * NSA_HEADS
DIFF_QK_COLS = DIFF_HEADS * 2 * HEAD_DIM
DIFF_V_COLS = DIFF_HEADS * 2 * HEAD_DIM
GATE_ROWS_PER_GROUP = 16

CH = 256
DIFF_TILE_CHUNKS = 2
DIFF_HEADS_PER_STEP = 4
ROW_TILE = 1024
VMEM_LIMIT = 56 * 1024 * 1024

LOG2E = float(np.log2(np.e))
Q_SCALE = HEAD_DIM ** -0.5 * LOG2E

QK_LANES = 2 * HEAD_DIM
AUG_LANES = 128
SLOPE_PIECES = 3
PAD_ROW = 2 * SLOPE_PIECES
ALIBI_ROWS = 16
MASK_ROW0 = ALIBI_ROWS
MASK_BIG = 1e30
SUM_ROWS = 16
BLOCK_COLS = 256

_NT = (((1,), (1,)), ((), ()))
_TN = (((0,), (0,)), ((), ()))


def _dot(a, b):
    return jnp.dot(a, b, preferred_element_type=F32)


def _const_spec(shape):
    nd = len(shape)
    return pl.BlockSpec(shape, lambda *_: (0,) * nd, pipeline_mode=pl.Buffered(1))


def _layer_spec(stacked, layer):
    nd = stacked.ndim - 1
    return pl.BlockSpec((None,) + tuple(stacked.shape[1:]), lambda *_: (layer,) + (0,) * nd,
                        pipeline_mode=pl.Buffered(1))


def _params(sem):
    return pltpu.CompilerParams(dimension_semantics=sem, vmem_limit_bytes=VMEM_LIMIT)


def _alibi_slopes():
    return 2.0 ** (-8.0 * np.arange(1, N_ALIBI_HEADS + 1) / N_ALIBI_HEADS)


_NAT_WIDTHS = (NSA_KV_COLS, NSA_KV_COLS, DIFF_QK_COLS, 2 * NSA_KV_COLS, 2 * D_MODEL)
_TR_ROWS = (NSA_Q_COLS, DIFF_QK_COLS, NSA_KV_COLS, NSA_KV_COLS, DIFF_V_COLS, 2 * GATE_ROWS_PER_GROUP)
_TR_SCALE = (Q_SCALE, Q_SCALE, 1.0, 1.0, 1.0, 1.0)
_KCVC_INDEX = 3


def _inproj_kernel(x_ref, g_ref, wn_ref, wt_ref, *refs):
    n_out = len(_NAT_WIDTHS) + len(_TR_ROWS)
    nat_refs = refs[:len(_NAT_WIDTHS)]
    tr_refs = refs[len(_NAT_WIDTHS):n_out]
    rows_k, rows_v = refs[n_out:]
    x = x_ref[...]
    xn = (x * lax.rsqrt(jnp.mean(x * x, axis=-1, keepdims=True) + RMS_EPS) * g_ref[...]).astype(BF16)
    off = 0
    for idx, (ref, width) in enumerate(zip(nat_refs, _NAT_WIDTHS)):
        if idx == _KCVC_INDEX:
            res = _dot(xn, wn_ref[:, off:off + width])
            rows_k[...] = res[:, :NSA_KV_COLS]
            rows_v[...] = res[:, NSA_KV_COLS:]
            n_rows = res.shape[0] // CMP_STRIDE
            for tok in range(CMP_STRIDE):
                lo = tok * width
                ref[:, lo:lo + NSA_KV_COLS] = rows_k[pl.ds(tok, n_rows, stride=CMP_STRIDE), :]
                ref[:, lo + NSA_KV_COLS:lo + width] = rows_v[pl.ds(tok, n_rows, stride=CMP_STRIDE), :]
        else:
            for c in range(0, width, 512):
                cw = min(512, width - c)
                ref[:, c:c + cw] = _dot(xn, wn_ref[:, off + c:off + c + cw]).astype(ref.dtype)
        off += width
    n_sub = x.shape[0] // CH
    off = 0
    for ref, rows, scale in zip(tr_refs, _TR_ROWS, _TR_SCALE):
        for c in range(0, rows, 256):
            rw = min(256, rows - c)
            res = lax.dot_general(wt_ref[off + c:off + c + rw, :], xn, _NT, preferred_element_type=F32)
            if scale != 1.0:
                res = res * scale
            for j in range(n_sub):
                ref[j, c:c + rw, :] = res[:, j * CH:(j + 1) * CH].astype(ref.dtype)
        off += rows


def _inproj(h2d, g, wn, wt, layer):
    n = h2d.shape[0]
    tm = ROW_TILE
    nat_dtypes = (BF16, BF16, BF16, F32, BF16)
    tr_dtypes = (BF16, BF16, BF16, BF16, BF16, F32)
    out_shape = [jax.ShapeDtypeStruct((n, w), dt) for w, dt in zip(_NAT_WIDTHS, nat_dtypes)]
    out_shape += [jax.ShapeDtypeStruct((n // CH, r, CH), dt) for r, dt in zip(_TR_ROWS, tr_dtypes)]
    out_specs = [pl.BlockSpec((tm, w), lambda i: (i, 0)) for w in _NAT_WIDTHS]
    out_specs += [pl.BlockSpec((tm // CH, r, CH), lambda i: (i, 0, 0)) for r in _TR_ROWS]
    hb_width = CMP_STRIDE * _NAT_WIDTHS[_KCVC_INDEX]
    out_shape[_KCVC_INDEX] = jax.ShapeDtypeStruct((n // CMP_STRIDE, hb_width), F32)
    out_specs[_KCVC_INDEX] = pl.BlockSpec((tm // CMP_STRIDE, hb_width), lambda i: (i, 0))
    return pl.pallas_call(
        _inproj_kernel,
        grid=(n // tm,),
        in_specs=[pl.BlockSpec((tm, D_MODEL), lambda i: (i, 0)),
                  _layer_spec(g, layer), _layer_spec(wn, layer), _layer_spec(wt, layer)],
        out_specs=out_specs,
        out_shape=out_shape,
        scratch_shapes=[pltpu.VMEM((tm, NSA_KV_COLS), F32), pltpu.VMEM((tm, NSA_KV_COLS), F32)],
        compiler_params=_params(("parallel",)),
        name="inproj",
    )(h2d, g, wn, wt)


def _compress_kernel(hb_ref, pe_ref, w1_ref, b1_ref, w2k_ref, b2k_ref, w2v_ref, b2v_ref, kc_ref, vct_ref):
    hb = hb_ref[0]
    rows = hb.shape[0]
    top = (hb + pe_ref[0]).astype(BF16)
    bot = (hb + pe_ref[1]).astype(BF16)
    p = _dot(top, w1_ref[0])
    q = _dot(bot, w1_ref[1])
    q_next = pltpu.roll(q, rows - 1, 0)
    hid = jax.nn.gelu(p + q_next + b1_ref[...])
    width = hid.shape[1] // 2
    kc_ref[0] = (_dot(hid[:, :width].astype(BF16), w2k_ref[...]) + b2k_ref[...]).astype(kc_ref.dtype)
    vct = lax.dot_general(w2v_ref[...], hid[:, width:].astype(BF16), _NT, preferred_element_type=F32)
    vct_ref[0] = (vct + b2v_ref[...]).astype(vct_ref.dtype)


def _compress(hb, pe_hb, w1_big, b1p, w2k, b2k, w2vt, b2v, layer):
    bsz, rows, width = hb.shape
    gk = NSA_KV_COLS
    return pl.pallas_call(
        _compress_kernel,
        grid=(bsz,),
        in_specs=[pl.BlockSpec((1, rows, width), lambda b: (b, 0, 0)),
                  *[_layer_spec(a, layer) for a in (pe_hb, w1_big, b1p, w2k, b2k, w2vt, b2v)]],
        out_specs=[pl.BlockSpec((1, rows, gk), lambda b: (b, 0, 0)),
                   pl.BlockSpec((1, gk, rows), lambda b: (b, 0, 0))],
        out_shape=[jax.ShapeDtypeStruct((bsz, rows, gk), BF16),
                   jax.ShapeDtypeStruct((bsz, gk, rows), BF16)],
        compiler_params=_params(("parallel",)),
        name="compress",
    )(hb, pe_hb, w1_big, b1p, w2k, b2k, w2vt, b2v)


def _bf16_pieces(x):
    out = []
    rest = np.asarray(x, np.float32)
    for _ in range(SLOPE_PIECES):
        piece = rest.astype(BF16).astype(np.float32)
        out.append(piece)
        rest = rest - piece
    return out


def _slope_rows(slopes, heads_per_block, cols_per_head):
    sl2 = (np.asarray(slopes, np.float32).astype(np.float64) * LOG2E).astype(np.float32)
    pieces = np.stack(_bf16_pieces(sl2) * 2, axis=0)
    rows = np.zeros((ALIBI_ROWS, sl2.shape[0]), np.float32)
    rows[:pieces.shape[0]] = pieces
    rows[PAD_ROW] = -MASK_BIG
    rows = np.repeat(rows, cols_per_head, axis=1)
    rows = rows.reshape(ALIBI_ROWS, -1, heads_per_block * cols_per_head).transpose(1, 0, 2)
    return jnp.asarray(rows, BF16)


def _key_aug_tables(seq, nb):
    pos = np.arange(seq)
    aug = np.zeros((2, seq + CH, AUG_LANES), np.float32)
    aug[:, :seq, 0:SLOPE_PIECES] = (pos % CH)[None, :, None]
    aug[:, :seq, SLOPE_PIECES:2 * SLOPE_PIECES] = (pos // CH * CH)[None, :, None]
    aug[0, pos, MASK_ROW0 + pos // SEL_BLOCK] = 1.0
    aug[:, seq:, PAD_ROW] = 1.0
    return jnp.asarray(aug.reshape(2, seq // CH + 1, CH, AUG_LANES), BF16)


def _cmp_aug_table(n_rows):
    aug = np.zeros((n_rows, AUG_LANES), np.float32)
    aug[:, 0:SLOPE_PIECES] = (np.arange(n_rows) * CMP_STRIDE)[:, None]
    aug[:, SLOPE_PIECES:2 * SLOPE_PIECES] = CMP_BLOCK - 1
    return jnp.asarray(aug, BF16)


def _tile_lanes(x, reps):
    return jnp.concatenate([x] * reps, axis=1)


def _query_minus_key(reps):
    shape = (CH, reps * CH)
    q_off = jnp.bitwise_and(lax.broadcasted_iota(jnp.int32, shape, 1), CH - 1)
    return q_off - lax.broadcasted_iota(jnp.int32, shape, 0)


def _flash_init(m_ref, acc_ref):
    m_ref[...] = jnp.full(m_ref.shape, NEG_INF, F32)
    acc_ref[...] = jnp.zeros(acc_ref.shape, F32)


def _normalized(acc_ref, g):
    dv = acc_ref.shape[1] - SUM_ROWS
    return acc_ref[g, 0:dv, :] / acc_ref[g, dv:dv + 1, :]


def _chunk_scores(k_blk, aug_blk, qa_ref):
    return _dot(jnp.concatenate([k_blk, aug_blk], axis=1), qa_ref[...])


def _stage_and_consume(prod, cons, qa_ref, block_cols=BLOCK_COLS):
    if prod is not None:
        k_blk, aug_blk, (ps_ref, pmx_ref), pmask = prod
        k_full = jnp.concatenate([k_blk, aug_blk], axis=1)
    if cons is not None:
        (cs_ref, cmx_ref), v_t, (m_ref, acc_ref), cmask = cons
        _, rows, gcols = acc_ref.shape
        dv = rows - SUM_ROWS
        ones = jnp.ones((SUM_ROWS, v_t.shape[1]), BF16)
    for c0 in range(0, qa_ref.shape[1], block_cols):
        csl = slice(c0, c0 + block_cols)
        if prod is not None:
            s = _dot(k_full, qa_ref[:, csl])
            if pmask is not None:
                s = jnp.where(pmask(c0), s, NEG_INF)
            ps_ref[:, csl] = s
            pmx_ref[:, csl] = jnp.max(s, axis=0, keepdims=True)
        if cons is not None:
            s = cs_ref[:, csl]
            if cmask is None:
                mx = cmx_ref[:, csl]
            else:
                s = jnp.where(cmask(c0), s, NEG_INF)
                mx = jnp.max(s, axis=0, keepdims=True)
            m_prev = m_ref[:, csl]
            m_new = jnp.maximum(m_prev, mx)
            alpha = jnp.exp2(m_prev - m_new)
            p = jnp.exp2(s - m_new).astype(BF16)
            g = c0 // gcols
            gsl = slice(c0 - g * gcols, c0 - g * gcols + block_cols)
            v_ones = jnp.concatenate([v_t[g * dv:(g + 1) * dv], ones], axis=0)
            acc_ref[g, :, gsl] = alpha * acc_ref[g, :, gsl] + _dot(v_ones, p)
            m_ref[:, csl] = m_new


def _consume(buf, v_t, state, qa_ref, mask=None, block_cols=BLOCK_COLS):
    _stage_and_consume(None, (buf, v_t, state, mask), qa_ref, block_cols)


def _pad_or(aug_ref, c, is_pad):
    return aug_ref[jnp.where(is_pad, aug_ref.shape[0] - 1, c)]


def _identity(pos):
    return pos


def _causal_first(n, k_ref, aug_ref, buf0, chunk_at=_identity):
    c = chunk_at(0)
    return (k_ref[c], _pad_or(aug_ref, c, jnp.bitwise_and(n, 1) == 1), buf0, None)


def _causal_pairs(n, k_ref, v_ref, aug_ref, qa_ref, bufs, state, block_cols=BLOCK_COLS, chunk_at=_identity):
    _causal_pairs_multi(n, [(k_ref, v_ref, aug_ref, qa_ref, bufs, state)], block_cols, chunk_at)


def _causal_pairs_multi(n, streams, block_cols=BLOCK_COLS, chunk_at=_identity):
    pad = jnp.bitwise_and(n, 1)

    def pair(k, carry):
        pos = 2 * k - pad
        c0 = chunk_at(jnp.maximum(pos, 0))
        c1 = chunk_at(pos + 1)
        c2 = chunk_at(pos + 2)
        for k_ref, v_ref, aug_ref, qa_ref, (buf0, buf1), state in streams:
            _stage_and_consume((k_ref[c1], aug_ref[c1], buf1, None), (buf0, v_ref[c0], state, None),
                               qa_ref, block_cols)
        for k_ref, v_ref, aug_ref, qa_ref, (buf0, buf1), state in streams:
            _stage_and_consume((k_ref[c2], aug_ref[c2], buf0, None), (buf1, v_ref[c1], state, None),
                               qa_ref, block_cols)
        return carry
    lax.fori_loop(0, jnp.right_shift(n + pad, 1), pair, 0)


def _unselected_mask_rows(imp, k_sel):
    n_blk, width = imp.shape
    rows_per = 8
    j_loc = lax.broadcasted_iota(jnp.int32, (rows_per, width), 0)
    mask_blocks = []
    for r0 in range(0, n_blk, rows_per):
        blk = imp[r0:r0 + rows_per, :]
        cnt = jnp.zeros((rows_per, width), jnp.int32)
        for jp in range(n_blk):
            row = imp[jp:jp + 1, :]
            gt = jnp.where(row > blk, 1, 0)
            ge = jnp.where(row >= blk, 1, 0)
            if jp >= r0 + rows_per - 1:
                cnt = cnt + gt
            elif jp < r0:
                cnt = cnt + ge
            else:
                cnt = cnt + jnp.where(j_loc + r0 > jp, ge, gt)
        mask_blocks.append(jnp.where(cnt < k_sel, 0.0, -MASK_BIG))
    return jnp.concatenate(mask_blocks, axis=0)


def _nsa_kernel(q_ref, srow_ref, ks_ref, kw_ref, vs_ref, vw_ref, kaug_ref, kc_ref, kcaug_ref, vc_ref,
                ng_ref, asel_ref, o_ref, qa_ref, ocmp_ref, s0, x0, s1, x1, w0, y0, w1, y1, w2, y2,
                m_s, acc_s, m_w, acc_w, chunk_list):
    i = pl.program_id(1)
    t0 = i * CH
    nh = NSA_GROUP_SIZE
    ng = NSA_KV_GROUPS
    gcols = nh * CH
    cols = ng * gcols
    n_cmp = kc_ref.shape[1]
    n_blk = asel_ref.shape[0]
    k_sel = min(SEL_TOPK, n_blk)
    bufs_s = ((s0, x0), (s1, x1))
    bufs_w = ((w0, y0), (w1, y1), (w2, y2))
    state_s = (m_s, acc_s)
    state_w = (m_w, acc_w)
    kaug_s = kaug_ref.at[0]
    kaug_w = kaug_ref.at[1]

    t_pos = t0 + lax.broadcasted_iota(jnp.int32, (1, CH), 1)
    d0_i = _query_minus_key(BLOCK_COLS // CH)
    causal = lambda c0: d0_i >= 0
    window_edge = lambda c0: d0_i < 0

    zeros_q = jnp.zeros((HEAD_DIM, CH), BF16)
    for g in range(ng):
        for hh in range(nh):
            h = g * nh + hh
            qh = q_ref[0, h * HEAD_DIM:(h + 1) * HEAD_DIM, :]
            for gg in range(ng):
                qa_ref[gg * HEAD_DIM:(gg + 1) * HEAD_DIM, h * CH:(h + 1) * CH] = qh if gg == g else zeros_q
    qa_ref[QK_LANES:QK_LANES + ALIBI_ROWS, :] = srow_ref[...]
    qa_ref[QK_LANES + MASK_ROW0:, :] = jnp.zeros((AUG_LANES - MASK_ROW0, cols), BF16)

    cmp_scores = _chunk_scores(kc_ref[0], kcaug_ref[...], qa_ref)

    n_back = WINDOW // CH
    stage_w = []
    chunks_w = []
    for back in range(n_back, 0, -1):
        c = jnp.maximum(i - back, 0)
        stage_w.append((kw_ref[c], _pad_or(kaug_w, c, i < back), bufs_w[n_back - back],
                        window_edge if back == n_back else None))
        chunks_w.append(c)
    stage_w.append((kw_ref[i], kaug_w[i], bufs_w[n_back], causal))
    chunks_w.append(i)
    _stage_and_consume(stage_w[0], None, qa_ref)

    n_idx = lax.broadcasted_iota(jnp.int32, (n_cmp, cols), 0)
    t_pos_all = t0 + jnp.bitwise_and(lax.broadcasted_iota(jnp.int32, (1, cols), 1), CH - 1)
    valid_c = n_idx * CMP_STRIDE + (CMP_BLOCK - 1) <= t_pos_all
    lg = jnp.where(valid_c, cmp_scores, NEG_INF)
    m = jnp.max(lg, axis=0, keepdims=True)
    p = jnp.where(valid_c, jnp.exp2(lg - m), 0.0)
    l = jnp.sum(p, axis=0, keepdims=True)
    pc = p * jnp.where(l > 0.0, 1.0 / l, 0.0)
    pc_b = pc.astype(BF16)
    vc = vc_ref[0]

    j_idx = lax.broadcasted_iota(jnp.int32, (n_blk, CH), 0)
    cur = jnp.right_shift(t_pos, SEL_BLOCK.bit_length() - 1)
    forced = (j_idx == 0) | (j_idx == cur) | (j_idx == cur - 1)
    in_past = j_idx * SEL_BLOCK <= t_pos
    for g in range(ng):
        gsl = slice(g * gcols, (g + 1) * gcols)
        ocmp_ref[g] = _dot(vc[g * HEAD_DIM:(g + 1) * HEAD_DIM], pc_b[:, gsl])
        psum = pc[:, g * gcols:g * gcols + CH]
        for hh in range(1, nh):
            psum = psum + pc[:, g * gcols + hh * CH:g * gcols + (hh + 1) * CH]
        imp = jnp.dot(asel_ref[...], psum, precision=lax.Precision.HIGHEST, preferred_element_type=F32)
        imp = jnp.where(in_past, jnp.where(forced, FORCED_SCORE, imp), -1.0)
        mask_rows = _unselected_mask_rows(imp, k_sel)
        kept = mask_rows if g == 0 else jnp.maximum(kept, mask_rows)
        qa_ref[QK_LANES + MASK_ROW0:QK_LANES + MASK_ROW0 + n_blk, gsl] = _tile_lanes(mask_rows.astype(BF16), nh)

    blocks_per_chunk = CH // SEL_BLOCK
    n_sel = jnp.int32(0)
    for c in range(n_blk // blocks_per_chunk):
        hit = jnp.max(kept[c * blocks_per_chunk:(c + 1) * blocks_per_chunk, :]) > -1.0
        chunk_list[n_sel] = c
        n_sel = n_sel + jnp.logical_and(hit, c < i).astype(jnp.int32)
    chunk_list[n_sel] = i

    def chunk_at(pos):
        return chunk_list[pos]

    _flash_init(*state_w)
    for j in range(1, n_back + 1):
        _stage_and_consume(stage_w[j], (bufs_w[j - 1], vw_ref[chunks_w[j - 1]], state_w, None), qa_ref)
    _stage_and_consume(_causal_first(n_sel, ks_ref, kaug_s, bufs_s[0], chunk_at),
                       (bufs_w[n_back], vw_ref[i], state_w, None), qa_ref)
    _flash_init(*state_s)
    _causal_pairs(n_sel, ks_ref, vs_ref, kaug_s, qa_ref, bufs_s, state_s, chunk_at=chunk_at)
    _consume(bufs_s[0], vs_ref[i], state_s, qa_ref, mask=causal)

    for g in range(ng):
        def gate(branch):
            r0 = g * GATE_ROWS_PER_GROUP + branch * nh
            return jax.nn.sigmoid(jnp.concatenate([ng_ref[0, r0 + hh:r0 + hh + 1, :] for hh in range(nh)], axis=1))
        out = gate(0) * ocmp_ref[g] + gate(1) * _normalized(acc_s, g) + gate(2) * _normalized(acc_w, g)
        for hh in range(nh):
            h = g * nh + hh
            o_ref[0, h * HEAD_DIM:(h + 1) * HEAD_DIM, :] = out[:, hh * CH:(hh + 1) * CH].astype(o_ref.dtype)


def _nsa(nq_t, srows, ks3, kw3, vs_t, vw_t, kaug, kc, kcaug, vc_t, ng_t, asel_t, bsz, nq):
    n_cmp = kc.shape[1]
    ng = NSA_KV_GROUPS
    gcols = NSA_GROUP_SIZE * CH
    cols = ng * gcols
    acc_shape = (ng, HEAD_DIM + SUM_ROWS, gcols)
    stage = [pltpu.VMEM((CH, cols), F32), pltpu.VMEM((1, cols), F32)]
    return pl.pallas_call(
        _nsa_kernel,
        grid=(bsz, nq),
        in_specs=[
            pl.BlockSpec((1, NSA_Q_COLS, CH), lambda b, i: (b * nq + i, 0, 0)),
            pl.BlockSpec(srows.shape, lambda b, i: (0, 0)),
            pl.BlockSpec((nq, CH, NSA_KV_COLS), lambda b, i: (b, 0, 0)),
            pl.BlockSpec((nq, CH, NSA_KV_COLS), lambda b, i: (b, 0, 0)),
            pl.BlockSpec((nq, NSA_KV_COLS, CH), lambda b, i: (b, 0, 0)),
            pl.BlockSpec((nq, NSA_KV_COLS, CH), lambda b, i: (b, 0, 0)),
            pl.BlockSpec(kaug.shape, lambda b, i: (0, 0, 0, 0)),
            pl.BlockSpec((1, n_cmp, NSA_KV_COLS), lambda b, i: (b, 0, 0)),
            pl.BlockSpec(kcaug.shape, lambda b, i: (0, 0)),
            pl.BlockSpec((1, NSA_KV_COLS, n_cmp), lambda b, i: (b, 0, 0)),
            pl.BlockSpec((1, ng * GATE_ROWS_PER_GROUP, CH), lambda b, i: (b * nq + i, 0, 0)),
            pl.BlockSpec(asel_t.shape, lambda b, i: (0, 0)),
        ],
        out_specs=pl.BlockSpec((1, NSA_Q_COLS, CH), lambda b, i: (b * nq + i, 0, 0)),
        out_shape=jax.ShapeDtypeStruct((bsz * nq, NSA_Q_COLS, CH), BF16),
        scratch_shapes=(
            [pltpu.VMEM((QK_LANES + AUG_LANES, cols), BF16), pltpu.VMEM((ng, HEAD_DIM, gcols), F32)]
            + stage * 5
            + [pltpu.VMEM((1, cols), F32), pltpu.VMEM(acc_shape, F32),
               pltpu.VMEM((1, cols), F32), pltpu.VMEM(acc_shape, F32),
               pltpu.SMEM((nq + 1,), jnp.int32)]),
        compiler_params=_params(("parallel", "arbitrary")),
        name="nsa_attention",
    )(nq_t, srows, ks3, kw3, vs_t, vw_t, kaug, kc, kcaug, vc_t, ng_t, asel_t)


def _diff_kernel(scal_ref, q_ref, srow_ref, k_ref, v_ref, kaug_ref, lam_ref, gain_ref, o_ref, *scratch):
    nt = DIFF_TILE_CHUNKS
    tq = nt * CH
    hd2 = 2 * HEAD_DIM
    c0 = pl.program_id(2) * nt
    lam_init = scal_ref[0]
    out_scale = scal_ref[1]
    sub2 = lax.broadcasted_iota(jnp.int32, (QK_LANES, CH), 0)
    zero = jnp.zeros((), BF16)
    bc = 2 * tq
    lane = lax.broadcasted_iota(jnp.int32, (CH, bc), 1)
    sub = lax.broadcasted_iota(jnp.int32, (CH, bc), 0)

    def on_or_after(first_key):
        return lambda col0: jnp.bitwise_and(lane + col0, tq - 1) - sub >= first_key

    streams = []
    for hd in range(DIFF_HEADS_PER_STEP):
        qa_ref, s0, x0, s1, x1, m_r, acc_r = scratch[7 * hd:7 * (hd + 1)]
        rows = slice(hd * hd2, (hd + 1) * hd2)
        for j in range(nt):
            q = q_ref[j, rows, :]
            qa_ref[0:QK_LANES, j * CH:(j + 1) * CH] = jnp.where(sub2 < HEAD_DIM, q, zero)
            qa_ref[0:QK_LANES, tq + j * CH:tq + (j + 1) * CH] = jnp.where(sub2 >= HEAD_DIM, q, zero)
        qa_ref[QK_LANES:QK_LANES + ALIBI_ROWS, :] = srow_ref[hd]
        qa_ref[QK_LANES + MASK_ROW0:, :] = jnp.zeros((AUG_LANES - MASK_ROW0, bc), BF16)
        streams.append((k_ref.at[:, :, rows], v_ref.at[:, rows, :], kaug_ref, qa_ref,
                        ((s0, x0), (s1, x1)), (m_r, acc_r)))

    for k_at, v_at, aug, qa_ref, bufs, state in streams:
        _stage_and_consume(_causal_first(c0, k_at, aug, bufs[0]), None, qa_ref, bc)
        _flash_init(*state)
    _causal_pairs_multi(c0, streams, bc)
    for k_at, v_at, aug, qa_ref, bufs, state in streams:
        _stage_and_consume((k_at[c0 + 1], aug[c0 + 1], bufs[1], on_or_after(CH)),
                           (bufs[0], v_at[c0], state, on_or_after(0)), qa_ref, bc)
    for k_at, v_at, aug, qa_ref, bufs, state in streams:
        _consume(bufs[1], v_at[c0 + 1], state, qa_ref, block_cols=bc)

    lp = lam_ref[...]
    lam = (jnp.exp(jnp.sum(lp[0:1] * lp[1:2], axis=1, keepdims=True))
           - jnp.exp(jnp.sum(lp[2:3] * lp[3:4], axis=1, keepdims=True)) + lam_init)
    for hd, stream in enumerate(streams):
        att = _normalized(stream[5][1], 0)
        o = att[:, 0:tq] - lam * att[:, tq:2 * tq]
        o = o * lax.rsqrt(jnp.mean(o * o, axis=0, keepdims=True) + SUBLN_EPS) * gain_ref[...]
        o = (o * out_scale).astype(o_ref.dtype)
        for j in range(nt):
            o_ref[j, hd * hd2:(hd + 1) * hd2, :] = o[:, j * CH:(j + 1) * CH]


def _diff(scal, dq_t, srows, dk3, dv_t, kaug_plain, lam_p, gain, bsz, nq):
    hd2 = 2 * HEAD_DIM
    nt = DIFF_TILE_CHUNKS
    n_tiles = nq // nt
    cols = 2 * nt * CH
    hps = DIFF_HEADS_PER_STEP
    rows = hps * hd2
    per_head = [
        pltpu.VMEM((QK_LANES + AUG_LANES, cols), BF16),
        pltpu.VMEM((CH, cols), F32), pltpu.VMEM((1, cols), F32),
        pltpu.VMEM((CH, cols), F32), pltpu.VMEM((1, cols), F32),
        pltpu.VMEM((1, cols), F32), pltpu.VMEM((1, hd2 + SUM_ROWS, cols), F32),
    ]
    grid_spec = pltpu.PrefetchScalarGridSpec(
        num_scalar_prefetch=1,
        grid=(bsz, DIFF_HEADS // hps, n_tiles),
        in_specs=[
            pl.BlockSpec((nt, rows, CH), lambda b, h, i, s: (b * n_tiles + i, h, 0)),
            pl.BlockSpec((hps, ALIBI_ROWS, cols), lambda b, h, i, s: (h, 0, 0)),
            pl.BlockSpec((nq, CH, rows), lambda b, h, i, s: (b, 0, h)),
            pl.BlockSpec((nq, rows, CH), lambda b, h, i, s: (b, h, 0)),
            pl.BlockSpec(kaug_plain.shape, lambda b, h, i, s: (0, 0, 0)),
            pl.BlockSpec(lam_p.shape, lambda b, h, i, s: (0, 0)),
            pl.BlockSpec(gain.shape, lambda b, h, i, s: (0, 0)),
        ],
        out_specs=pl.BlockSpec((nt, rows, CH), lambda b, h, i, s: (b * n_tiles + i, h, 0)),
        scratch_shapes=per_head * hps,
    )
    return pl.pallas_call(
        _diff_kernel,
        grid_spec=grid_spec,
        out_shape=jax.ShapeDtypeStruct((bsz * nq, DIFF_V_COLS, CH), BF16),
        compiler_params=_params(("parallel", "parallel", "arbitrary")),
        name="diff_attention",
    )(scal, dq_t, srows, dk3, dv_t, kaug_plain, lam_p, gain)


def _merge_kernel(ya_ref, yb_ref, gates_ref, h_ref, wa_ref, wb_ref, wo_ref, o_ref):
    for j in range(ya_ref.shape[0]):
        rows = slice(j * CH, (j + 1) * CH)
        a = lax.dot_general(ya_ref[j], wa_ref[...], _TN, preferred_element_type=F32)
        b = lax.dot_general(yb_ref[j], wb_ref[...], _TN, preferred_element_type=F32)
        ga = jax.nn.sigmoid(gates_ref[rows, :D_MODEL].astype(F32))
        gb = jax.nn.sigmoid(gates_ref[rows, D_MODEL:].astype(F32))
        merged = (ga * a + gb * b).astype(BF16)
        o_ref[rows, :] = h_ref[rows, :] + _dot(merged, wo_ref[...])


def _merge(ya_t, yb_t, gates, h2d, wa, wb, wo, layer):
    n = h2d.shape[0]
    tm = ROW_TILE
    return pl.pallas_call(
        _merge_kernel,
        grid=(n // tm,),
        in_specs=[pl.BlockSpec((tm // CH, NSA_Q_COLS, CH), lambda i: (i, 0, 0)),
                  pl.BlockSpec((tm // CH, DIFF_V_COLS, CH), lambda i: (i, 0, 0)),
                  pl.BlockSpec((tm, 2 * D_MODEL), lambda i: (i, 0)),
                  pl.BlockSpec((tm, D_MODEL), lambda i: (i, 0)),
                  _layer_spec(wa, layer), _layer_spec(wb, layer), _layer_spec(wo, layer)],
        out_specs=pl.BlockSpec((tm, D_MODEL), lambda i: (i, 0)),
        out_shape=jax.ShapeDtypeStruct((n, D_MODEL), F32),
        compiler_params=_params(("parallel",)),
        name="merge_outproj",
    )(ya_t, yb_t, gates, h2d, wa, wb, wo)


def _first_argmax(x, rows, n):
    mx = jnp.max(x, axis=0, keepdims=True)
    idx = jnp.min(jnp.where(x == mx, rows, n), axis=0, keepdims=True)
    return mx, idx


def _moe_kernel(h_ref, g_ref, wr_ref, br_ref, eexp_ref, wg_ref, wu_ref, wd_ref, fg_ref, o_ref, *, final):
    hres = h_ref[...]
    xf = hres * lax.rsqrt(jnp.mean(hres * hres, axis=-1, keepdims=True) + RMS_EPS) * g_ref[...]
    xb = xf.astype(BF16)
    tm = hres.shape[0]

    x_lo = (xf - xb.astype(F32)).astype(BF16)
    n_r = wr_ref.shape[0] // 2
    part = lax.dot_general(wr_ref[...], xb, _NT, preferred_element_type=F32)
    logits = (part[0:n_r] + part[n_r:2 * n_r]
              + lax.dot_general(wr_ref[0:n_r, :], x_lo, _NT, preferred_element_type=F32)
              + br_ref[...])
    gl = logits[0:MOE_GROUPS]
    rows_g = lax.broadcasted_iota(jnp.int32, (MOE_GROUPS, tm), 0)
    gmax, gidx = _first_argmax(gl, rows_g, MOE_GROUPS)
    g_w = 1.0 / jnp.sum(jnp.exp(gl - gmax), axis=0, keepdims=True)
    esel = jnp.zeros((EXPERTS_PER_GROUP, tm), F32)
    for gg in range(MOE_GROUPS):
        lo = MOE_GROUPS + gg * EXPERTS_PER_GROUP
        esel = jnp.where(gidx == gg, logits[lo:lo + EXPERTS_PER_GROUP], esel)
    rows_e = lax.broadcasted_iota(jnp.int32, (EXPERTS_PER_GROUP, tm), 0)
    v1, i1 = _first_argmax(esel, rows_e, EXPERTS_PER_GROUP)
    rest = jnp.where(rows_e == i1, -jnp.inf, esel)
    v2, i2 = _first_argmax(rest, rows_e, EXPERTS_PER_GROUP)
    e21 = jnp.exp(v2 - v1)
    w1 = g_w / (1.0 + e21)
    w2 = g_w * e21 / (1.0 + e21)
    rows_c = lax.broadcasted_iota(jnp.int32, (N_EXPERTS, tm), 0)
    grp_c = jnp.right_shift(rows_c, EXPERTS_PER_GROUP.bit_length() - 1)
    exp_c = jnp.bitwise_and(rows_c, EXPERTS_PER_GROUP - 1)
    comb = jnp.where(grp_c == gidx,
                     jnp.where(exp_c == i1, w1, 0.0) + jnp.where(exp_c == i2, w2, 0.0), 0.0)
    comb_hi = comb.astype(BF16)
    comb_lo = (comb - comb_hi.astype(F32)).astype(BF16)

    comb_nat = (lax.dot_general(comb_hi, eexp_ref[...], _TN, preferred_element_type=F32)
                + lax.dot_general(comb_lo, eexp_ref[...], _TN, preferred_element_type=F32))

    acc = hres
    n_ff = wg_ref.shape[1]
    step = 256
    for c in range(0, n_ff, step):
        cols = slice(c, c + step)
        hg = _dot(xb, wg_ref[:, cols])
        hu = _dot(xb, wu_ref[:, cols])
        act = jax.nn.silu(hg) * hu
        parts = []
        for e0 in range(0, step, EXPERT_FF):
            e = (c + e0) // EXPERT_FF
            parts.append((act[:, e0:e0 + EXPERT_FF] * comb_nat[:, e:e + 1]).astype(BF16))
        acc = acc + _dot(jnp.concatenate(parts, axis=1), wd_ref[cols, :])
    if final:
        acc = acc * lax.rsqrt(jnp.mean(acc * acc, axis=-1, keepdims=True) + RMS_EPS) * fg_ref[...]
    o_ref[...] = acc


def _moe(h2d, g, wr_t, br, eexp, wg, wu, wd, fg, layer, final):
    n = h2d.shape[0]
    tm = ROW_TILE
    return pl.pallas_call(
        functools.partial(_moe_kernel, final=final),
        grid=(n // tm,),
        in_specs=[pl.BlockSpec((tm, D_MODEL), lambda i: (i, 0)),
                  _layer_spec(g, layer), _layer_spec(wr_t, layer), _layer_spec(br, layer),
                  _const_spec(eexp.shape), _layer_spec(wg, layer), _layer_spec(wu, layer),
                  _layer_spec(wd, layer), _const_spec(fg.shape)],
        out_specs=pl.BlockSpec((tm, D_MODEL), lambda i: (i, 0)),
        out_shape=jax.ShapeDtypeStruct((n, D_MODEL), F32),
        compiler_params=_params(("parallel",)),
        name="moe_final" if final else "moe",
    )(h2d, g, wr_t, br, eexp, wg, wu, wd, fg)


def _split_points():
    sizes = ([NSA_Q_COLS] + [NSA_KV_COLS] * 6
             + [NSA_GATE_COLS, DIFF_QK_COLS, DIFF_QK_COLS, DIFF_V_COLS, D_MODEL, D_MODEL])
    return [int(v) for v in np.cumsum(sizes)[:-1]]


def _cmp_to_sel_t(n_rows, nc, nb):
    c0 = np.arange(nc)[:, None] * CMP_STRIDE
    s0 = np.arange(nb)[None, :] * SEL_BLOCK
    ov = np.maximum(0, np.minimum(c0 + CMP_BLOCK, s0 + SEL_BLOCK) - np.maximum(c0, s0)) / CMP_BLOCK
    out = np.zeros((nb, n_rows), np.float32)
    out[:, :nc] = ov.T
    return out


def kernel(x, norm1_g, w_in, cmp_pe, cmp_w1, cmp_b1, cmp_w2, cmp_b2, diff_lambda, diff_subln_g, w_branch_a, w_branch_b, w_out, norm2_g, router_grp_w, router_grp_b, router_exp_w, router_exp_b, exp_w_gate, exp_w_up, exp_w_down, final_norm_g):
    bsz, seq, d = x.shape
    depth = w_in.shape[0]
    n = bsz * seq
    nq = seq // CH
    n_half = seq // CMP_STRIDE
    nc = (seq - CMP_BLOCK) // CMP_STRIDE + 1
    nb = seq // SEL_BLOCK
    assert d == D_MODEL and seq % ROW_TILE == 0 and seq >= WINDOW and WINDOW % CH == 0
    assert nb % 16 == 0 and MASK_ROW0 + nb <= AUG_LANES and n_half <= 256
    g_kv = NSA_KV_GROUPS
    eye_g = jnp.eye(g_kv, dtype=F32)

    (nq_w, kc_w, vc_w, ks_w, vs_w, kw_w, vw_w, ng_w, dq_w, dk_w, dv_w, ga_w, gb_w) = jnp.split(
        w_in, _split_points(), axis=-1)
    ng_w = ng_w.reshape(depth, d, g_kv, NSA_GROUP_SIZE, 3).transpose(0, 1, 2, 4, 3)
    ng_w = ng_w.reshape(depth, d, g_kv, 3 * NSA_GROUP_SIZE)
    ng_w = jnp.pad(ng_w, ((0, 0), (0, 0), (0, 0), (0, GATE_ROWS_PER_GROUP - 3 * NSA_GROUP_SIZE)))
    ng_w = ng_w.reshape(depth, d, g_kv * GATE_ROWS_PER_GROUP)
    wn_all = jnp.concatenate([ks_w, kw_w, dk_w, kc_w, vc_w, ga_w, gb_w], axis=-1).astype(BF16)
    wt_all = jnp.concatenate([nq_w, dq_w, vs_w, vw_w, dv_w, ng_w], axis=-1)
    wt_all = jnp.swapaxes(wt_all, 1, 2).astype(BF16)

    w1r = cmp_w1.reshape(depth, 2, 2, CMP_STRIDE, HEAD_DIM, CMP_HIDDEN).astype(BF16)
    w1_big = jnp.zeros((depth, 2, CMP_STRIDE, 2 * NSA_KV_COLS, 2 * g_kv * CMP_HIDDEN), BF16)
    for c in range(2):
        for g in range(g_kv):
            r0 = c * NSA_KV_COLS + g * HEAD_DIM
            c0 = (c * g_kv + g) * CMP_HIDDEN
            w1_big = w1_big.at[:, :, :, r0:r0 + HEAD_DIM, c0:c0 + CMP_HIDDEN].set(w1r[:, c])
    w1_big = w1_big.reshape(depth, 2, CMP_STRIDE * 2 * NSA_KV_COLS, 2 * g_kv * CMP_HIDDEN)
    per = cmp_pe.reshape(depth, 2, 2, CMP_STRIDE, HEAD_DIM)
    pe_hb = jnp.einsum('Lchld,g->Lhlcgd', per, jnp.ones((g_kv,), F32))
    pe_hb = pe_hb.reshape(depth, 2, 1, CMP_STRIDE * 2 * NSA_KV_COLS)
    b1p = jnp.broadcast_to(cmp_b1[:, :, None, :], (depth, 2, g_kv, CMP_HIDDEN)).reshape(depth, 1, -1)
    w2k = jnp.einsum('Lfd,gG->LgfGd', cmp_w2[:, 0], eye_g).reshape(depth, g_kv * CMP_HIDDEN, NSA_KV_COLS)
    w2vt = jnp.einsum('Lfd,gG->LGdgf', cmp_w2[:, 1], eye_g).reshape(depth, NSA_KV_COLS, g_kv * CMP_HIDDEN)
    w2k = w2k.astype(BF16)
    w2vt = w2vt.astype(BF16)
    b2k = jnp.tile(cmp_b2[:, 0], (1, g_kv))[:, None, :]
    b2v = jnp.tile(cmp_b2[:, 1], (1, g_kv))[:, :, None]

    slopes = _alibi_slopes()
    nsa_srows = _slope_rows(slopes[:NSA_HEADS], NSA_HEADS, CH)[0]
    diff_srows = _slope_rows(np.repeat(slopes[NSA_HEADS:], 2), 2, DIFF_TILE_CHUNKS * CH)
    kaug = _key_aug_tables(seq, nb)
    kcaug = _cmp_aug_table(n_half)
    asel_t = jnp.asarray(_cmp_to_sel_t(n_half, nc, nb))

    wa_all = w_branch_a.astype(BF16)
    wb_all = w_branch_b.astype(BF16)
    wo_all = w_out.astype(BF16)

    wr = jnp.concatenate([router_grp_w, router_exp_w.reshape(depth, d, N_EXPERTS)], axis=-1)
    n_r = MOE_GROUPS + N_EXPERTS
    wr_t = jnp.pad(jnp.swapaxes(wr, 1, 2), ((0, 0), (0, 32 - n_r), (0, 0)))
    wr_hi = wr_t.astype(BF16)
    wr_t = jnp.concatenate([wr_hi, (wr_t - wr_hi.astype(F32)).astype(BF16)], axis=1)
    br = jnp.concatenate([router_grp_b, router_exp_b.reshape(depth, N_EXPERTS)], axis=-1)
    br = jnp.pad(br, ((0, 0), (0, 32 - n_r)))[:, :, None]
    eexp = jnp.asarray(np.eye(N_EXPERTS, 128, dtype=np.float32), BF16)
    wg_all = jnp.swapaxes(exp_w_gate, 1, 2).reshape(depth, d, N_EXPERTS * EXPERT_FF).astype(BF16)
    wu_all = jnp.swapaxes(exp_w_up, 1, 2).reshape(depth, d, N_EXPERTS * EXPERT_FF).astype(BF16)
    wd_all = exp_w_down.reshape(depth, N_EXPERTS * EXPERT_FF, d).astype(BF16)

    h = x.reshape(n, d)
    for l in range(depth):
        (ks, kw, dk, kcvc_hb, gates, nq_t, dq_t, vs_t, vw_t, dv_t, ng_t) = _inproj(
            h, norm1_g[:, None, :], wn_all, wt_all, l)
        hb = kcvc_hb.reshape(bsz, n_half, CMP_STRIDE * 2 * NSA_KV_COLS)
        kc, vc_t = _compress(hb, pe_hb, w1_big, b1p, w2k, b2k, w2vt, b2v, l)
        ya_t = _nsa(nq_t, nsa_srows, ks.reshape(n // CH, CH, NSA_KV_COLS), kw.reshape(n // CH, CH, NSA_KV_COLS),
                    vs_t, vw_t, kaug, kc, kcaug, vc_t, ng_t, asel_t, bsz, nq)
        lam_init = 0.8 - 0.6 * float(np.exp(-0.3 * l))
        scal = jnp.asarray([lam_init, 1.0 - lam_init], F32)
        yb_t = _diff(scal, dq_t, diff_srows, dk.reshape(n // CH, CH, DIFF_QK_COLS), dv_t, kaug[1],
                     diff_lambda[l], diff_subln_g[l][:, None], bsz, nq)
        h = _merge(ya_t, yb_t, gates, h, wa_all, wb_all, wo_all, l)
        h = _moe(h, norm2_g[:, None, :], wr_t, br, eexp, wg_all, wu_all, wd_all,
                 final_norm_g[None, :], l, final=(l == depth - 1))
    return h.reshape(bsz, seq, d)
```

```python
import functools

import numpy as np
import jax
import jax.numpy as jnp
from jax import lax
from jax.experimental import pallas as pl
from jax.experimental.pallas import tpu as pltpu

F32 = jnp.float32
BF16 = jnp.bfloat16

D_MODEL = 1024
HEAD_DIM = 64
NSA_HEADS = 8
NSA_KV_GROUPS = 2
NSA_GROUP_SIZE = NSA_HEADS // NSA_KV_GROUPS
CMP_BLOCK = 32
CMP_STRIDE = 16
CMP_HIDDEN = 128
SEL_BLOCK = 64
SEL_TOPK = 8
WINDOW = 512
FORCED_SCORE = 1e9
DIFF_HEADS = 4
MOE_GROUPS = 4
EXPERTS_PER_GROUP = 4
N_EXPERTS = MOE_GROUPS * EXPERTS_PER_GROUP
EXPERT_FF = D_MODEL // 8
RMS_EPS = 1e-6
SUBLN_EPS = 1e-5
NEG_INF = -1e30
N_ALIBI_HEADS = NSA_HEADS + DIFF_HEADS

NSA_Q_COLS = NSA_HEADS * HEAD_DIM
NSA_KV_COLS = NSA_KV_GROUPS * HEAD_DIM
NSA_GATE_COLS = 3 * NSA_HEADS
DIFF_QK_COLS = DIFF_HEADS * 2 * HEAD_DIM
DIFF_V_COLS = DIFF_HEADS * 2 * HEAD_DIM
GATE_ROWS_PER_GROUP = 16

CH = 256
DIFF_TILE_CHUNKS = 2
DIFF_HEADS_PER_STEP = 4
ROW_TILE = 1024
MERGE_SUB = 2
VMEM_LIMIT = 56 * 1024 * 1024

LOG2E = float(np.log2(np.e))
Q_SCALE = HEAD_DIM ** -0.5 * LOG2E

QK_LANES = 2 * HEAD_DIM
AUG_LANES = 128
SLOPE_PIECES = 3
PAD_ROW = 2 * SLOPE_PIECES
ALIBI_ROWS = 16
MASK_ROW0 = ALIBI_ROWS
MASK_BIG = 1e30
SUM_ROWS = 16
BLOCK_COLS = 256

_NT = (((1,), (1,)), ((), ()))
_TN = (((0,), (0,)), ((), ()))


def _dot(a, b):
    return jnp.dot(a, b, preferred_element_type=F32)


def _const_spec(shape):
    nd = len(shape)
    return pl.BlockSpec(shape, lambda *_: (0,) * nd, pipeline_mode=pl.Buffered(1))


def _layer_spec(stacked, layer):
    nd = stacked.ndim - 1
    return pl.BlockSpec((None,) + tuple(stacked.shape[1:]), lambda *_: (layer,) + (0,) * nd,
                        pipeline_mode=pl.Buffered(1))


def _params(sem):
    return pltpu.CompilerParams(dimension_semantics=sem, vmem_limit_bytes=VMEM_LIMIT)


def _alibi_slopes():
    return 2.0 ** (-8.0 * np.arange(1, N_ALIBI_HEADS + 1) / N_ALIBI_HEADS)


_NAT_WIDTHS = (NSA_KV_COLS, NSA_KV_COLS, DIFF_QK_COLS, 2 * NSA_KV_COLS, 2 * D_MODEL)
_TR_ROWS = (NSA_Q_COLS, DIFF_QK_COLS, NSA_KV_COLS, NSA_KV_COLS, DIFF_V_COLS, 2 * GATE_ROWS_PER_GROUP)
_TR_SCALE = (Q_SCALE, Q_SCALE, 1.0, 1.0, 1.0, 1.0)
_KCVC_INDEX = 3


def _inproj_kernel(x_ref, g_ref, wn_ref, wt_ref, *refs):
    n_out = len(_NAT_WIDTHS) + len(_TR_ROWS)
    nat_refs = refs[:len(_NAT_WIDTHS)]
    tr_refs = refs[len(_NAT_WIDTHS):n_out]
    rows_k, rows_v = refs[n_out:]
    x = x_ref[...]
    xn = (x * lax.rsqrt(jnp.mean(x * x, axis=-1, keepdims=True) + RMS_EPS) * g_ref[...]).astype(BF16)
    off = 0
    for idx, (ref, width) in enumerate(zip(nat_refs, _NAT_WIDTHS)):
        if idx == _KCVC_INDEX:
            res = _dot(xn, wn_ref[:, off:off + width])
            rows_k[...] = res[:, :NSA_KV_COLS]
            rows_v[...] = res[:, NSA_KV_COLS:]
            n_rows = res.shape[0] // CMP_STRIDE
            for tok in range(CMP_STRIDE):
                lo = tok * width
                ref[:, lo:lo + NSA_KV_COLS] = rows_k[pl.ds(tok, n_rows, stride=CMP_STRIDE), :]
                ref[:, lo + NSA_KV_COLS:lo + width] = rows_v[pl.ds(tok, n_rows, stride=CMP_STRIDE), :]
        else:
            for c in range(0, width, 512):
                cw = min(512, width - c)
                ref[:, c:c + cw] = _dot(xn, wn_ref[:, off + c:off + c + cw]).astype(ref.dtype)
        off += width
    n_sub = x.shape[0] // CH
    off = 0
    for ref, rows, scale in zip(tr_refs, _TR_ROWS, _TR_SCALE):
        for c in range(0, rows, 512):
            rw = min(512, rows - c)
            res = lax.dot_general(wt_ref[off + c:off + c + rw, :], xn, _NT, preferred_element_type=F32)
            if scale != 1.0:
                res = res * scale
            for j in range(n_sub):
                ref[j, c:c + rw, :] = res[:, j * CH:(j + 1) * CH].astype(ref.dtype)
        off += rows


def _inproj(h2d, g, wn, wt, layer):
    n = h2d.shape[0]
    tm = ROW_TILE
    nat_dtypes = (BF16, BF16, BF16, F32, BF16)
    tr_dtypes = (BF16, BF16, BF16, BF16, BF16, F32)
    out_shape = [jax.ShapeDtypeStruct((n, w), dt) for w, dt in zip(_NAT_WIDTHS, nat_dtypes)]
    out_shape += [jax.ShapeDtypeStruct((n // CH, r, CH), dt) for r, dt in zip(_TR_ROWS, tr_dtypes)]
    out_specs = [pl.BlockSpec((tm, w), lambda i: (i, 0)) for w in _NAT_WIDTHS]
    out_specs += [pl.BlockSpec((tm // CH, r, CH), lambda i: (i, 0, 0)) for r in _TR_ROWS]
    hb_width = CMP_STRIDE * _NAT_WIDTHS[_KCVC_INDEX]
    out_shape[_KCVC_INDEX] = jax.ShapeDtypeStruct((n // CMP_STRIDE, hb_width), F32)
    out_specs[_KCVC_INDEX] = pl.BlockSpec((tm // CMP_STRIDE, hb_width), lambda i: (i, 0))
    return pl.pallas_call(
        _inproj_kernel,
        grid=(n // tm,),
        in_specs=[pl.BlockSpec((tm, D_MODEL), lambda i: (i, 0)),
                  _layer_spec(g, layer), _layer_spec(wn, layer), _layer_spec(wt, layer)],
        out_specs=out_specs,
        out_shape=out_shape,
        scratch_shapes=[pltpu.VMEM((tm, NSA_KV_COLS), F32), pltpu.VMEM((tm, NSA_KV_COLS), F32)],
        compiler_params=_params(("parallel",)),
        name="inproj",
    )(h2d, g, wn, wt)


def _compress_kernel(hb_ref, pe_ref, w1_ref, b1_ref, w2k_ref, b2k_ref, w2v_ref, b2v_ref, kc_ref, vct_ref):
    hb = hb_ref[0]
    rows = hb.shape[0]
    top = (hb + pe_ref[0]).astype(BF16)
    bot = (hb + pe_ref[1]).astype(BF16)
    p = _dot(top, w1_ref[0])
    q = _dot(bot, w1_ref[1])
    q_next = pltpu.roll(q, rows - 1, 0)
    hid = jax.nn.gelu(p + q_next + b1_ref[...])
    width = hid.shape[1] // 2
    kc_ref[0] = (_dot(hid[:, :width].astype(BF16), w2k_ref[...]) + b2k_ref[...]).astype(kc_ref.dtype)
    vct = lax.dot_general(w2v_ref[...], hid[:, width:].astype(BF16), _NT, preferred_element_type=F32)
    vct_ref[0] = (vct + b2v_ref[...]).astype(vct_ref.dtype)


def _compress(hb, pe_hb, w1_big, b1p, w2k, b2k, w2vt, b2v, layer):
    bsz, rows, width = hb.shape
    gk = NSA_KV_COLS
    return pl.pallas_call(
        _compress_kernel,
        grid=(bsz,),
        in_specs=[pl.BlockSpec((1, rows, width), lambda b: (b, 0, 0)),
                  *[_layer_spec(a, layer) for a in (pe_hb, w1_big, b1p, w2k, b2k, w2vt, b2v)]],
        out_specs=[pl.BlockSpec((1, rows, gk), lambda b: (b, 0, 0)),
                   pl.BlockSpec((1, gk, rows), lambda b: (b, 0, 0))],
        out_shape=[jax.ShapeDtypeStruct((bsz, rows, gk), BF16),
                   jax.ShapeDtypeStruct((bsz, gk, rows), BF16)],
        compiler_params=_params(("parallel",)),
        name="compress",
    )(hb, pe_hb, w1_big, b1p, w2k, b2k, w2vt, b2v)


def _bf16_pieces(x):
    out = []
    rest = np.asarray(x, np.float32)
    for _ in range(SLOPE_PIECES):
        piece = rest.astype(BF16).astype(np.float32)
        out.append(piece)
        rest = rest - piece
    return out


def _slope_rows(slopes, heads_per_block, cols_per_head):
    sl2 = (np.asarray(slopes, np.float32).astype(np.float64) * LOG2E).astype(np.float32)
    pieces = np.stack(_bf16_pieces(sl2) * 2, axis=0)
    rows = np.zeros((ALIBI_ROWS, sl2.shape[0]), np.float32)
    rows[:pieces.shape[0]] = pieces
    rows[PAD_ROW] = -MASK_BIG
    rows = np.repeat(rows, cols_per_head, axis=1)
    rows = rows.reshape(ALIBI_ROWS, -1, heads_per_block * cols_per_head).transpose(1, 0, 2)
    return jnp.asarray(rows, BF16)


def _key_aug_tables(seq, nb):
    pos = np.arange(seq)
    aug = np.zeros((2, seq + CH, AUG_LANES), np.float32)
    aug[:, :seq, 0:SLOPE_PIECES] = (pos % CH)[None, :, None]
    aug[:, :seq, SLOPE_PIECES:2 * SLOPE_PIECES] = (pos // CH * CH)[None, :, None]
    aug[0, pos, MASK_ROW0 + pos // SEL_BLOCK] = 1.0
    aug[:, seq:, PAD_ROW] = 1.0
    return jnp.asarray(aug.reshape(2, seq // CH + 1, CH, AUG_LANES), BF16)


def _cmp_aug_table(n_rows):
    aug = np.zeros((n_rows, AUG_LANES), np.float32)
    aug[:, 0:SLOPE_PIECES] = (np.arange(n_rows) * CMP_STRIDE)[:, None]
    aug[:, SLOPE_PIECES:2 * SLOPE_PIECES] = CMP_BLOCK - 1
    return jnp.asarray(aug, BF16)


def _tile_lanes(x, reps):
    return jnp.concatenate([x] * reps, axis=1)


def _query_minus_key(reps):
    shape = (CH, reps * CH)
    q_off = jnp.bitwise_and(lax.broadcasted_iota(jnp.int32, shape, 1), CH - 1)
    return q_off - lax.broadcasted_iota(jnp.int32, shape, 0)


def _flash_init(m_ref, acc_ref):
    m_ref[...] = jnp.full(m_ref.shape, NEG_INF, F32)
    acc_ref[...] = jnp.zeros(acc_ref.shape, F32)


def _normalized(acc_ref, g):
    dv = acc_ref.shape[1] - SUM_ROWS
    return acc_ref[g, 0:dv, :] / acc_ref[g, dv:dv + 1, :]


def _chunk_scores(k_blk, aug_blk, qa_ref):
    return _dot(jnp.concatenate([k_blk, aug_blk], axis=1), qa_ref[...])


def _stage_and_consume(prod, cons, qa_ref, block_cols=BLOCK_COLS):
    if prod is not None:
        k_blk, aug_blk, (ps_ref, pmx_ref), pmask = prod
        k_full = jnp.concatenate([k_blk, aug_blk], axis=1)
    if cons is not None:
        (cs_ref, cmx_ref), v_t, (m_ref, acc_ref), cmask = cons
        _, rows, gcols = acc_ref.shape
        dv = rows - SUM_ROWS
        ones = jnp.ones((SUM_ROWS, v_t.shape[1]), BF16)
    for c0 in range(0, qa_ref.shape[1], block_cols):
        csl = slice(c0, c0 + block_cols)
        if prod is not None:
            s = _dot(k_full, qa_ref[:, csl])
            if pmask is not None:
                s = jnp.where(pmask(c0), s, NEG_INF)
            ps_ref[:, csl] = s
            pmx_ref[:, csl] = jnp.max(s, axis=0, keepdims=True)
        if cons is not None:
            s = cs_ref[:, csl]
            if cmask is None:
                mx = cmx_ref[:, csl]
            else:
                s = jnp.where(cmask(c0), s, NEG_INF)
                mx = jnp.max(s, axis=0, keepdims=True)
            m_prev = m_ref[:, csl]
            m_new = jnp.maximum(m_prev, mx)
            alpha = jnp.exp2(m_prev - m_new)
            p = jnp.exp2(s - m_new).astype(BF16)
            g = c0 // gcols
            gsl = slice(c0 - g * gcols, c0 - g * gcols + block_cols)
            v_ones = jnp.concatenate([v_t[g * dv:(g + 1) * dv], ones], axis=0)
            acc_ref[g, :, gsl] = alpha * acc_ref[g, :, gsl] + _dot(v_ones, p)
            m_ref[:, csl] = m_new


def _consume(buf, v_t, state, qa_ref, mask=None, block_cols=BLOCK_COLS):
    _stage_and_consume(None, (buf, v_t, state, mask), qa_ref, block_cols)


def _pad_or(aug_ref, c, is_pad):
    return aug_ref[jnp.where(is_pad, aug_ref.shape[0] - 1, c)]


def _identity(pos):
    return pos


def _causal_first(n, k_ref, aug_ref, buf0, chunk_at=_identity):
    c = chunk_at(0)
    return (k_ref[c], _pad_or(aug_ref, c, jnp.bitwise_and(n, 1) == 1), buf0, None)


def _causal_pairs(n, k_ref, v_ref, aug_ref, qa_ref, bufs, state, block_cols=BLOCK_COLS, chunk_at=_identity):
    _causal_pairs_multi(n, [(k_ref, v_ref, aug_ref, qa_ref, bufs, state)], block_cols, chunk_at)


def _causal_pairs_multi(n, streams, block_cols=BLOCK_COLS, chunk_at=_identity):
    pad = jnp.bitwise_and(n, 1)

    def pair(k, carry):
        pos = 2 * k - pad
        c0 = chunk_at(jnp.maximum(pos, 0))
        c1 = chunk_at(pos + 1)
        c2 = chunk_at(pos + 2)
        for k_ref, v_ref, aug_ref, qa_ref, (buf0, buf1), state in streams:
            _stage_and_consume((k_ref[c1], aug_ref[c1], buf1, None), (buf0, v_ref[c0], state, None),
                               qa_ref, block_cols)
        for k_ref, v_ref, aug_ref, qa_ref, (buf0, buf1), state in streams:
            _stage_and_consume((k_ref[c2], aug_ref[c2], buf0, None), (buf1, v_ref[c1], state, None),
                               qa_ref, block_cols)
        return carry
    lax.fori_loop(0, jnp.right_shift(n + pad, 1), pair, 0)


def _unselected_mask_rows(imp, k_sel):
    n_blk, width = imp.shape
    rows_per = 8
    j_loc = lax.broadcasted_iota(jnp.int32, (rows_per, width), 0)
    mask_blocks = []
    for r0 in range(0, n_blk, rows_per):
        blk = imp[r0:r0 + rows_per, :]
        cnt = jnp.zeros((rows_per, width), jnp.int32)
        for jp in range(n_blk):
            row = imp[jp:jp + 1, :]
            gt = jnp.where(row > blk, 1, 0)
            ge = jnp.where(row >= blk, 1, 0)
            if jp >= r0 + rows_per - 1:
                cnt = cnt + gt
            elif jp < r0:
                cnt = cnt + ge
            else:
                cnt = cnt + jnp.where(j_loc + r0 > jp, ge, gt)
        mask_blocks.append(jnp.where(cnt < k_sel, 0.0, -MASK_BIG))
    return jnp.concatenate(mask_blocks, axis=0)


def _nsa_kernel(q_ref, srow_ref, ks_ref, kw_ref, vs_ref, vw_ref, kaug_ref, kc_ref, kcaug_ref, vc_ref,
                ng_ref, asel_ref, o_ref, qa_ref, ocmp_ref, s0, x0, s1, x1, w0, y0, w1, y1, w2, y2,
                m_s, acc_s, m_w, acc_w, chunk_list):
    i = pl.program_id(1)
    t0 = i * CH
    nh = NSA_GROUP_SIZE
    ng = NSA_KV_GROUPS
    gcols = nh * CH
    cols = ng * gcols
    n_cmp = kc_ref.shape[1]
    n_blk = asel_ref.shape[0]
    k_sel = min(SEL_TOPK, n_blk)
    bufs_s = ((s0, x0), (s1, x1))
    bufs_w = ((w0, y0), (w1, y1), (w2, y2))
    state_s = (m_s, acc_s)
    state_w = (m_w, acc_w)
    kaug_s = kaug_ref.at[0]
    kaug_w = kaug_ref.at[1]

    t_pos = t0 + lax.broadcasted_iota(jnp.int32, (1, CH), 1)
    d0_i = _query_minus_key(BLOCK_COLS // CH)
    causal = lambda c0: d0_i >= 0
    window_edge = lambda c0: d0_i < 0

    zeros_q = jnp.zeros((HEAD_DIM, CH), BF16)
    for g in range(ng):
        for hh in range(nh):
            h = g * nh + hh
            qh = q_ref[0, h * HEAD_DIM:(h + 1) * HEAD_DIM, :]
            for gg in range(ng):
                qa_ref[gg * HEAD_DIM:(gg + 1) * HEAD_DIM, h * CH:(h + 1) * CH] = qh if gg == g else zeros_q
    qa_ref[QK_LANES:QK_LANES + ALIBI_ROWS, :] = srow_ref[...]
    qa_ref[QK_LANES + MASK_ROW0:, :] = jnp.zeros((AUG_LANES - MASK_ROW0, cols), BF16)

    cmp_scores = _chunk_scores(kc_ref[0], kcaug_ref[...], qa_ref)

    n_back = WINDOW // CH
    stage_w = []
    chunks_w = []
    for back in range(n_back, 0, -1):
        c = jnp.maximum(i - back, 0)
        stage_w.append((kw_ref[c], _pad_or(kaug_w, c, i < back), bufs_w[n_back - back],
                        window_edge if back == n_back else None))
        chunks_w.append(c)
    stage_w.append((kw_ref[i], kaug_w[i], bufs_w[n_back], causal))
    chunks_w.append(i)
    _stage_and_consume(stage_w[0], None, qa_ref)

    n_idx = lax.broadcasted_iota(jnp.int32, (n_cmp, cols), 0)
    t_pos_all = t0 + jnp.bitwise_and(lax.broadcasted_iota(jnp.int32, (1, cols), 1), CH - 1)
    valid_c = n_idx * CMP_STRIDE + (CMP_BLOCK - 1) <= t_pos_all
    lg = jnp.where(valid_c, cmp_scores, NEG_INF)
    m = jnp.max(lg, axis=0, keepdims=True)
    p = jnp.where(valid_c, jnp.exp2(lg - m), 0.0)
    l = jnp.sum(p, axis=0, keepdims=True)
    pc = p * jnp.where(l > 0.0, 1.0 / l, 0.0)
    pc_b = pc.astype(BF16)
    vc = vc_ref[0]

    j_idx = lax.broadcasted_iota(jnp.int32, (n_blk, CH), 0)
    cur = jnp.right_shift(t_pos, SEL_BLOCK.bit_length() - 1)
    forced = (j_idx == 0) | (j_idx == cur) | (j_idx == cur - 1)
    in_past = j_idx * SEL_BLOCK <= t_pos
    for g in range(ng):
        gsl = slice(g * gcols, (g + 1) * gcols)
        ocmp_ref[g] = _dot(vc[g * HEAD_DIM:(g + 1) * HEAD_DIM], pc_b[:, gsl])
        psum = pc[:, g * gcols:g * gcols + CH]
        for hh in range(1, nh):
            psum = psum + pc[:, g * gcols + hh * CH:g * gcols + (hh + 1) * CH]
        imp = jnp.dot(asel_ref[...], psum, precision=lax.Precision.HIGHEST, preferred_element_type=F32)
        imp = jnp.where(in_past, jnp.where(forced, FORCED_SCORE, imp), -1.0)
        mask_rows = _unselected_mask_rows(imp, k_sel)
        kept = mask_rows if g == 0 else jnp.maximum(kept, mask_rows)
        qa_ref[QK_LANES + MASK_ROW0:QK_LANES + MASK_ROW0 + n_blk, gsl] = _tile_lanes(mask_rows.astype(BF16), nh)

    blocks_per_chunk = CH // SEL_BLOCK
    n_sel = jnp.int32(0)
    for c in range(n_blk // blocks_per_chunk):
        hit = jnp.max(kept[c * blocks_per_chunk:(c + 1) * blocks_per_chunk, :]) > -1.0
        chunk_list[n_sel] = c
        n_sel = n_sel + jnp.logical_and(hit, c < i).astype(jnp.int32)
    chunk_list[n_sel] = i

    def chunk_at(pos):
        return chunk_list[pos]

    _flash_init(*state_w)
    for j in range(1, n_back + 1):
        _stage_and_consume(stage_w[j], (bufs_w[j - 1], vw_ref[chunks_w[j - 1]], state_w, None), qa_ref)
    _stage_and_consume(_causal_first(n_sel, ks_ref, kaug_s, bufs_s[0], chunk_at),
                       (bufs_w[n_back], vw_ref[i], state_w, None), qa_ref)
    _flash_init(*state_s)
    _causal_pairs(n_sel, ks_ref, vs_ref, kaug_s, qa_ref, bufs_s, state_s, chunk_at=chunk_at)
    _consume(bufs_s[0], vs_ref[i], state_s, qa_ref, mask=causal)

    for g in range(ng):
        def gate(branch):
            r0 = g * GATE_ROWS_PER_GROUP + branch * nh
            return jax.nn.sigmoid(jnp.concatenate([ng_ref[0, r0 + hh:r0 + hh + 1, :] for hh in range(nh)], axis=1))
        out = gate(0) * ocmp_ref[g] + gate(1) * _normalized(acc_s, g) + gate(2) * _normalized(acc_w, g)
        for hh in range(nh):
            h = g * nh + hh
            o_ref[0, h * HEAD_DIM:(h + 1) * HEAD_DIM, :] = out[:, hh * CH:(hh + 1) * CH].astype(o_ref.dtype)


def _nsa(nq_t, srows, ks3, kw3, vs_t, vw_t, kaug, kc, kcaug, vc_t, ng_t, asel_t, bsz, nq):
    n_cmp = kc.shape[1]
    ng = NSA_KV_GROUPS
    gcols = NSA_GROUP_SIZE * CH
    cols = ng * gcols
    acc_shape = (ng, HEAD_DIM + SUM_ROWS, gcols)
    stage = [pltpu.VMEM((CH, cols), F32), pltpu.VMEM((1, cols), F32)]
    return pl.pallas_call(
        _nsa_kernel,
        grid=(bsz, nq),
        in_specs=[
            pl.BlockSpec((1, NSA_Q_COLS, CH), lambda b, i: (b * nq + i, 0, 0)),
            pl.BlockSpec(srows.shape, lambda b, i: (0, 0)),
            pl.BlockSpec((nq, CH, NSA_KV_COLS), lambda b, i: (b, 0, 0)),
            pl.BlockSpec((nq, CH, NSA_KV_COLS), lambda b, i: (b, 0, 0)),
            pl.BlockSpec((nq, NSA_KV_COLS, CH), lambda b, i: (b, 0, 0)),
            pl.BlockSpec((nq, NSA_KV_COLS, CH), lambda b, i: (b, 0, 0)),
            pl.BlockSpec(kaug.shape, lambda b, i: (0, 0, 0, 0)),
            pl.BlockSpec((1, n_cmp, NSA_KV_COLS), lambda b, i: (b, 0, 0)),
            pl.BlockSpec(kcaug.shape, lambda b, i: (0, 0)),
            pl.BlockSpec((1, NSA_KV_COLS, n_cmp), lambda b, i: (b, 0, 0)),
            pl.BlockSpec((1, ng * GATE_ROWS_PER_GROUP, CH), lambda b, i: (b * nq + i, 0, 0)),
            pl.BlockSpec(asel_t.shape, lambda b, i: (0, 0)),
        ],
        out_specs=pl.BlockSpec((1, NSA_Q_COLS, CH), lambda b, i: (b * nq + i, 0, 0)),
        out_shape=jax.ShapeDtypeStruct((bsz * nq, NSA_Q_COLS, CH), BF16),
        scratch_shapes=(
            [pltpu.VMEM((QK_LANES + AUG_LANES, cols), BF16), pltpu.VMEM((ng, HEAD_DIM, gcols), F32)]
            + stage * 5
            + [pltpu.VMEM((1, cols), F32), pltpu.VMEM(acc_shape, F32),
               pltpu.VMEM((1, cols), F32), pltpu.VMEM(acc_shape, F32),
               pltpu.SMEM((nq + 1,), jnp.int32)]),
        compiler_params=_params(("parallel", "arbitrary")),
        name="nsa_attention",
    )(nq_t, srows, ks3, kw3, vs_t, vw_t, kaug, kc, kcaug, vc_t, ng_t, asel_t)


def _diff_kernel(scal_ref, q_ref, srow_ref, k_ref, v_ref, kaug_ref, lam_ref, gain_ref, o_ref, *scratch):
    nt = DIFF_TILE_CHUNKS
    tq = nt * CH
    hd2 = 2 * HEAD_DIM
    c0 = pl.program_id(2) * nt
    lam_init = scal_ref[0]
    out_scale = scal_ref[1]
    sub2 = lax.broadcasted_iota(jnp.int32, (QK_LANES, CH), 0)
    zero = jnp.zeros((), BF16)
    bc = 2 * tq
    lane = lax.broadcasted_iota(jnp.int32, (CH, bc), 1)
    sub = lax.broadcasted_iota(jnp.int32, (CH, bc), 0)

    def on_or_after(first_key):
        return lambda col0: jnp.bitwise_and(lane + col0, tq - 1) - sub >= first_key

    streams = []
    for hd in range(DIFF_HEADS_PER_STEP):
        qa_ref, s0, x0, s1, x1, m_r, acc_r = scratch[7 * hd:7 * (hd + 1)]
        rows = slice(hd * hd2, (hd + 1) * hd2)
        for j in range(nt):
            q = q_ref[j, rows, :]
            qa_ref[0:QK_LANES, j * CH:(j + 1) * CH] = jnp.where(sub2 < HEAD_DIM, q, zero)
            qa_ref[0:QK_LANES, tq + j * CH:tq + (j + 1) * CH] = jnp.where(sub2 >= HEAD_DIM, q, zero)
        qa_ref[QK_LANES:QK_LANES + ALIBI_ROWS, :] = srow_ref[hd]
        qa_ref[QK_LANES + MASK_ROW0:, :] = jnp.zeros((AUG_LANES - MASK_ROW0, bc), BF16)
        streams.append((k_ref.at[:, :, rows], v_ref.at[:, rows, :], kaug_ref, qa_ref,
                        ((s0, x0), (s1, x1)), (m_r, acc_r)))

    for k_at, v_at, aug, qa_ref, bufs, state in streams:
        _stage_and_consume(_causal_first(c0, k_at, aug, bufs[0]), None, qa_ref, bc)
        _flash_init(*state)
    _causal_pairs_multi(c0, streams, bc)
    for k_at, v_at, aug, qa_ref, bufs, state in streams:
        _stage_and_consume((k_at[c0 + 1], aug[c0 + 1], bufs[1], on_or_after(CH)),
                           (bufs[0], v_at[c0], state, on_or_after(0)), qa_ref, bc)
    for k_at, v_at, aug, qa_ref, bufs, state in streams:
        _consume(bufs[1], v_at[c0 + 1], state, qa_ref, block_cols=bc)

    lp = lam_ref[...]
    lam = (jnp.exp(jnp.sum(lp[0:1] * lp[1:2], axis=1, keepdims=True))
           - jnp.exp(jnp.sum(lp[2:3] * lp[3:4], axis=1, keepdims=True)) + lam_init)
    for hd, stream in enumerate(streams):
        att = _normalized(stream[5][1], 0)
        o = att[:, 0:tq] - lam * att[:, tq:2 * tq]
        o = o * lax.rsqrt(jnp.mean(o * o, axis=0, keepdims=True) + SUBLN_EPS) * gain_ref[...]
        o = (o * out_scale).astype(o_ref.dtype)
        for j in range(nt):
            o_ref[j, hd * hd2:(hd + 1) * hd2, :] = o[:, j * CH:(j + 1) * CH]


def _diff(scal, dq_t, srows, dk3, dv_t, kaug_plain, lam_p, gain, bsz, nq):
    hd2 = 2 * HEAD_DIM
    nt = DIFF_TILE_CHUNKS
    n_tiles = nq // nt
    cols = 2 * nt * CH
    hps = DIFF_HEADS_PER_STEP
    rows = hps * hd2
    per_head = [
        pltpu.VMEM((QK_LANES + AUG_LANES, cols), BF16),
        pltpu.VMEM((CH, cols), F32), pltpu.VMEM((1, cols), F32),
        pltpu.VMEM((CH, cols), F32), pltpu.VMEM((1, cols), F32),
        pltpu.VMEM((1, cols), F32), pltpu.VMEM((1, hd2 + SUM_ROWS, cols), F32),
    ]
    grid_spec = pltpu.PrefetchScalarGridSpec(
        num_scalar_prefetch=1,
        grid=(bsz, DIFF_HEADS // hps, n_tiles),
        in_specs=[
            pl.BlockSpec((nt, rows, CH), lambda b, h, i, s: (b * n_tiles + i, h, 0)),
            pl.BlockSpec((hps, ALIBI_ROWS, cols), lambda b, h, i, s: (h, 0, 0)),
            pl.BlockSpec((nq, CH, rows), lambda b, h, i, s: (b, 0, h)),
            pl.BlockSpec((nq, rows, CH), lambda b, h, i, s: (b, h, 0)),
            pl.BlockSpec(kaug_plain.shape, lambda b, h, i, s: (0, 0, 0)),
            pl.BlockSpec(lam_p.shape, lambda b, h, i, s: (0, 0)),
            pl.BlockSpec(gain.shape, lambda b, h, i, s: (0, 0)),
        ],
        out_specs=pl.BlockSpec((nt, rows, CH), lambda b, h, i, s: (b * n_tiles + i, h, 0)),
        scratch_shapes=per_head * hps,
    )
    return pl.pallas_call(
        _diff_kernel,
        grid_spec=grid_spec,
        out_shape=jax.ShapeDtypeStruct((bsz * nq, DIFF_V_COLS, CH), BF16),
        compiler_params=_params(("parallel", "parallel", "arbitrary")),
        name="diff_attention",
    )(scal, dq_t, srows, dk3, dv_t, kaug_plain, lam_p, gain)


def _merge_kernel(ya_ref, yb_ref, gates_ref, h_ref, wa_ref, wb_ref, wo_ref, o_ref):
    n_sub = ya_ref.shape[0]
    for j0 in range(0, n_sub, MERGE_SUB):
        rows = slice(j0 * CH, (j0 + MERGE_SUB) * CH)
        ya = jnp.concatenate([ya_ref[j] for j in range(j0, j0 + MERGE_SUB)], axis=1)
        yb = jnp.concatenate([yb_ref[j] for j in range(j0, j0 + MERGE_SUB)], axis=1)
        a = lax.dot_general(ya, wa_ref[...], _TN, preferred_element_type=F32)
        b = lax.dot_general(yb, wb_ref[...], _TN, preferred_element_type=F32)
        ga = jax.nn.sigmoid(gates_ref[rows, :D_MODEL].astype(F32))
        gb = jax.nn.sigmoid(gates_ref[rows, D_MODEL:].astype(F32))
        merged = (ga * a + gb * b).astype(BF16)
        o_ref[rows, :] = h_ref[rows, :] + _dot(merged, wo_ref[...])


def _merge(ya_t, yb_t, gates, h2d, wa, wb, wo, layer):
    n = h2d.shape[0]
    tm = ROW_TILE
    return pl.pallas_call(
        _merge_kernel,
        grid=(n // tm,),
        in_specs=[pl.BlockSpec((tm // CH, NSA_Q_COLS, CH), lambda i: (i, 0, 0)),
                  pl.BlockSpec((tm // CH, DIFF_V_COLS, CH), lambda i: (i, 0, 0)),
                  pl.BlockSpec((tm, 2 * D_MODEL), lambda i: (i, 0)),
                  pl.BlockSpec((tm, D_MODEL), lambda i: (i, 0)),
                  _layer_spec(wa, layer), _layer_spec(wb, layer), _layer_spec(wo, layer)],
        out_specs=pl.BlockSpec((tm, D_MODEL), lambda i: (i, 0)),
        out_shape=jax.ShapeDtypeStruct((n, D_MODEL), F32),
        compiler_params=_params(("parallel",)),
        name="merge_outproj",
    )(ya_t, yb_t, gates, h2d, wa, wb, wo)


def _first_argmax(x, rows, n):
    mx = jnp.max(x, axis=0, keepdims=True)
    idx = jnp.min(jnp.where(x == mx, rows, n), axis=0, keepdims=True)
    return mx, idx


def _moe_kernel(h_ref, g_ref, wr_ref, br_ref, eexp_ref, wg_ref, wu_ref, wd_ref, fg_ref, o_ref, *, final):
    hres = h_ref[...]
    xf = hres * lax.rsqrt(jnp.mean(hres * hres, axis=-1, keepdims=True) + RMS_EPS) * g_ref[...]
    xb = xf.astype(BF16)
    tm = hres.shape[0]

    x_lo = (xf - xb.astype(F32)).astype(BF16)
    n_r = wr_ref.shape[0] // 2
    part = lax.dot_general(wr_ref[...], xb, _NT, preferred_element_type=F32)
    logits = (part[0:n_r] + part[n_r:2 * n_r]
              + lax.dot_general(wr_ref[0:n_r, :], x_lo, _NT, preferred_element_type=F32)
              + br_ref[...])
    gl = logits[0:MOE_GROUPS]
    rows_g = lax.broadcasted_iota(jnp.int32, (MOE_GROUPS, tm), 0)
    gmax, gidx = _first_argmax(gl, rows_g, MOE_GROUPS)
    g_w = 1.0 / jnp.sum(jnp.exp(gl - gmax), axis=0, keepdims=True)
    esel = jnp.zeros((EXPERTS_PER_GROUP, tm), F32)
    for gg in range(MOE_GROUPS):
        lo = MOE_GROUPS + gg * EXPERTS_PER_GROUP
        esel = jnp.where(gidx == gg, logits[lo:lo + EXPERTS_PER_GROUP], esel)
    rows_e = lax.broadcasted_iota(jnp.int32, (EXPERTS_PER_GROUP, tm), 0)
    v1, i1 = _first_argmax(esel, rows_e, EXPERTS_PER_GROUP)
    rest = jnp.where(rows_e == i1, -jnp.inf, esel)
    v2, i2 = _first_argmax(rest, rows_e, EXPERTS_PER_GROUP)
    e21 = jnp.exp(v2 - v1)
    w1 = g_w / (1.0 + e21)
    w2 = g_w * e21 / (1.0 + e21)
    rows_c = lax.broadcasted_iota(jnp.int32, (N_EXPERTS, tm), 0)
    grp_c = jnp.right_shift(rows_c, EXPERTS_PER_GROUP.bit_length() - 1)
    exp_c = jnp.bitwise_and(rows_c, EXPERTS_PER_GROUP - 1)
    comb = jnp.where(grp_c == gidx,
                     jnp.where(exp_c == i1, w1, 0.0) + jnp.where(exp_c == i2, w2, 0.0), 0.0)
    comb_hi = comb.astype(BF16)
    comb_lo = (comb - comb_hi.astype(F32)).astype(BF16)

    comb_nat = (lax.dot_general(comb_hi, eexp_ref[...], _TN, preferred_element_type=F32)
                + lax.dot_general(comb_lo, eexp_ref[...], _TN, preferred_element_type=F32))

    acc = hres
    n_ff = wg_ref.shape[1]
    step = 256
    for c in range(0, n_ff, step):
        cols = slice(c, c + step)
        hg = _dot(xb, wg_ref[:, cols])
        hu = _dot(xb, wu_ref[:, cols])
        act = jax.nn.silu(hg) * hu
        parts = []
        for e0 in range(0, step, EXPERT_FF):
            e = (c + e0) // EXPERT_FF
            parts.append((act[:, e0:e0 + EXPERT_FF] * comb_nat[:, e:e + 1]).astype(BF16))
        acc = acc + _dot(jnp.concatenate(parts, axis=1), wd_ref[cols, :])
    if final:
        acc = acc * lax.rsqrt(jnp.mean(acc * acc, axis=-1, keepdims=True) + RMS_EPS) * fg_ref[...]
    o_ref[...] = acc


def _moe(h2d, g, wr_t, br, eexp, wg, wu, wd, fg, layer, final):
    n = h2d.shape[0]
    tm = ROW_TILE
    return pl.pallas_call(
        functools.partial(_moe_kernel, final=final),
        grid=(n // tm,),
        in_specs=[pl.BlockSpec((tm, D_MODEL), lambda i: (i, 0)),
                  _layer_spec(g, layer), _layer_spec(wr_t, layer), _layer_spec(br, layer),
                  _const_spec(eexp.shape), _layer_spec(wg, layer), _layer_spec(wu, layer),
                  _layer_spec(wd, layer), _const_spec(fg.shape)],
        out_specs=pl.BlockSpec((tm, D_MODEL), lambda i: (i, 0)),
        out_shape=jax.ShapeDtypeStruct((n, D_MODEL), F32),
        compiler_params=_params(("parallel",)),
        name="moe_final" if final else "moe",
    )(h2d, g, wr_t, br, eexp, wg, wu, wd, fg)


def _split_points():
    sizes = ([NSA_Q_COLS] + [NSA_KV_COLS] * 6
             + [NSA_GATE_COLS, DIFF_QK_COLS, DIFF_QK_COLS, DIFF_V_COLS, D_MODEL, D_MODEL])
    return [int(v) for v in np.cumsum(sizes)[:-1]]


def _cmp_to_sel_t(n_rows, nc, nb):
    c0 = np.arange(nc)[:, None] * CMP_STRIDE
    s0 = np.arange(nb)[None, :] * SEL_BLOCK
    ov = np.maximum(0, np.minimum(c0 + CMP_BLOCK, s0 + SEL_BLOCK) - np.maximum(c0, s0)) / CMP_BLOCK
    out = np.zeros((nb, n_rows), np.float32)
    out[:, :nc] = ov.T
    return out


def kernel(x, norm1_g, w_in, cmp_pe, cmp_w1, cmp_b1, cmp_w2, cmp_b2, diff_lambda, diff_subln_g, w_branch_a, w_branch_b, w_out, norm2_g, router_grp_w, router_grp_b, router_exp_w, router_exp_b, exp_w_gate, exp_w_up, exp_w_down, final_norm_g):
    bsz, seq, d = x.shape
    depth = w_in.shape[0]
    n = bsz * seq
    nq = seq // CH
    n_half = seq // CMP_STRIDE
    nc = (seq - CMP_BLOCK) // CMP_STRIDE + 1
    nb = seq // SEL_BLOCK
    assert d == D_MODEL and seq % ROW_TILE == 0 and seq >= WINDOW and WINDOW % CH == 0
    assert nb % 16 == 0 and MASK_ROW0 + nb <= AUG_LANES and n_half <= 256
    g_kv = NSA_KV_GROUPS
    eye_g = jnp.eye(g_kv, dtype=F32)

    (nq_w, kc_w, vc_w, ks_w, vs_w, kw_w, vw_w, ng_w, dq_w, dk_w, dv_w, ga_w, gb_w) = jnp.split(
        w_in, _split_points(), axis=-1)
    ng_w = ng_w.reshape(depth, d, g_kv, NSA_GROUP_SIZE, 3).transpose(0, 1, 2, 4, 3)
    ng_w = ng_w.reshape(depth, d, g_kv, 3 * NSA_GROUP_SIZE)
    ng_w = jnp.pad(ng_w, ((0, 0), (0, 0), (0, 0), (0, GATE_ROWS_PER_GROUP - 3 * NSA_GROUP_SIZE)))
    ng_w = ng_w.reshape(depth, d, g_kv * GATE_ROWS_PER_GROUP)
    wn_all = jnp.concatenate([ks_w, kw_w, dk_w, kc_w, vc_w, ga_w, gb_w], axis=-1).astype(BF16)
    wt_all = jnp.concatenate([nq_w, dq_w, vs_w, vw_w, dv_w, ng_w], axis=-1)
    wt_all = jnp.swapaxes(wt_all, 1, 2).astype(BF16)

    w1r = cmp_w1.reshape(depth, 2, 2, CMP_STRIDE, HEAD_DIM, CMP_HIDDEN).astype(BF16)
    w1_big = jnp.zeros((depth, 2, CMP_STRIDE, 2 * NSA_KV_COLS, 2 * g_kv * CMP_HIDDEN), BF16)
    for c in range(2):
        for g in range(g_kv):
            r0 = c * NSA_KV_COLS + g * HEAD_DIM
            c0 = (c * g_kv + g) * CMP_HIDDEN
            w1_big = w1_big.at[:, :, :, r0:r0 + HEAD_DIM, c0:c0 + CMP_HIDDEN].set(w1r[:, c])
    w1_big = w1_big.reshape(depth, 2, CMP_STRIDE * 2 * NSA_KV_COLS, 2 * g_kv * CMP_HIDDEN)
    per = cmp_pe.reshape(depth, 2, 2, CMP_STRIDE, HEAD_DIM)
    pe_hb = jnp.einsum('Lchld,g->Lhlcgd', per, jnp.ones((g_kv,), F32))
    pe_hb = pe_hb.reshape(depth, 2, 1, CMP_STRIDE * 2 * NSA_KV_COLS)
    b1p = jnp.broadcast_to(cmp_b1[:, :, None, :], (depth, 2, g_kv, CMP_HIDDEN)).reshape(depth, 1, -1)
    w2k = jnp.einsum('Lfd,gG->LgfGd', cmp_w2[:, 0], eye_g).reshape(depth, g_kv * CMP_HIDDEN, NSA_KV_COLS)
    w2vt = jnp.einsum('Lfd,gG->LGdgf', cmp_w2[:, 1], eye_g).reshape(depth, NSA_KV_COLS, g_kv * CMP_HIDDEN)
    w2k = w2k.astype(BF16)
    w2vt = w2vt.astype(BF16)
    b2k = jnp.tile(cmp_b2[:, 0], (1, g_kv))[:, None, :]
    b2v = jnp.tile(cmp_b2[:, 1], (1, g_kv))[:, :, None]

    slopes = _alibi_slopes()
    nsa_srows = _slope_rows(slopes[:NSA_HEADS], NSA_HEADS, CH)[0]
    diff_srows = _slope_rows(np.repeat(slopes[NSA_HEADS:], 2), 2, DIFF_TILE_CHUNKS * CH)
    kaug = _key_aug_tables(seq, nb)
    kcaug = _cmp_aug_table(n_half)
    asel_t = jnp.asarray(_cmp_to_sel_t(n_half, nc, nb))

    wa_all = w_branch_a.astype(BF16)
    wb_all = w_branch_b.astype(BF16)
    wo_all = w_out.astype(BF16)

    wr = jnp.concatenate([router_grp_w, router_exp_w.reshape(depth, d, N_EXPERTS)], axis=-1)
    n_r = MOE_GROUPS + N_EXPERTS
    wr_t = jnp.pad(jnp.swapaxes(wr, 1, 2), ((0, 0), (0, 32 - n_r), (0, 0)))
    wr_hi = wr_t.astype(BF16)
    wr_t = jnp.concatenate([wr_hi, (wr_t - wr_hi.astype(F32)).astype(BF16)], axis=1)
    br = jnp.concatenate([router_grp_b, router_exp_b.reshape(depth, N_EXPERTS)], axis=-1)
    br = jnp.pad(br, ((0, 0), (0, 32 - n_r)))[:, :, None]
    eexp = jnp.asarray(np.eye(N_EXPERTS, 128, dtype=np.float32), BF16)
    wg_all = jnp.swapaxes(exp_w_gate, 1, 2).reshape(depth, d, N_EXPERTS * EXPERT_FF).astype(BF16)
    wu_all = jnp.swapaxes(exp_w_up, 1, 2).reshape(depth, d, N_EXPERTS * EXPERT_FF).astype(BF16)
    wd_all = exp_w_down.reshape(depth, N_EXPERTS * EXPERT_FF, d).astype(BF16)

    h = x.reshape(n, d)
    for l in range(depth):
        (ks, kw, dk, kcvc_hb, gates, nq_t, dq_t, vs_t, vw_t, dv_t, ng_t) = _inproj(
            h, norm1_g[:, None, :], wn_all, wt_all, l)
        hb = kcvc_hb.reshape(bsz, n_half, CMP_STRIDE * 2 * NSA_KV_COLS)
        kc, vc_t = _compress(hb, pe_hb, w1_big, b1p, w2k, b2k, w2vt, b2v, l)
        ya_t = _nsa(nq_t, nsa_srows, ks.reshape(n // CH, CH, NSA_KV_COLS), kw.reshape(n // CH, CH, NSA_KV_COLS),
                    vs_t, vw_t, kaug, kc, kcaug, vc_t, ng_t, asel_t, bsz, nq)
        lam_init = 0.8 - 0.6 * float(np.exp(-0.3 * l))
        scal = jnp.asarray([lam_init, 1.0 - lam_init], F32)
        yb_t = _diff(scal, dq_t, diff_srows, dk.reshape(n // CH, CH, DIFF_QK_COLS), dv_t, kaug[1],
                     diff_lambda[l], diff_subln_g[l][:, None], bsz, nq)
        h = _merge(ya_t, yb_t, gates, h, wa_all, wb_all, wo_all, l)
        h = _moe(h, norm2_g[:, None, :], wr_t, br, eexp, wg_all, wu_all, wd_all,
                 final_norm_g[None, :], l, final=(l == depth - 1))
    return h.reshape(bsz, seq, d)
```

```python
import functools

import numpy as np
import jax
import jax.numpy as jnp
from jax import lax
from jax.experimental import pallas as pl
from jax.experimental.pallas import tpu as pltpu

F32 = jnp.float32
BF16 = jnp.bfloat16

D_MODEL = 1024
HEAD_DIM = 64
NSA_HEADS = 8
NSA_KV_GROUPS = 2
NSA_GROUP_SIZE = NSA_HEADS // NSA_KV_GROUPS
CMP_BLOCK = 32
CMP_STRIDE = 16
CMP_HIDDEN = 128
SEL_BLOCK = 64
SEL_TOPK = 8
WINDOW = 512
FORCED_SCORE = 1e9
DIFF_HEADS = 4
MOE_GROUPS = 4
EXPERTS_PER_GROUP = 4
N_EXPERTS = MOE_GROUPS * EXPERTS_PER_GROUP
EXPERT_FF = D_MODEL // 8
RMS_EPS = 1e-6
SUBLN_EPS = 1e-5
NEG_INF = -1e30
N_ALIBI_HEADS = NSA_HEADS + DIFF_HEADS

NSA_Q_COLS = NSA_HEADS * HEAD_DIM
NSA_KV_COLS = NSA_KV_GROUPS * HEAD_DIM
NSA_GATE_COLS = 3 * NSA_HEADS
DIFF_QK_COLS = DIFF_HEADS * 2 * HEAD_DIM
DIFF_V_COLS = DIFF_HEADS * 2 * HEAD_DIM
GATE_ROWS_PER_GROUP = 16

CH = 256
DIFF_TILE_CHUNKS = 2
DIFF_HEADS_PER_STEP = 4
ROW_TILE = 1024
MERGE_SUB = 2
VMEM_LIMIT = 56 * 1024 * 1024

LOG2E = float(np.log2(np.e))
Q_SCALE = HEAD_DIM ** -0.5 * LOG2E

QK_LANES = 2 * HEAD_DIM
AUG_LANES = 128
SLOPE_PIECES = 3
PAD_ROW = 2 * SLOPE_PIECES
ALIBI_ROWS = 16
MASK_ROW0 = ALIBI_ROWS
MASK_BIG = 1e30
SUM_ROWS = 16
BLOCK_COLS = 256

_NT = (((1,), (1,)), ((), ()))
_TN = (((0,), (0,)), ((), ()))


def _dot(a, b):
    return jnp.dot(a, b, preferred_element_type=F32)


def _const_spec(shape):
    nd = len(shape)
    return pl.BlockSpec(shape, lambda *_: (0,) * nd, pipeline_mode=pl.Buffered(1))


def _layer_spec(stacked, layer):
    nd = stacked.ndim - 1
    return pl.BlockSpec((None,) + tuple(stacked.shape[1:]), lambda *_: (layer,) + (0,) * nd,
                        pipeline_mode=pl.Buffered(1))


def _params(sem):
    return pltpu.CompilerParams(dimension_semantics=sem, vmem_limit_bytes=VMEM_LIMIT)


def _alibi_slopes():
    return 2.0 ** (-8.0 * np.arange(1, N_ALIBI_HEADS + 1) / N_ALIBI_HEADS)


_NAT_WIDTHS = (NSA_KV_COLS, NSA_KV_COLS, DIFF_QK_COLS, 2 * NSA_KV_COLS, 2 * D_MODEL)
_TR_ROWS = (NSA_Q_COLS, DIFF_QK_COLS, NSA_KV_COLS, NSA_KV_COLS, DIFF_V_COLS, 2 * GATE_ROWS_PER_GROUP)
_TR_SCALE = (Q_SCALE, Q_SCALE, 1.0, 1.0, 1.0, 1.0)
_KCVC_INDEX = 3


def _inproj_kernel(x_ref, g_ref, wn_ref, wt_ref, *refs):
    n_out = len(_NAT_WIDTHS) + len(_TR_ROWS)
    nat_refs = refs[:len(_NAT_WIDTHS)]
    tr_refs = refs[len(_NAT_WIDTHS):n_out]
    rows_k, rows_v = refs[n_out:]
    x = x_ref[...]
    xn = (x * lax.rsqrt(jnp.mean(x * x, axis=-1, keepdims=True) + RMS_EPS) * g_ref[...]).astype(BF16)
    off = 0
    for idx, (ref, width) in enumerate(zip(nat_refs, _NAT_WIDTHS)):
        if idx == _KCVC_INDEX:
            res = _dot(xn, wn_ref[:, off:off + width])
            rows_k[...] = res[:, :NSA_KV_COLS]
            rows_v[...] = res[:, NSA_KV_COLS:]
            n_rows = res.shape[0] // CMP_STRIDE
            for tok in range(CMP_STRIDE):
                lo = tok * width
                ref[:, lo:lo + NSA_KV_COLS] = rows_k[pl.ds(tok, n_rows, stride=CMP_STRIDE), :]
                ref[:, lo + NSA_KV_COLS:lo + width] = rows_v[pl.ds(tok, n_rows, stride=CMP_STRIDE), :]
        else:
            for c in range(0, width, 512):
                cw = min(512, width - c)
                ref[:, c:c + cw] = _dot(xn, wn_ref[:, off + c:off + c + cw]).astype(ref.dtype)
        off += width
    n_sub = x.shape[0] // CH
    off = 0
    for ref, rows, scale in zip(tr_refs, _TR_ROWS, _TR_SCALE):
        for c in range(0, rows, 512):
            rw = min(512, rows - c)
            res = lax.dot_general(wt_ref[off + c:off + c + rw, :], xn, _NT, preferred_element_type=F32)
            if scale != 1.0:
                res = res * scale
            for j in range(n_sub):
                ref[j, c:c + rw, :] = res[:, j * CH:(j + 1) * CH].astype(ref.dtype)
        off += rows


def _inproj(h2d, g, wn, wt, layer):
    n = h2d.shape[0]
    tm = ROW_TILE
    nat_dtypes = (BF16, BF16, BF16, F32, BF16)
    tr_dtypes = (BF16, BF16, BF16, BF16, BF16, F32)
    out_shape = [jax.ShapeDtypeStruct((n, w), dt) for w, dt in zip(_NAT_WIDTHS, nat_dtypes)]
    out_shape += [jax.ShapeDtypeStruct((n // CH, r, CH), dt) for r, dt in zip(_TR_ROWS, tr_dtypes)]
    out_specs = [pl.BlockSpec((tm, w), lambda i: (i, 0)) for w in _NAT_WIDTHS]
    out_specs += [pl.BlockSpec((tm // CH, r, CH), lambda i: (i, 0, 0)) for r in _TR_ROWS]
    hb_width = CMP_STRIDE * _NAT_WIDTHS[_KCVC_INDEX]
    out_shape[_KCVC_INDEX] = jax.ShapeDtypeStruct((n // CMP_STRIDE, hb_width), F32)
    out_specs[_KCVC_INDEX] = pl.BlockSpec((tm // CMP_STRIDE, hb_width), lambda i: (i, 0))
    return pl.pallas_call(
        _inproj_kernel,
        grid=(n // tm,),
        in_specs=[pl.BlockSpec((tm, D_MODEL), lambda i: (i, 0)),
                  _layer_spec(g, layer), _layer_spec(wn, layer), _layer_spec(wt, layer)],
        out_specs=out_specs,
        out_shape=out_shape,
        scratch_shapes=[pltpu.VMEM((tm, NSA_KV_COLS), F32), pltpu.VMEM((tm, NSA_KV_COLS), F32)],
        compiler_params=_params(("parallel",)),
        name="inproj",
    )(h2d, g, wn, wt)


def _compress_kernel(hb_ref, pe_ref, w1_ref, b1_ref, w2k_ref, b2k_ref, w2v_ref, b2v_ref, kc_ref, vct_ref):
    hb = hb_ref[0]
    rows = hb.shape[0]
    top = (hb + pe_ref[0]).astype(BF16)
    bot = (hb + pe_ref[1]).astype(BF16)
    p = _dot(top, w1_ref[0])
    q = _dot(bot, w1_ref[1])
    q_next = pltpu.roll(q, rows - 1, 0)
    hid = jax.nn.gelu(p + q_next + b1_ref[...])
    width = hid.shape[1] // 2
    kc_ref[0] = (_dot(hid[:, :width].astype(BF16), w2k_ref[...]) + b2k_ref[...]).astype(kc_ref.dtype)
    vct = lax.dot_general(w2v_ref[...], hid[:, width:].astype(BF16), _NT, preferred_element_type=F32)
    vct_ref[0] = (vct + b2v_ref[...]).astype(vct_ref.dtype)


def _compress(hb, pe_hb, w1_big, b1p, w2k, b2k, w2vt, b2v, layer):
    bsz, rows, width = hb.shape
    gk = NSA_KV_COLS
    return pl.pallas_call(
        _compress_kernel,
        grid=(bsz,),
        in_specs=[pl.BlockSpec((1, rows, width), lambda b: (b, 0, 0)),
                  *[_layer_spec(a, layer) for a in (pe_hb, w1_big, b1p, w2k, b2k, w2vt, b2v)]],
        out_specs=[pl.BlockSpec((1, rows, gk), lambda b: (b, 0, 0)),
                   pl.BlockSpec((1, gk, rows), lambda b: (b, 0, 0))],
        out_shape=[jax.ShapeDtypeStruct((bsz, rows, gk), BF16),
                   jax.ShapeDtypeStruct((bsz, gk, rows), BF16)],
        compiler_params=_params(("parallel",)),
        name="compress",
    )(hb, pe_hb, w1_big, b1p, w2k, b2k, w2vt, b2v)


def _bf16_pieces(x):
    out = []
    rest = np.asarray(x, np.float32)
    for _ in range(SLOPE_PIECES):
        piece = rest.astype(BF16).astype(np.float32)
        out.append(piece)
        rest = rest - piece
    return out


def _slope_rows(slopes, heads_per_block, cols_per_head):
    sl2 = (np.asarray(slopes, np.float32).astype(np.float64) * LOG2E).astype(np.float32)
    pieces = np.stack(_bf16_pieces(sl2) * 2, axis=0)
    rows = np.zeros((ALIBI_ROWS, sl2.shape[0]), np.float32)
    rows[:pieces.shape[0]] = pieces
    rows[PAD_ROW] = -MASK_BIG
    rows = np.repeat(rows, cols_per_head, axis=1)
    rows = rows.reshape(ALIBI_ROWS, -1, heads_per_block * cols_per_head).transpose(1, 0, 2)
    return jnp.asarray(rows, BF16)


def _key_aug_tables(seq, nb):
    pos = np.arange(seq)
    aug = np.zeros((2, seq + CH, AUG_LANES), np.float32)
    aug[:, :seq, 0:SLOPE_PIECES] = (pos % CH)[None, :, None]
    aug[:, :seq, SLOPE_PIECES:2 * SLOPE_PIECES] = (pos // CH * CH)[None, :, None]
    aug[0, pos, MASK_ROW0 + pos // SEL_BLOCK] = 1.0
    aug[:, seq:, PAD_ROW] = 1.0
    return jnp.asarray(aug.reshape(2, seq // CH + 1, CH, AUG_LANES), BF16)


def _cmp_aug_table(n_rows):
    aug = np.zeros((n_rows, AUG_LANES), np.float32)
    aug[:, 0:SLOPE_PIECES] = (np.arange(n_rows) * CMP_STRIDE)[:, None]
    aug[:, SLOPE_PIECES:2 * SLOPE_PIECES] = CMP_BLOCK - 1
    return jnp.asarray(aug, BF16)


def _tile_lanes(x, reps):
    return jnp.concatenate([x] * reps, axis=1)


def _query_minus_key(reps):
    shape = (CH, reps * CH)
    q_off = jnp.bitwise_and(lax.broadcasted_iota(jnp.int32, shape, 1), CH - 1)
    return q_off - lax.broadcasted_iota(jnp.int32, shape, 0)


def _flash_init(m_ref, acc_ref):
    m_ref[...] = jnp.full(m_ref.shape, NEG_INF, F32)
    acc_ref[...] = jnp.zeros(acc_ref.shape, F32)


def _normalized(acc_ref, g):
    dv = acc_ref.shape[1] - SUM_ROWS
    return acc_ref[g, 0:dv, :] * (1.0 / acc_ref[g, dv:dv + 1, :])


def _chunk_scores(k_blk, aug_blk, qa_ref):
    return _dot(jnp.concatenate([k_blk, aug_blk], axis=1), qa_ref[...])


def _stage_and_consume(prod, cons, qa_ref, block_cols=BLOCK_COLS):
    if prod is not None:
        k_blk, aug_blk, (ps_ref, pmx_ref), pmask = prod
        k_full = jnp.concatenate([k_blk, aug_blk], axis=1)
    if cons is not None:
        (cs_ref, cmx_ref), v_t, (m_ref, acc_ref), cmask = cons
        _, rows, gcols = acc_ref.shape
        dv = rows - SUM_ROWS
        ones = jnp.ones((SUM_ROWS, v_t.shape[1]), BF16)
    for c0 in range(0, qa_ref.shape[1], block_cols):
        csl = slice(c0, c0 + block_cols)
        if prod is not None:
            s = _dot(k_full, qa_ref[:, csl])
            if pmask is not None:
                s = jnp.where(pmask(c0), s, NEG_INF)
            ps_ref[:, csl] = s
            pmx_ref[:, csl] = jnp.max(s, axis=0, keepdims=True)
        if cons is not None:
            s = cs_ref[:, csl]
            if cmask is None:
                mx = cmx_ref[:, csl]
            else:
                s = jnp.where(cmask(c0), s, NEG_INF)
                mx = jnp.max(s, axis=0, keepdims=True)
            m_prev = m_ref[:, csl]
            m_new = jnp.maximum(m_prev, mx)
            alpha = jnp.exp2(m_prev - m_new)
            p = jnp.exp2(s - m_new).astype(BF16)
            g = c0 // gcols
            gsl = slice(c0 - g * gcols, c0 - g * gcols + block_cols)
            v_ones = jnp.concatenate([v_t[g * dv:(g + 1) * dv], ones], axis=0)
            acc_ref[g, :, gsl] = alpha * acc_ref[g, :, gsl] + _dot(v_ones, p)
            m_ref[:, csl] = m_new


def _consume(buf, v_t, state, qa_ref, mask=None, block_cols=BLOCK_COLS):
    _stage_and_consume(None, (buf, v_t, state, mask), qa_ref, block_cols)


def _pad_or(aug_ref, c, is_pad):
    return aug_ref[jnp.where(is_pad, aug_ref.shape[0] - 1, c)]


def _identity(pos):
    return pos


def _causal_first(n, k_ref, aug_ref, buf0, chunk_at=_identity):
    c = chunk_at(0)
    return (k_ref[c], _pad_or(aug_ref, c, jnp.bitwise_and(n, 1) == 1), buf0, None)


def _causal_pairs(n, k_ref, v_ref, aug_ref, qa_ref, bufs, state, block_cols=BLOCK_COLS, chunk_at=_identity):
    _causal_pairs_multi(n, [(k_ref, v_ref, aug_ref, qa_ref, bufs, state)], block_cols, chunk_at)


def _causal_pairs_multi(n, streams, block_cols=BLOCK_COLS, chunk_at=_identity):
    pad = jnp.bitwise_and(n, 1)

    def pair(k, carry):
        pos = 2 * k - pad
        c0 = chunk_at(jnp.maximum(pos, 0))
        c1 = chunk_at(pos + 1)
        c2 = chunk_at(pos + 2)
        for k_ref, v_ref, aug_ref, qa_ref, (buf0, buf1), state in streams:
            _stage_and_consume((k_ref[c1], aug_ref[c1], buf1, None), (buf0, v_ref[c0], state, None),
                               qa_ref, block_cols)
        for k_ref, v_ref, aug_ref, qa_ref, (buf0, buf1), state in streams:
            _stage_and_consume((k_ref[c2], aug_ref[c2], buf0, None), (buf1, v_ref[c1], state, None),
                               qa_ref, block_cols)
        return carry
    lax.fori_loop(0, jnp.right_shift(n + pad, 1), pair, 0)


def _unselected_mask_rows(imp, k_sel):
    n_blk, width = imp.shape
    rows_per = 8
    j_loc = lax.broadcasted_iota(jnp.int32, (rows_per, width), 0)
    mask_blocks = []
    for r0 in range(0, n_blk, rows_per):
        blk = imp[r0:r0 + rows_per, :]
        cnt = jnp.zeros((rows_per, width), jnp.int32)
        for jp in range(n_blk):
            row = imp[jp:jp + 1, :]
            gt = jnp.where(row > blk, 1, 0)
            ge = jnp.where(row >= blk, 1, 0)
            if jp >= r0 + rows_per - 1:
                cnt = cnt + gt
            elif jp < r0:
                cnt = cnt + ge
            else:
                cnt = cnt + jnp.where(j_loc + r0 > jp, ge, gt)
        mask_blocks.append(jnp.where(cnt < k_sel, 0.0, -MASK_BIG))
    return jnp.concatenate(mask_blocks, axis=0)


def _nsa_kernel(q_ref, srow_ref, ks_ref, kw_ref, vs_ref, vw_ref, kaug_ref, kc_ref, kcaug_ref, vc_ref,
                ng_ref, asel_ref, o_ref, qa_ref, ocmp_ref, s0, x0, s1, x1, w0, y0, w1, y1, w2, y2,
                m_s, acc_s, m_w, acc_w, chunk_list):
    i = pl.program_id(1)
    t0 = i * CH
    nh = NSA_GROUP_SIZE
    ng = NSA_KV_GROUPS
    gcols = nh * CH
    cols = ng * gcols
    n_cmp = kc_ref.shape[1]
    n_blk = asel_ref.shape[0]
    k_sel = min(SEL_TOPK, n_blk)
    bufs_s = ((s0, x0), (s1, x1))
    bufs_w = ((w0, y0), (w1, y1), (w2, y2))
    state_s = (m_s, acc_s)
    state_w = (m_w, acc_w)
    kaug_s = kaug_ref.at[0]
    kaug_w = kaug_ref.at[1]

    t_pos = t0 + lax.broadcasted_iota(jnp.int32, (1, CH), 1)
    d0_i = _query_minus_key(BLOCK_COLS // CH)
    causal = lambda c0: d0_i >= 0
    window_edge = lambda c0: d0_i < 0

    zeros_q = jnp.zeros((HEAD_DIM, CH), BF16)
    for g in range(ng):
        for hh in range(nh):
            h = g * nh + hh
            qh = q_ref[0, h * HEAD_DIM:(h + 1) * HEAD_DIM, :]
            for gg in range(ng):
                qa_ref[gg * HEAD_DIM:(gg + 1) * HEAD_DIM, h * CH:(h + 1) * CH] = qh if gg == g else zeros_q
    qa_ref[QK_LANES:QK_LANES + ALIBI_ROWS, :] = srow_ref[...]
    qa_ref[QK_LANES + MASK_ROW0:, :] = jnp.zeros((AUG_LANES - MASK_ROW0, cols), BF16)

    cmp_scores = _chunk_scores(kc_ref[0], kcaug_ref[...], qa_ref)

    n_back = WINDOW // CH
    stage_w = []
    chunks_w = []
    for back in range(n_back, 0, -1):
        c = jnp.maximum(i - back, 0)
        stage_w.append((kw_ref[c], _pad_or(kaug_w, c, i < back), bufs_w[n_back - back],
                        window_edge if back == n_back else None))
        chunks_w.append(c)
    stage_w.append((kw_ref[i], kaug_w[i], bufs_w[n_back], causal))
    chunks_w.append(i)
    _stage_and_consume(stage_w[0], None, qa_ref)

    n_idx = lax.broadcasted_iota(jnp.int32, (n_cmp, cols), 0)
    t_pos_all = t0 + jnp.bitwise_and(lax.broadcasted_iota(jnp.int32, (1, cols), 1), CH - 1)
    valid_c = n_idx * CMP_STRIDE + (CMP_BLOCK - 1) <= t_pos_all
    lg = jnp.where(valid_c, cmp_scores, NEG_INF)
    m = jnp.max(lg, axis=0, keepdims=True)
    p = jnp.exp2(lg - m)
    l = jnp.sum(p, axis=0, keepdims=True)
    any_valid = t_pos_all >= CMP_BLOCK - 1
    pc = p * jnp.where(any_valid, 1.0 / l, 0.0)
    pc_b = pc.astype(BF16)
    vc = vc_ref[0]

    j_idx = lax.broadcasted_iota(jnp.int32, (n_blk, CH), 0)
    cur = jnp.right_shift(t_pos, SEL_BLOCK.bit_length() - 1)
    forced = (j_idx == 0) | (j_idx == cur) | (j_idx == cur - 1)
    in_past = j_idx * SEL_BLOCK <= t_pos
    for g in range(ng):
        gsl = slice(g * gcols, (g + 1) * gcols)
        ocmp_ref[g] = _dot(vc[g * HEAD_DIM:(g + 1) * HEAD_DIM], pc_b[:, gsl])
        psum = pc[:, g * gcols:g * gcols + CH]
        for hh in range(1, nh):
            psum = psum + pc[:, g * gcols + hh * CH:g * gcols + (hh + 1) * CH]
        imp = jnp.zeros((n_blk, CH), F32)
        rest = psum
        for _ in range(SLOPE_PIECES):
            piece = rest.astype(BF16)
            imp = imp + _dot(asel_ref[...], piece)
            rest = rest - piece.astype(F32)
        imp = jnp.where(in_past, jnp.where(forced, FORCED_SCORE, imp), -1.0)
        mask_rows = _unselected_mask_rows(imp, k_sel)
        kept = mask_rows if g == 0 else jnp.maximum(kept, mask_rows)
        qa_ref[QK_LANES + MASK_ROW0:QK_LANES + MASK_ROW0 + n_blk, gsl] = _tile_lanes(mask_rows.astype(BF16), nh)

    blocks_per_chunk = CH // SEL_BLOCK
    n_sel = jnp.int32(0)
    for c in range(n_blk // blocks_per_chunk):
        hit = jnp.max(kept[c * blocks_per_chunk:(c + 1) * blocks_per_chunk, :]) > -1.0
        chunk_list[n_sel] = c
        n_sel = n_sel + jnp.logical_and(hit, c < i).astype(jnp.int32)
    chunk_list[n_sel] = i

    def chunk_at(pos):
        return chunk_list[pos]

    _flash_init(*state_w)
    for j in range(1, n_back + 1):
        _stage_and_consume(stage_w[j], (bufs_w[j - 1], vw_ref[chunks_w[j - 1]], state_w, None), qa_ref)
    _stage_and_consume(_causal_first(n_sel, ks_ref, kaug_s, bufs_s[0], chunk_at),
                       (bufs_w[n_back], vw_ref[i], state_w, None), qa_ref)
    _flash_init(*state_s)
    _causal_pairs(n_sel, ks_ref, vs_ref, kaug_s, qa_ref, bufs_s, state_s, chunk_at=chunk_at)
    _consume(bufs_s[0], vs_ref[i], state_s, qa_ref, mask=causal)

    for g in range(ng):
        def gate(branch):
            r0 = g * GATE_ROWS_PER_GROUP + branch * nh
            return jax.nn.sigmoid(jnp.concatenate([ng_ref[0, r0 + hh:r0 + hh + 1, :] for hh in range(nh)], axis=1))
        out = gate(0) * ocmp_ref[g] + gate(1) * _normalized(acc_s, g) + gate(2) * _normalized(acc_w, g)
        for hh in range(nh):
            h = g * nh + hh
            o_ref[0, h * HEAD_DIM:(h + 1) * HEAD_DIM, :] = out[:, hh * CH:(hh + 1) * CH].astype(o_ref.dtype)


def _nsa(nq_t, srows, ks3, kw3, vs_t, vw_t, kaug, kc, kcaug, vc_t, ng_t, asel_t, bsz, nq):
    n_cmp = kc.shape[1]
    ng = NSA_KV_GROUPS
    gcols = NSA_GROUP_SIZE * CH
    cols = ng * gcols
    acc_shape = (ng, HEAD_DIM + SUM_ROWS, gcols)
    stage = [pltpu.VMEM((CH, cols), F32), pltpu.VMEM((1, cols), F32)]
    return pl.pallas_call(
        _nsa_kernel,
        grid=(bsz, nq),
        in_specs=[
            pl.BlockSpec((1, NSA_Q_COLS, CH), lambda b, i: (b * nq + i, 0, 0)),
            pl.BlockSpec(srows.shape, lambda b, i: (0, 0)),
            pl.BlockSpec((nq, CH, NSA_KV_COLS), lambda b, i: (b, 0, 0)),
            pl.BlockSpec((nq, CH, NSA_KV_COLS), lambda b, i: (b, 0, 0)),
            pl.BlockSpec((nq, NSA_KV_COLS, CH), lambda b, i: (b, 0, 0)),
            pl.BlockSpec((nq, NSA_KV_COLS, CH), lambda b, i: (b, 0, 0)),
            pl.BlockSpec(kaug.shape, lambda b, i: (0, 0, 0, 0)),
            pl.BlockSpec((1, n_cmp, NSA_KV_COLS), lambda b, i: (b, 0, 0)),
            pl.BlockSpec(kcaug.shape, lambda b, i: (0, 0)),
            pl.BlockSpec((1, NSA_KV_COLS, n_cmp), lambda b, i: (b, 0, 0)),
            pl.BlockSpec((1, ng * GATE_ROWS_PER_GROUP, CH), lambda b, i: (b * nq + i, 0, 0)),
            pl.BlockSpec(asel_t.shape, lambda b, i: (0, 0)),
        ],
        out_specs=pl.BlockSpec((1, NSA_Q_COLS, CH), lambda b, i: (b * nq + i, 0, 0)),
        out_shape=jax.ShapeDtypeStruct((bsz * nq, NSA_Q_COLS, CH), BF16),
        scratch_shapes=(
            [pltpu.VMEM((QK_LANES + AUG_LANES, cols), BF16), pltpu.VMEM((ng, HEAD_DIM, gcols), F32)]
            + stage * 5
            + [pltpu.VMEM((1, cols), F32), pltpu.VMEM(acc_shape, F32),
               pltpu.VMEM((1, cols), F32), pltpu.VMEM(acc_shape, F32),
               pltpu.SMEM((nq + 1,), jnp.int32)]),
        compiler_params=_params(("parallel", "arbitrary")),
        name="nsa_attention",
    )(nq_t, srows, ks3, kw3, vs_t, vw_t, kaug, kc, kcaug, vc_t, ng_t, asel_t)


def _diff_kernel(scal_ref, q_ref, srow_ref, k_ref, v_ref, kaug_ref, lam_ref, gain_ref, o_ref, *scratch):
    nt = DIFF_TILE_CHUNKS
    tq = nt * CH
    hd2 = 2 * HEAD_DIM
    c0 = pl.program_id(2) * nt
    lam_init = scal_ref[0]
    out_scale = scal_ref[1]
    sub2 = lax.broadcasted_iota(jnp.int32, (QK_LANES, CH), 0)
    zero = jnp.zeros((), BF16)
    bc = 2 * tq
    lane = lax.broadcasted_iota(jnp.int32, (CH, bc), 1)
    sub = lax.broadcasted_iota(jnp.int32, (CH, bc), 0)

    def on_or_after(first_key):
        return lambda col0: jnp.bitwise_and(lane + col0, tq - 1) - sub >= first_key

    streams = []
    for hd in range(DIFF_HEADS_PER_STEP):
        qa_ref, s0, x0, s1, x1, m_r, acc_r = scratch[7 * hd:7 * (hd + 1)]
        rows = slice(hd * hd2, (hd + 1) * hd2)
        for j in range(nt):
            q = q_ref[j, rows, :]
            qa_ref[0:QK_LANES, j * CH:(j + 1) * CH] = jnp.where(sub2 < HEAD_DIM, q, zero)
            qa_ref[0:QK_LANES, tq + j * CH:tq + (j + 1) * CH] = jnp.where(sub2 >= HEAD_DIM, q, zero)
        qa_ref[QK_LANES:QK_LANES + ALIBI_ROWS, :] = srow_ref[hd]
        qa_ref[QK_LANES + MASK_ROW0:, :] = jnp.zeros((AUG_LANES - MASK_ROW0, bc), BF16)
        streams.append((k_ref.at[:, :, rows], v_ref.at[:, rows, :], kaug_ref, qa_ref,
                        ((s0, x0), (s1, x1)), (m_r, acc_r)))

    for k_at, v_at, aug, qa_ref, bufs, state in streams:
        _stage_and_consume(_causal_first(c0, k_at, aug, bufs[0]), None, qa_ref, bc)
        _flash_init(*state)
    _causal_pairs_multi(c0, streams, bc)
    for k_at, v_at, aug, qa_ref, bufs, state in streams:
        _stage_and_consume((k_at[c0 + 1], aug[c0 + 1], bufs[1], on_or_after(CH)),
                           (bufs[0], v_at[c0], state, on_or_after(0)), qa_ref, bc)
    for k_at, v_at, aug, qa_ref, bufs, state in streams:
        _consume(bufs[1], v_at[c0 + 1], state, qa_ref, block_cols=bc)

    lp = lam_ref[...]
    lam = (jnp.exp(jnp.sum(lp[0:1] * lp[1:2], axis=1, keepdims=True))
           - jnp.exp(jnp.sum(lp[2:3] * lp[3:4], axis=1, keepdims=True)) + lam_init)
    for hd, stream in enumerate(streams):
        att = _normalized(stream[5][1], 0)
        o = att[:, 0:tq] - lam * att[:, tq:2 * tq]
        o = o * lax.rsqrt(jnp.mean(o * o, axis=0, keepdims=True) + SUBLN_EPS) * gain_ref[...]
        o = (o * out_scale).astype(o_ref.dtype)
        for j in range(nt):
            o_ref[j, hd * hd2:(hd + 1) * hd2, :] = o[:, j * CH:(j + 1) * CH]


def _diff(scal, dq_t, srows, dk3, dv_t, kaug_plain, lam_p, gain, bsz, nq):
    hd2 = 2 * HEAD_DIM
    nt = DIFF_TILE_CHUNKS
    n_tiles = nq // nt
    cols = 2 * nt * CH
    hps = DIFF_HEADS_PER_STEP
    rows = hps * hd2
    per_head = [
        pltpu.VMEM((QK_LANES + AUG_LANES, cols), BF16),
        pltpu.VMEM((CH, cols), F32), pltpu.VMEM((1, cols), F32),
        pltpu.VMEM((CH, cols), F32), pltpu.VMEM((1, cols), F32),
        pltpu.VMEM((1, cols), F32), pltpu.VMEM((1, hd2 + SUM_ROWS, cols), F32),
    ]
    grid_spec = pltpu.PrefetchScalarGridSpec(
        num_scalar_prefetch=1,
        grid=(bsz, DIFF_HEADS // hps, n_tiles),
        in_specs=[
            pl.BlockSpec((nt, rows, CH), lambda b, h, i, s: (b * n_tiles + i, h, 0)),
            pl.BlockSpec((hps, ALIBI_ROWS, cols), lambda b, h, i, s: (h, 0, 0)),
            pl.BlockSpec((nq, CH, rows), lambda b, h, i, s: (b, 0, h)),
            pl.BlockSpec((nq, rows, CH), lambda b, h, i, s: (b, h, 0)),
            pl.BlockSpec(kaug_plain.shape, lambda b, h, i, s: (0, 0, 0)),
            pl.BlockSpec(lam_p.shape, lambda b, h, i, s: (0, 0)),
            pl.BlockSpec(gain.shape, lambda b, h, i, s: (0, 0)),
        ],
        out_specs=pl.BlockSpec((nt, rows, CH), lambda b, h, i, s: (b * n_tiles + i, h, 0)),
        scratch_shapes=per_head * hps,
    )
    return pl.pallas_call(
        _diff_kernel,
        grid_spec=grid_spec,
        out_shape=jax.ShapeDtypeStruct((bsz * nq, DIFF_V_COLS, CH), BF16),
        compiler_params=_params(("parallel", "parallel", "arbitrary")),
        name="diff_attention",
    )(scal, dq_t, srows, dk3, dv_t, kaug_plain, lam_p, gain)


def _merge_kernel(ya_ref, yb_ref, gates_ref, h_ref, wa_ref, wb_ref, wo_ref, o_ref):
    n_sub = ya_ref.shape[0]
    for j0 in range(0, n_sub, MERGE_SUB):
        rows = slice(j0 * CH, (j0 + MERGE_SUB) * CH)
        ya = jnp.concatenate([ya_ref[j] for j in range(j0, j0 + MERGE_SUB)], axis=1)
        yb = jnp.concatenate([yb_ref[j] for j in range(j0, j0 + MERGE_SUB)], axis=1)
        a = lax.dot_general(ya, wa_ref[...], _TN, preferred_element_type=F32)
        b = lax.dot_general(yb, wb_ref[...], _TN, preferred_element_type=F32)
        ga = jax.nn.sigmoid(gates_ref[rows, :D_MODEL].astype(F32))
        gb = jax.nn.sigmoid(gates_ref[rows, D_MODEL:].astype(F32))
        merged = (ga * a + gb * b).astype(BF16)
        o_ref[rows, :] = h_ref[rows, :] + _dot(merged, wo_ref[...])


def _merge(ya_t, yb_t, gates, h2d, wa, wb, wo, layer):
    n = h2d.shape[0]
    tm = ROW_TILE
    return pl.pallas_call(
        _merge_kernel,
        grid=(n // tm,),
        in_specs=[pl.BlockSpec((tm // CH, NSA_Q_COLS, CH), lambda i: (i, 0, 0)),
                  pl.BlockSpec((tm // CH, DIFF_V_COLS, CH), lambda i: (i, 0, 0)),
                  pl.BlockSpec((tm, 2 * D_MODEL), lambda i: (i, 0)),
                  pl.BlockSpec((tm, D_MODEL), lambda i: (i, 0)),
                  _layer_spec(wa, layer), _layer_spec(wb, layer), _layer_spec(wo, layer)],
        out_specs=pl.BlockSpec((tm, D_MODEL), lambda i: (i, 0)),
        out_shape=jax.ShapeDtypeStruct((n, D_MODEL), F32),
        compiler_params=_params(("parallel",)),
        name="merge_outproj",
    )(ya_t, yb_t, gates, h2d, wa, wb, wo)


def _first_argmax(x, rows, n):
    mx = jnp.max(x, axis=0, keepdims=True)
    idx = jnp.min(jnp.where(x == mx, rows, n), axis=0, keepdims=True)
    return mx, idx


def _moe_kernel(h_ref, g_ref, wr_ref, br_ref, eexp_ref, wg_ref, wu_ref, wd_ref, fg_ref, o_ref, *, final):
    hres = h_ref[...]
    xf = hres * lax.rsqrt(jnp.mean(hres * hres, axis=-1, keepdims=True) + RMS_EPS) * g_ref[...]
    xb = xf.astype(BF16)
    tm = hres.shape[0]

    x_lo = (xf - xb.astype(F32)).astype(BF16)
    n_r = wr_ref.shape[0] // 2
    part = lax.dot_general(wr_ref[...], xb, _NT, preferred_element_type=F32)
    logits = (part[0:n_r] + part[n_r:2 * n_r]
              + lax.dot_general(wr_ref[0:n_r, :], x_lo, _NT, preferred_element_type=F32)
              + br_ref[...])
    gl = logits[0:MOE_GROUPS]
    rows_g = lax.broadcasted_iota(jnp.int32, (MOE_GROUPS, tm), 0)
    gmax, gidx = _first_argmax(gl, rows_g, MOE_GROUPS)
    g_w = 1.0 / jnp.sum(jnp.exp(gl - gmax), axis=0, keepdims=True)
    esel = jnp.zeros((EXPERTS_PER_GROUP, tm), F32)
    for gg in range(MOE_GROUPS):
        lo = MOE_GROUPS + gg * EXPERTS_PER_GROUP
        esel = jnp.where(gidx == gg, logits[lo:lo + EXPERTS_PER_GROUP], esel)
    rows_e = lax.broadcasted_iota(jnp.int32, (EXPERTS_PER_GROUP, tm), 0)
    v1, i1 = _first_argmax(esel, rows_e, EXPERTS_PER_GROUP)
    rest = jnp.where(rows_e == i1, -jnp.inf, esel)
    v2, i2 = _first_argmax(rest, rows_e, EXPERTS_PER_GROUP)
    e21 = jnp.exp(v2 - v1)
    w1 = g_w / (1.0 + e21)
    w2 = g_w * e21 / (1.0 + e21)
    rows_c = lax.broadcasted_iota(jnp.int32, (N_EXPERTS, tm), 0)
    grp_c = jnp.right_shift(rows_c, EXPERTS_PER_GROUP.bit_length() - 1)
    exp_c = jnp.bitwise_and(rows_c, EXPERTS_PER_GROUP - 1)
    comb = jnp.where(grp_c == gidx,
                     jnp.where(exp_c == i1, w1, 0.0) + jnp.where(exp_c == i2, w2, 0.0), 0.0)
    comb_hi = comb.astype(BF16)
    comb_lo = (comb - comb_hi.astype(F32)).astype(BF16)

    comb_nat = (lax.dot_general(comb_hi, eexp_ref[...], _TN, preferred_element_type=F32)
                + lax.dot_general(comb_lo, eexp_ref[...], _TN, preferred_element_type=F32))

    n_ff = wg_ref.shape[1]
    step = 256
    parts = []
    for c in range(0, n_ff, step):
        cols = slice(c, c + step)
        hg = _dot(xb, wg_ref[:, cols])
        hu = _dot(xb, wu_ref[:, cols])
        act = jax.nn.silu(hg) * hu
        for e0 in range(0, step, EXPERT_FF):
            e = (c + e0) // EXPERT_FF
            parts.append((act[:, e0:e0 + EXPERT_FF] * comb_nat[:, e:e + 1]).astype(BF16))
    acc = hres + _dot(jnp.concatenate(parts, axis=1), wd_ref[...])
    if final:
        acc = acc * lax.rsqrt(jnp.mean(acc * acc, axis=-1, keepdims=True) + RMS_EPS) * fg_ref[...]
    o_ref[...] = acc


def _moe(h2d, g, wr_t, br, eexp, wg, wu, wd, fg, layer, final):
    n = h2d.shape[0]
    tm = ROW_TILE
    return pl.pallas_call(
        functools.partial(_moe_kernel, final=final),
        grid=(n // tm,),
        in_specs=[pl.BlockSpec((tm, D_MODEL), lambda i: (i, 0)),
                  _layer_spec(g, layer), _layer_spec(wr_t, layer), _layer_spec(br, layer),
                  _const_spec(eexp.shape), _layer_spec(wg, layer), _layer_spec(wu, layer),
                  _layer_spec(wd, layer), _const_spec(fg.shape)],
        out_specs=pl.BlockSpec((tm, D_MODEL), lambda i: (i, 0)),
        out_shape=jax.ShapeDtypeStruct((n, D_MODEL), F32),
        compiler_params=_params(("parallel",)),
        name="moe_final" if final else "moe",
    )(h2d, g, wr_t, br, eexp, wg, wu, wd, fg)


def _split_points():
    sizes = ([NSA_Q_COLS] + [NSA_KV_COLS] * 6
             + [NSA_GATE_COLS, DIFF_QK_COLS, DIFF_QK_COLS, DIFF_V_COLS, D_MODEL, D_MODEL])
    return [int(v) for v in np.cumsum(sizes)[:-1]]


def _cmp_to_sel_t(n_rows, nc, nb):
    c0 = np.arange(nc)[:, None] * CMP_STRIDE
    s0 = np.arange(nb)[None, :] * SEL_BLOCK
    ov = np.maximum(0, np.minimum(c0 + CMP_BLOCK, s0 + SEL_BLOCK) - np.maximum(c0, s0)) / CMP_BLOCK
    out = np.zeros((nb, n_rows), np.float32)
    out[:, :nc] = ov.T
    return out


def kernel(x, norm1_g, w_in, cmp_pe, cmp_w1, cmp_b1, cmp_w2, cmp_b2, diff_lambda, diff_subln_g, w_branch_a, w_branch_b, w_out, norm2_g, router_grp_w, router_grp_b, router_exp_w, router_exp_b, exp_w_gate, exp_w_up, exp_w_down, final_norm_g):
    bsz, seq, d = x.shape
    depth = w_in.shape[0]
    n = bsz * seq
    nq = seq // CH
    n_half = seq // CMP_STRIDE
    nc = (seq - CMP_BLOCK) // CMP_STRIDE + 1
    nb = seq // SEL_BLOCK
    assert d == D_MODEL and seq % ROW_TILE == 0 and seq >= WINDOW and WINDOW % CH == 0
    assert nb % 16 == 0 and MASK_ROW0 + nb <= AUG_LANES and n_half <= 256
    g_kv = NSA_KV_GROUPS
    eye_g = jnp.eye(g_kv, dtype=F32)

    (nq_w, kc_w, vc_w, ks_w, vs_w, kw_w, vw_w, ng_w, dq_w, dk_w, dv_w, ga_w, gb_w) = jnp.split(
        w_in, _split_points(), axis=-1)
    ng_w = ng_w.reshape(depth, d, g_kv, NSA_GROUP_SIZE, 3).transpose(0, 1, 2, 4, 3)
    ng_w = ng_w.reshape(depth, d, g_kv, 3 * NSA_GROUP_SIZE)
    ng_w = jnp.pad(ng_w, ((0, 0), (0, 0), (0, 0), (0, GATE_ROWS_PER_GROUP - 3 * NSA_GROUP_SIZE)))
    ng_w = ng_w.reshape(depth, d, g_kv * GATE_ROWS_PER_GROUP)
    wn_all = jnp.concatenate([ks_w, kw_w, dk_w, kc_w, vc_w, ga_w, gb_w], axis=-1).astype(BF16)
    wt_all = jnp.concatenate([nq_w, dq_w, vs_w, vw_w, dv_w, ng_w], axis=-1)
    wt_all = jnp.swapaxes(wt_all, 1, 2).astype(BF16)

    w1r = cmp_w1.reshape(depth, 2, 2, CMP_STRIDE, HEAD_DIM, CMP_HIDDEN).astype(BF16)
    w1_big = jnp.zeros((depth, 2, CMP_STRIDE, 2 * NSA_KV_COLS, 2 * g_kv * CMP_HIDDEN), BF16)
    for c in range(2):
        for g in range(g_kv):
            r0 = c * NSA_KV_COLS + g * HEAD_DIM
            c0 = (c * g_kv + g) * CMP_HIDDEN
            w1_big = w1_big.at[:, :, :, r0:r0 + HEAD_DIM, c0:c0 + CMP_HIDDEN].set(w1r[:, c])
    w1_big = w1_big.reshape(depth, 2, CMP_STRIDE * 2 * NSA_KV_COLS, 2 * g_kv * CMP_HIDDEN)
    per = cmp_pe.reshape(depth, 2, 2, CMP_STRIDE, HEAD_DIM)
    pe_hb = jnp.einsum('Lchld,g->Lhlcgd', per, jnp.ones((g_kv,), F32))
    pe_hb = pe_hb.reshape(depth, 2, 1, CMP_STRIDE * 2 * NSA_KV_COLS)
    b1p = jnp.broadcast_to(cmp_b1[:, :, None, :], (depth, 2, g_kv, CMP_HIDDEN)).reshape(depth, 1, -1)
    w2k = jnp.einsum('Lfd,gG->LgfGd', cmp_w2[:, 0], eye_g).reshape(depth, g_kv * CMP_HIDDEN, NSA_KV_COLS)
    w2vt = jnp.einsum('Lfd,gG->LGdgf', cmp_w2[:, 1], eye_g).reshape(depth, NSA_KV_COLS, g_kv * CMP_HIDDEN)
    w2k = w2k.astype(BF16)
    w2vt = w2vt.astype(BF16)
    b2k = jnp.tile(cmp_b2[:, 0], (1, g_kv))[:, None, :]
    b2v = jnp.tile(cmp_b2[:, 1], (1, g_kv))[:, :, None]

    slopes = _alibi_slopes()
    nsa_srows = _slope_rows(slopes[:NSA_HEADS], NSA_HEADS, CH)[0]
    diff_srows = _slope_rows(np.repeat(slopes[NSA_HEADS:], 2), 2, DIFF_TILE_CHUNKS * CH)
    kaug = _key_aug_tables(seq, nb)
    kcaug = _cmp_aug_table(n_half)
    asel_t = jnp.asarray(_cmp_to_sel_t(n_half, nc, nb), BF16)

    wa_all = w_branch_a.astype(BF16)
    wb_all = w_branch_b.astype(BF16)
    wo_all = w_out.astype(BF16)

    wr = jnp.concatenate([router_grp_w, router_exp_w.reshape(depth, d, N_EXPERTS)], axis=-1)
    n_r = MOE_GROUPS + N_EXPERTS
    wr_t = jnp.pad(jnp.swapaxes(wr, 1, 2), ((0, 0), (0, 32 - n_r), (0, 0)))
    wr_hi = wr_t.astype(BF16)
    wr_t = jnp.concatenate([wr_hi, (wr_t - wr_hi.astype(F32)).astype(BF16)], axis=1)
    br = jnp.concatenate([router_grp_b, router_exp_b.reshape(depth, N_EXPERTS)], axis=-1)
    br = jnp.pad(br, ((0, 0), (0, 32 - n_r)))[:, :, None]
    eexp = jnp.asarray(np.eye(N_EXPERTS, 128, dtype=np.float32), BF16)
    wg_all = jnp.swapaxes(exp_w_gate, 1, 2).reshape(depth, d, N_EXPERTS * EXPERT_FF).astype(BF16)
    wu_all = jnp.swapaxes(exp_w_up, 1, 2).reshape(depth, d, N_EXPERTS * EXPERT_FF).astype(BF16)
    wd_all = exp_w_down.reshape(depth, N_EXPERTS * EXPERT_FF, d).astype(BF16)

    h = x.reshape(n, d)
    for l in range(depth):
        (ks, kw, dk, kcvc_hb, gates, nq_t, dq_t, vs_t, vw_t, dv_t, ng_t) = _inproj(
            h, norm1_g[:, None, :], wn_all, wt_all, l)
        hb = kcvc_hb.reshape(bsz, n_half, CMP_STRIDE * 2 * NSA_KV_COLS)
        kc, vc_t = _compress(hb, pe_hb, w1_big, b1p, w2k, b2k, w2vt, b2v, l)
        ya_t = _nsa(nq_t, nsa_srows, ks.reshape(n // CH, CH, NSA_KV_COLS), kw.reshape(n // CH, CH, NSA_KV_COLS),
                    vs_t, vw_t, kaug, kc, kcaug, vc_t, ng_t, asel_t, bsz, nq)
        lam_init = 0.8 - 0.6 * float(np.exp(-0.3 * l))
        scal = jnp.asarray([lam_init, 1.0 - lam_init], F32)
        yb_t = _diff(scal, dq_t, diff_srows, dk.reshape(n // CH, CH, DIFF_QK_COLS), dv_t, kaug[1],
                     diff_lambda[l], diff_subln_g[l][:, None], bsz, nq)
        h = _merge(ya_t, yb_t, gates, h, wa_all, wb_all, wo_all, l)
        h = _moe(h, norm2_g[:, None, :], wr_t, br, eexp, wg_all, wu_all, wd_all,
                 final_norm_g[None, :], l, final=(l == depth - 1))
    return h.reshape(bsz, seq, d)
```

```python
import functools

import numpy as np
import jax
import jax.numpy as jnp
from jax import lax
from jax.experimental import pallas as pl
from jax.experimental.pallas import tpu as pltpu

F32 = jnp.float32
BF16 = jnp.bfloat16

D_MODEL = 1024
HEAD_DIM = 64
NSA_HEADS = 8
NSA_KV_GROUPS = 2
NSA_GROUP_SIZE = NSA_HEADS // NSA_KV_GROUPS
CMP_BLOCK = 32
CMP_STRIDE = 16
CMP_HIDDEN = 128
SEL_BLOCK = 64
SEL_TOPK = 8
WINDOW = 512
FORCED_SCORE = 1e9
DIFF_HEADS = 4
MOE_GROUPS = 4
EXPERTS_PER_GROUP = 4
N_EXPERTS = MOE_GROUPS * EXPERTS_PER_GROUP
EXPERT_FF = D_MODEL // 8
RMS_EPS = 1e-6
SUBLN_EPS = 1e-5
NEG_INF = -1e30
N_ALIBI_HEADS = NSA_HEADS + DIFF_HEADS

NSA_Q_COLS = NSA_HEADS * HEAD_DIM
NSA_KV_COLS = NSA_KV_GROUPS * HEAD_DIM
NSA_GATE_COLS = 3 * NSA_HEADS
DIFF_QK_COLS = DIFF_HEADS * 2 * HEAD_DIM
DIFF_V_COLS = DIFF_HEADS * 2 * HEAD_DIM
GATE_ROWS_PER_GROUP = 16

CH = 256
DIFF_TILE_CHUNKS = 2
DIFF_HEADS_PER_STEP = 4
ROW_TILE = 1024
MERGE_SUB = 2
VMEM_LIMIT = 56 * 1024 * 1024

LOG2E = float(np.log2(np.e))
Q_SCALE = HEAD_DIM ** -0.5 * LOG2E

QK_LANES = 2 * HEAD_DIM
AUG_LANES = 128
SLOPE_PIECES = 3
PAD_ROW = 2 * SLOPE_PIECES
ALIBI_ROWS = 16
MASK_ROW0 = ALIBI_ROWS
MASK_BIG = 1e30
SUM_ROWS = 16
BLOCK_COLS = 256

_NT = (((1,), (1,)), ((), ()))
_TN = (((0,), (0,)), ((), ()))


def _dot(a, b):
    return jnp.dot(a, b, preferred_element_type=F32)


def _const_spec(shape):
    nd = len(shape)
    return pl.BlockSpec(shape, lambda *_: (0,) * nd, pipeline_mode=pl.Buffered(1))


def _layer_spec(stacked, layer):
    nd = stacked.ndim - 1
    return pl.BlockSpec((None,) + tuple(stacked.shape[1:]), lambda *_: (layer,) + (0,) * nd,
                        pipeline_mode=pl.Buffered(1))


def _params(sem):
    return pltpu.CompilerParams(dimension_semantics=sem, vmem_limit_bytes=VMEM_LIMIT)


def _alibi_slopes():
    return 2.0 ** (-8.0 * np.arange(1, N_ALIBI_HEADS + 1) / N_ALIBI_HEADS)


_NAT_WIDTHS = (NSA_KV_COLS, NSA_KV_COLS, DIFF_QK_COLS, 2 * NSA_KV_COLS, 2 * D_MODEL)
_TR_ROWS = (NSA_Q_COLS, DIFF_QK_COLS, NSA_KV_COLS, NSA_KV_COLS, DIFF_V_COLS, 2 * GATE_ROWS_PER_GROUP)
_TR_SCALE = (Q_SCALE, Q_SCALE, 1.0, 1.0, 1.0, 1.0)
_KCVC_INDEX = 3


def _inproj_kernel(x_ref, g_ref, wn_ref, wt_ref, *refs):
    n_out = len(_NAT_WIDTHS) + len(_TR_ROWS)
    nat_refs = refs[:len(_NAT_WIDTHS)]
    tr_refs = refs[len(_NAT_WIDTHS):n_out]
    rows_k, rows_v = refs[n_out:]
    x = x_ref[...]
    xn = (x * lax.rsqrt(jnp.mean(x * x, axis=-1, keepdims=True) + RMS_EPS) * g_ref[...]).astype(BF16)
    off = 0
    for idx, (ref, width) in enumerate(zip(nat_refs, _NAT_WIDTHS)):
        if idx == _KCVC_INDEX:
            res = _dot(xn, wn_ref[:, off:off + width])
            rows_k[...] = res[:, :NSA_KV_COLS]
            rows_v[...] = res[:, NSA_KV_COLS:]
            n_rows = res.shape[0] // CMP_STRIDE
            for tok in range(CMP_STRIDE):
                lo = tok * width
                ref[:, lo:lo + NSA_KV_COLS] = rows_k[pl.ds(tok, n_rows, stride=CMP_STRIDE), :]
                ref[:, lo + NSA_KV_COLS:lo + width] = rows_v[pl.ds(tok, n_rows, stride=CMP_STRIDE), :]
        else:
            for c in range(0, width, 512):
                cw = min(512, width - c)
                ref[:, c:c + cw] = _dot(xn, wn_ref[:, off + c:off + c + cw]).astype(ref.dtype)
        off += width
    n_sub = x.shape[0] // CH
    off = 0
    for ref, rows, scale in zip(tr_refs, _TR_ROWS, _TR_SCALE):
        for c in range(0, rows, 512):
            rw = min(512, rows - c)
            res = lax.dot_general(wt_ref[off + c:off + c + rw, :], xn, _NT, preferred_element_type=F32)
            if scale != 1.0:
                res = res * scale
            for j in range(n_sub):
                ref[j, c:c + rw, :] = res[:, j * CH:(j + 1) * CH].astype(ref.dtype)
        off += rows


def _inproj(h2d, g, wn, wt, layer):
    n = h2d.shape[0]
    tm = ROW_TILE
    nat_dtypes = (BF16, BF16, BF16, F32, BF16)
    tr_dtypes = (BF16, BF16, BF16, BF16, BF16, F32)
    out_shape = [jax.ShapeDtypeStruct((n, w), dt) for w, dt in zip(_NAT_WIDTHS, nat_dtypes)]
    out_shape += [jax.ShapeDtypeStruct((n // CH, r, CH), dt) for r, dt in zip(_TR_ROWS, tr_dtypes)]
    out_specs = [pl.BlockSpec((tm, w), lambda i: (i, 0)) for w in _NAT_WIDTHS]
    out_specs += [pl.BlockSpec((tm // CH, r, CH), lambda i: (i, 0, 0)) for r in _TR_ROWS]
    hb_width = CMP_STRIDE * _NAT_WIDTHS[_KCVC_INDEX]
    out_shape[_KCVC_INDEX] = jax.ShapeDtypeStruct((n // CMP_STRIDE, hb_width), F32)
    out_specs[_KCVC_INDEX] = pl.BlockSpec((tm // CMP_STRIDE, hb_width), lambda i: (i, 0))
    return pl.pallas_call(
        _inproj_kernel,
        grid=(n // tm,),
        in_specs=[pl.BlockSpec((tm, D_MODEL), lambda i: (i, 0)),
                  _layer_spec(g, layer), _layer_spec(wn, layer), _layer_spec(wt, layer)],
        out_specs=out_specs,
        out_shape=out_shape,
        scratch_shapes=[pltpu.VMEM((tm, NSA_KV_COLS), F32), pltpu.VMEM((tm, NSA_KV_COLS), F32)],
        compiler_params=_params(("parallel",)),
        name="inproj",
    )(h2d, g, wn, wt)


def _compress_kernel(hb_ref, pe_ref, w1_ref, b1_ref, w2k_ref, b2k_ref, w2v_ref, b2v_ref, kc_ref, vct_ref):
    hb = hb_ref[0]
    rows = hb.shape[0]
    top = (hb + pe_ref[0]).astype(BF16)
    bot = (hb + pe_ref[1]).astype(BF16)
    p = _dot(top, w1_ref[0])
    q = _dot(bot, w1_ref[1])
    q_next = pltpu.roll(q, rows - 1, 0)
    hid = jax.nn.gelu(p + q_next + b1_ref[...])
    width = hid.shape[1] // 2
    kc_ref[0] = (_dot(hid[:, :width].astype(BF16), w2k_ref[...]) + b2k_ref[...]).astype(kc_ref.dtype)
    vct = lax.dot_general(w2v_ref[...], hid[:, width:].astype(BF16), _NT, preferred_element_type=F32)
    vct_ref[0] = (vct + b2v_ref[...]).astype(vct_ref.dtype)


def _compress(hb, pe_hb, w1_big, b1p, w2k, b2k, w2vt, b2v, layer):
    bsz, rows, width = hb.shape
    gk = NSA_KV_COLS
    return pl.pallas_call(
        _compress_kernel,
        grid=(bsz,),
        in_specs=[pl.BlockSpec((1, rows, width), lambda b: (b, 0, 0)),
                  *[_layer_spec(a, layer) for a in (pe_hb, w1_big, b1p, w2k, b2k, w2vt, b2v)]],
        out_specs=[pl.BlockSpec((1, rows, gk), lambda b: (b, 0, 0)),
                   pl.BlockSpec((1, gk, rows), lambda b: (b, 0, 0))],
        out_shape=[jax.ShapeDtypeStruct((bsz, rows, gk), BF16),
                   jax.ShapeDtypeStruct((bsz, gk, rows), BF16)],
        compiler_params=_params(("parallel",)),
        name="compress",
    )(hb, pe_hb, w1_big, b1p, w2k, b2k, w2vt, b2v)


def _bf16_pieces(x):
    out = []
    rest = np.asarray(x, np.float32)
    for _ in range(SLOPE_PIECES):
        piece = rest.astype(BF16).astype(np.float32)
        out.append(piece)
        rest = rest - piece
    return out


def _slope_rows(slopes, heads_per_block, cols_per_head):
    sl2 = (np.asarray(slopes, np.float32).astype(np.float64) * LOG2E).astype(np.float32)
    pieces = np.stack(_bf16_pieces(sl2) * 2, axis=0)
    rows = np.zeros((ALIBI_ROWS, sl2.shape[0]), np.float32)
    rows[:pieces.shape[0]] = pieces
    rows[PAD_ROW] = -MASK_BIG
    rows = np.repeat(rows, cols_per_head, axis=1)
    rows = rows.reshape(ALIBI_ROWS, -1, heads_per_block * cols_per_head).transpose(1, 0, 2)
    return jnp.asarray(rows, BF16)


def _key_aug_tables(seq, nb):
    pos = np.arange(seq)
    aug = np.zeros((2, seq + CH, AUG_LANES), np.float32)
    aug[:, :seq, 0:SLOPE_PIECES] = (pos % CH)[None, :, None]
    aug[:, :seq, SLOPE_PIECES:2 * SLOPE_PIECES] = (pos // CH * CH)[None, :, None]
    aug[0, pos, MASK_ROW0 + pos // SEL_BLOCK] = 1.0
    aug[:, seq:, PAD_ROW] = 1.0
    return jnp.asarray(aug.reshape(2, seq // CH + 1, CH, AUG_LANES), BF16)


def _cmp_aug_table(n_rows):
    aug = np.zeros((n_rows, AUG_LANES), np.float32)
    aug[:, 0:SLOPE_PIECES] = (np.arange(n_rows) * CMP_STRIDE)[:, None]
    aug[:, SLOPE_PIECES:2 * SLOPE_PIECES] = CMP_BLOCK - 1
    return jnp.asarray(aug, BF16)


def _tile_lanes(x, reps):
    return jnp.concatenate([x] * reps, axis=1)


def _query_minus_key(reps):
    shape = (CH, reps * CH)
    q_off = jnp.bitwise_and(lax.broadcasted_iota(jnp.int32, shape, 1), CH - 1)
    return q_off - lax.broadcasted_iota(jnp.int32, shape, 0)


def _flash_init(m_ref, acc_ref):
    m_ref[...] = jnp.full(m_ref.shape, NEG_INF, F32)
    acc_ref[...] = jnp.zeros(acc_ref.shape, F32)


def _normalized(acc_ref, g):
    dv = acc_ref.shape[1] - SUM_ROWS
    return acc_ref[g, 0:dv, :] * (1.0 / acc_ref[g, dv:dv + 1, :])


def _chunk_scores(k_blk, aug_blk, qa_ref):
    return _dot(jnp.concatenate([k_blk, aug_blk], axis=1), qa_ref[...])


def _stage_and_consume(prod, cons, qa_ref, block_cols=BLOCK_COLS):
    if prod is not None:
        k_blk, aug_blk, (ps_ref, pmx_ref), pmask = prod
        k_full = jnp.concatenate([k_blk, aug_blk], axis=1)
    if cons is not None:
        (cs_ref, cmx_ref), v_t, (m_ref, acc_ref), cmask = cons
        _, rows, gcols = acc_ref.shape
        dv = rows - SUM_ROWS
        ones = jnp.ones((SUM_ROWS, v_t.shape[1]), BF16)
    for c0 in range(0, qa_ref.shape[1], block_cols):
        csl = slice(c0, c0 + block_cols)
        if prod is not None:
            s = _dot(k_full, qa_ref[:, csl])
            if pmask is not None:
                s = jnp.where(pmask(c0), s, NEG_INF)
            ps_ref[:, csl] = s
            pmx_ref[:, csl] = jnp.max(s, axis=0, keepdims=True)
        if cons is not None:
            s = cs_ref[:, csl]
            if cmask is None:
                mx = cmx_ref[:, csl]
            else:
                s = jnp.where(cmask(c0), s, NEG_INF)
                mx = jnp.max(s, axis=0, keepdims=True)
            m_prev = m_ref[:, csl]
            m_new = jnp.maximum(m_prev, mx)
            alpha = jnp.exp2(m_prev - m_new)
            p = jnp.exp2(s - m_new).astype(BF16)
            g = c0 // gcols
            gsl = slice(c0 - g * gcols, c0 - g * gcols + block_cols)
            v_ones = jnp.concatenate([v_t[g * dv:(g + 1) * dv], ones], axis=0)
            acc_ref[g, :, gsl] = alpha * acc_ref[g, :, gsl] + _dot(v_ones, p)
            m_ref[:, csl] = m_new


def _consume(buf, v_t, state, qa_ref, mask=None, block_cols=BLOCK_COLS):
    _stage_and_consume(None, (buf, v_t, state, mask), qa_ref, block_cols)


def _pad_or(aug_ref, c, is_pad):
    return aug_ref[jnp.where(is_pad, aug_ref.shape[0] - 1, c)]


def _identity(pos):
    return pos


def _causal_first(n, k_ref, aug_ref, buf0, chunk_at=_identity):
    c = chunk_at(0)
    return (k_ref[c], _pad_or(aug_ref, c, jnp.bitwise_and(n, 1) == 1), buf0, None)


def _causal_pairs(n, k_ref, v_ref, aug_ref, qa_ref, bufs, state, block_cols=BLOCK_COLS, chunk_at=_identity):
    _causal_pairs_multi(n, [(k_ref, v_ref, aug_ref, qa_ref, bufs, state)], block_cols, chunk_at)


def _causal_pairs_multi(n, streams, block_cols=BLOCK_COLS, chunk_at=_identity):
    pad = jnp.bitwise_and(n, 1)

    def pair(k, carry):
        pos = 2 * k - pad
        c0 = chunk_at(jnp.maximum(pos, 0))
        c1 = chunk_at(pos + 1)
        c2 = chunk_at(pos + 2)
        for k_ref, v_ref, aug_ref, qa_ref, (buf0, buf1), state in streams:
            _stage_and_consume((k_ref[c1], aug_ref[c1], buf1, None), (buf0, v_ref[c0], state, None),
                               qa_ref, block_cols)
        for k_ref, v_ref, aug_ref, qa_ref, (buf0, buf1), state in streams:
            _stage_and_consume((k_ref[c2], aug_ref[c2], buf0, None), (buf1, v_ref[c1], state, None),
                               qa_ref, block_cols)
        return carry
    lax.fori_loop(0, jnp.right_shift(n + pad, 1), pair, 0)


def _unselected_mask_rows(imp, k_sel):
    n_blk, width = imp.shape
    rows_per = 8
    j_loc = lax.broadcasted_iota(jnp.int32, (rows_per, width), 0)
    mask_blocks = []
    for r0 in range(0, n_blk, rows_per):
        blk = imp[r0:r0 + rows_per, :]
        cnt = jnp.zeros((rows_per, width), jnp.int32)
        for jp in range(n_blk):
            row = imp[jp:jp + 1, :]
            gt = jnp.where(row > blk, 1, 0)
            ge = jnp.where(row >= blk, 1, 0)
            if jp >= r0 + rows_per - 1:
                cnt = cnt + gt
            elif jp < r0:
                cnt = cnt + ge
            else:
                cnt = cnt + jnp.where(j_loc + r0 > jp, ge, gt)
        mask_blocks.append(jnp.where(cnt < k_sel, 0.0, -MASK_BIG))
    return jnp.concatenate(mask_blocks, axis=0)


def _nsa_kernel(q_ref, srow_ref, ks_ref, kw_ref, vs_ref, vw_ref, kaug_ref, kc_ref, kcaug_ref, vc_ref,
                ng_ref, asel_ref, o_ref, qa_ref, ocmp_ref, s0, x0, s1, x1, w0, y0, w1, y1, w2, y2,
                m_s, acc_s, m_w, acc_w, chunk_list):
    i = pl.program_id(1)
    t0 = i * CH
    nh = NSA_GROUP_SIZE
    ng = NSA_KV_GROUPS
    gcols = nh * CH
    cols = ng * gcols
    n_cmp = kc_ref.shape[1]
    n_blk = asel_ref.shape[0]
    k_sel = min(SEL_TOPK, n_blk)
    bufs_s = ((s0, x0), (s1, x1))
    bufs_w = ((w0, y0), (w1, y1), (w2, y2))
    state_s = (m_s, acc_s)
    state_w = (m_w, acc_w)
    kaug_s = kaug_ref.at[0]
    kaug_w = kaug_ref.at[1]

    t_pos = t0 + lax.broadcasted_iota(jnp.int32, (1, CH), 1)
    d0_i = _query_minus_key(BLOCK_COLS // CH)
    causal = lambda c0: d0_i >= 0
    window_edge = lambda c0: d0_i < 0

    @pl.when((pl.program_id(0) == 0) & (i == 0))
    def _():
        qa_ref[...] = jnp.zeros(qa_ref.shape, BF16)
        qa_ref[QK_LANES:QK_LANES + ALIBI_ROWS, :] = srow_ref[...]
    for h in range(ng * nh):
        g = h // nh
        qa_ref[g * HEAD_DIM:(g + 1) * HEAD_DIM, h * CH:(h + 1) * CH] = q_ref[0, h * HEAD_DIM:(h + 1) * HEAD_DIM, :]

    cmp_scores = _chunk_scores(kc_ref[0], kcaug_ref[...], qa_ref)

    n_back = WINDOW // CH
    stage_w = []
    chunks_w = []
    for back in range(n_back, 0, -1):
        c = jnp.maximum(i - back, 0)
        stage_w.append((kw_ref[c], _pad_or(kaug_w, c, i < back), bufs_w[n_back - back],
                        window_edge if back == n_back else None))
        chunks_w.append(c)
    stage_w.append((kw_ref[i], kaug_w[i], bufs_w[n_back], causal))
    chunks_w.append(i)
    _stage_and_consume(stage_w[0], None, qa_ref)

    n_idx = lax.broadcasted_iota(jnp.int32, (n_cmp, cols), 0)
    t_pos_all = t0 + jnp.bitwise_and(lax.broadcasted_iota(jnp.int32, (1, cols), 1), CH - 1)
    valid_c = n_idx * CMP_STRIDE + (CMP_BLOCK - 1) <= t_pos_all
    lg = jnp.where(valid_c, cmp_scores, NEG_INF)
    m = jnp.max(lg, axis=0, keepdims=True)
    p = jnp.exp2(lg - m)
    l = jnp.sum(p, axis=0, keepdims=True)
    any_valid = t_pos_all >= CMP_BLOCK - 1
    pc = p * jnp.where(any_valid, 1.0 / l, 0.0)
    pc_b = pc.astype(BF16)
    vc = vc_ref[0]

    j_idx = lax.broadcasted_iota(jnp.int32, (n_blk, CH), 0)
    cur = jnp.right_shift(t_pos, SEL_BLOCK.bit_length() - 1)
    forced = (j_idx == 0) | (j_idx == cur) | (j_idx == cur - 1)
    in_past = j_idx * SEL_BLOCK <= t_pos
    for g in range(ng):
        gsl = slice(g * gcols, (g + 1) * gcols)
        ocmp_ref[g] = _dot(vc[g * HEAD_DIM:(g + 1) * HEAD_DIM], pc_b[:, gsl])
        psum = pc[:, g * gcols:g * gcols + CH]
        for hh in range(1, nh):
            psum = psum + pc[:, g * gcols + hh * CH:g * gcols + (hh + 1) * CH]
        imp = jnp.zeros((n_blk, CH), F32)
        rest = psum
        for _ in range(SLOPE_PIECES):
            piece = rest.astype(BF16)
            imp = imp + _dot(asel_ref[...], piece)
            rest = rest - piece.astype(F32)
        imp = jnp.where(in_past, jnp.where(forced, FORCED_SCORE, imp), -1.0)
        mask_rows = _unselected_mask_rows(imp, k_sel)
        kept = mask_rows if g == 0 else jnp.maximum(kept, mask_rows)
        qa_ref[QK_LANES + MASK_ROW0:QK_LANES + MASK_ROW0 + n_blk, gsl] = _tile_lanes(mask_rows.astype(BF16), nh)

    blocks_per_chunk = CH // SEL_BLOCK
    n_sel = jnp.int32(0)
    for c in range(n_blk // blocks_per_chunk):
        hit = jnp.max(kept[c * blocks_per_chunk:(c + 1) * blocks_per_chunk, :]) > -1.0
        chunk_list[n_sel] = c
        n_sel = n_sel + jnp.logical_and(hit, c < i).astype(jnp.int32)
    chunk_list[n_sel] = i

    def chunk_at(pos):
        return chunk_list[pos]

    _flash_init(*state_w)
    for j in range(1, n_back + 1):
        _stage_and_consume(stage_w[j], (bufs_w[j - 1], vw_ref[chunks_w[j - 1]], state_w, None), qa_ref)
    _stage_and_consume(_causal_first(n_sel, ks_ref, kaug_s, bufs_s[0], chunk_at),
                       (bufs_w[n_back], vw_ref[i], state_w, None), qa_ref)
    _flash_init(*state_s)
    _causal_pairs(n_sel, ks_ref, vs_ref, kaug_s, qa_ref, bufs_s, state_s, chunk_at=chunk_at)
    _consume(bufs_s[0], vs_ref[i], state_s, qa_ref, mask=causal)

    for g in range(ng):
        def gate(branch):
            r0 = g * GATE_ROWS_PER_GROUP + branch * nh
            return jax.nn.sigmoid(jnp.concatenate([ng_ref[0, r0 + hh:r0 + hh + 1, :] for hh in range(nh)], axis=1))
        out = gate(0) * ocmp_ref[g] + gate(1) * _normalized(acc_s, g) + gate(2) * _normalized(acc_w, g)
        for hh in range(nh):
            h = g * nh + hh
            o_ref[0, h * HEAD_DIM:(h + 1) * HEAD_DIM, :] = out[:, hh * CH:(hh + 1) * CH].astype(o_ref.dtype)


def _nsa(nq_t, srows, ks3, kw3, vs_t, vw_t, kaug, kc, kcaug, vc_t, ng_t, asel_t, bsz, nq):
    n_cmp = kc.shape[1]
    ng = NSA_KV_GROUPS
    gcols = NSA_GROUP_SIZE * CH
    cols = ng * gcols
    acc_shape = (ng, HEAD_DIM + SUM_ROWS, gcols)
    stage = [pltpu.VMEM((CH, cols), F32), pltpu.VMEM((1, cols), F32)]
    return pl.pallas_call(
        _nsa_kernel,
        grid=(bsz, nq),
        in_specs=[
            pl.BlockSpec((1, NSA_Q_COLS, CH), lambda b, i: (b * nq + i, 0, 0)),
            pl.BlockSpec(srows.shape, lambda b, i: (0, 0)),
            pl.BlockSpec((nq, CH, NSA_KV_COLS), lambda b, i: (b, 0, 0)),
            pl.BlockSpec((nq, CH, NSA_KV_COLS), lambda b, i: (b, 0, 0)),
            pl.BlockSpec((nq, NSA_KV_COLS, CH), lambda b, i: (b, 0, 0)),
            pl.BlockSpec((nq, NSA_KV_COLS, CH), lambda b, i: (b, 0, 0)),
            pl.BlockSpec(kaug.shape, lambda b, i: (0, 0, 0, 0)),
            pl.BlockSpec((1, n_cmp, NSA_KV_COLS), lambda b, i: (b, 0, 0)),
            pl.BlockSpec(kcaug.shape, lambda b, i: (0, 0)),
            pl.BlockSpec((1, NSA_KV_COLS, n_cmp), lambda b, i: (b, 0, 0)),
            pl.BlockSpec((1, ng * GATE_ROWS_PER_GROUP, CH), lambda b, i: (b * nq + i, 0, 0)),
            pl.BlockSpec(asel_t.shape, lambda b, i: (0, 0)),
        ],
        out_specs=pl.BlockSpec((1, NSA_Q_COLS, CH), lambda b, i: (b * nq + i, 0, 0)),
        out_shape=jax.ShapeDtypeStruct((bsz * nq, NSA_Q_COLS, CH), BF16),
        scratch_shapes=(
            [pltpu.VMEM((QK_LANES + AUG_LANES, cols), BF16), pltpu.VMEM((ng, HEAD_DIM, gcols), F32)]
            + stage * 5
            + [pltpu.VMEM((1, cols), F32), pltpu.VMEM(acc_shape, F32),
               pltpu.VMEM((1, cols), F32), pltpu.VMEM(acc_shape, F32),
               pltpu.SMEM((nq + 1,), jnp.int32)]),
        compiler_params=_params(("arbitrary", "arbitrary")),
        name="nsa_attention",
    )(nq_t, srows, ks3, kw3, vs_t, vw_t, kaug, kc, kcaug, vc_t, ng_t, asel_t)


def _diff_kernel(scal_ref, q_ref, srow_ref, k_ref, v_ref, kaug_ref, lam_ref, gain_ref, o_ref, *scratch):
    nt = DIFF_TILE_CHUNKS
    tq = nt * CH
    hd2 = 2 * HEAD_DIM
    c0 = pl.program_id(2) * nt
    lam_init = scal_ref[0]
    out_scale = scal_ref[1]
    sub2 = lax.broadcasted_iota(jnp.int32, (QK_LANES, CH), 0)
    zero = jnp.zeros((), BF16)
    bc = 2 * tq
    lane = lax.broadcasted_iota(jnp.int32, (CH, bc), 1)
    sub = lax.broadcasted_iota(jnp.int32, (CH, bc), 0)

    def on_or_after(first_key):
        return lambda col0: jnp.bitwise_and(lane + col0, tq - 1) - sub >= first_key

    streams = []
    for hd in range(DIFF_HEADS_PER_STEP):
        qa_ref, s0, x0, s1, x1, m_r, acc_r = scratch[7 * hd:7 * (hd + 1)]
        rows = slice(hd * hd2, (hd + 1) * hd2)
        for j in range(nt):
            q = q_ref[j, rows, :]
            qa_ref[0:QK_LANES, j * CH:(j + 1) * CH] = jnp.where(sub2 < HEAD_DIM, q, zero)
            qa_ref[0:QK_LANES, tq + j * CH:tq + (j + 1) * CH] = jnp.where(sub2 >= HEAD_DIM, q, zero)
        qa_ref[QK_LANES:QK_LANES + ALIBI_ROWS, :] = srow_ref[hd]
        qa_ref[QK_LANES + MASK_ROW0:, :] = jnp.zeros((AUG_LANES - MASK_ROW0, bc), BF16)
        streams.append((k_ref.at[:, :, rows], v_ref.at[:, rows, :], kaug_ref, qa_ref,
                        ((s0, x0), (s1, x1)), (m_r, acc_r)))

    for k_at, v_at, aug, qa_ref, bufs, state in streams:
        _stage_and_consume(_causal_first(c0, k_at, aug, bufs[0]), None, qa_ref, bc)
        _flash_init(*state)
    _causal_pairs_multi(c0, streams, bc)
    for k_at, v_at, aug, qa_ref, bufs, state in streams:
        _stage_and_consume((k_at[c0 + 1], aug[c0 + 1], bufs[1], on_or_after(CH)),
                           (bufs[0], v_at[c0], state, on_or_after(0)), qa_ref, bc)
    for k_at, v_at, aug, qa_ref, bufs, state in streams:
        _consume(bufs[1], v_at[c0 + 1], state, qa_ref, block_cols=bc)

    lp = lam_ref[...]
    lam = (jnp.exp(jnp.sum(lp[0:1] * lp[1:2], axis=1, keepdims=True))
           - jnp.exp(jnp.sum(lp[2:3] * lp[3:4], axis=1, keepdims=True)) + lam_init)
    for hd, stream in enumerate(streams):
        att = _normalized(stream[5][1], 0)
        o = att[:, 0:tq] - lam * att[:, tq:2 * tq]
        o = o * lax.rsqrt(jnp.mean(o * o, axis=0, keepdims=True) + SUBLN_EPS) * gain_ref[...]
        o = (o * out_scale).astype(o_ref.dtype)
        for j in range(nt):
            o_ref[j, hd * hd2:(hd + 1) * hd2, :] = o[:, j * CH:(j + 1) * CH]


def _diff(scal, dq_t, srows, dk3, dv_t, kaug_plain, lam_p, gain, bsz, nq):
    hd2 = 2 * HEAD_DIM
    nt = DIFF_TILE_CHUNKS
    n_tiles = nq // nt
    cols = 2 * nt * CH
    hps = DIFF_HEADS_PER_STEP
    rows = hps * hd2
    per_head = [
        pltpu.VMEM((QK_LANES + AUG_LANES, cols), BF16),
        pltpu.VMEM((CH, cols), F32), pltpu.VMEM((1, cols), F32),
        pltpu.VMEM((CH, cols), F32), pltpu.VMEM((1, cols), F32),
        pltpu.VMEM((1, cols), F32), pltpu.VMEM((1, hd2 + SUM_ROWS, cols), F32),
    ]
    grid_spec = pltpu.PrefetchScalarGridSpec(
        num_scalar_prefetch=1,
        grid=(bsz, DIFF_HEADS // hps, n_tiles),
        in_specs=[
            pl.BlockSpec((nt, rows, CH), lambda b, h, i, s: (b * n_tiles + i, h, 0)),
            pl.BlockSpec((hps, ALIBI_ROWS, cols), lambda b, h, i, s: (h, 0, 0)),
            pl.BlockSpec((nq, CH, rows), lambda b, h, i, s: (b, 0, h)),
            pl.BlockSpec((nq, rows, CH), lambda b, h, i, s: (b, h, 0)),
            pl.BlockSpec(kaug_plain.shape, lambda b, h, i, s: (0, 0, 0)),
            pl.BlockSpec(lam_p.shape, lambda b, h, i, s: (0, 0)),
            pl.BlockSpec(gain.shape, lambda b, h, i, s: (0, 0)),
        ],
        out_specs=pl.BlockSpec((nt, rows, CH), lambda b, h, i, s: (b * n_tiles + i, h, 0)),
        scratch_shapes=per_head * hps,
    )
    return pl.pallas_call(
        _diff_kernel,
        grid_spec=grid_spec,
        out_shape=jax.ShapeDtypeStruct((bsz * nq, DIFF_V_COLS, CH), BF16),
        compiler_params=_params(("parallel", "parallel", "arbitrary")),
        name="diff_attention",
    )(scal, dq_t, srows, dk3, dv_t, kaug_plain, lam_p, gain)


def _merge_kernel(ya_ref, yb_ref, gates_ref, h_ref, wa_ref, wb_ref, wo_ref, o_ref):
    n_sub = ya_ref.shape[0]
    for j0 in range(0, n_sub, MERGE_SUB):
        rows = slice(j0 * CH, (j0 + MERGE_SUB) * CH)
        ya = jnp.concatenate([ya_ref[j] for j in range(j0, j0 + MERGE_SUB)], axis=1)
        yb = jnp.concatenate([yb_ref[j] for j in range(j0, j0 + MERGE_SUB)], axis=1)
        a = lax.dot_general(ya, wa_ref[...], _TN, preferred_element_type=F32)
        b = lax.dot_general(yb, wb_ref[...], _TN, preferred_element_type=F32)
        ga = jax.nn.sigmoid(gates_ref[rows, :D_MODEL].astype(F32))
        gb = jax.nn.sigmoid(gates_ref[rows, D_MODEL:].astype(F32))
        merged = (ga * a + gb * b).astype(BF16)
        o_ref[rows, :] = h_ref[rows, :] + _dot(merged, wo_ref[...])


def _merge(ya_t, yb_t, gates, h2d, wa, wb, wo, layer):
    n = h2d.shape[0]
    tm = ROW_TILE
    return pl.pallas_call(
        _merge_kernel,
        grid=(n // tm,),
        in_specs=[pl.BlockSpec((tm // CH, NSA_Q_COLS, CH), lambda i: (i, 0, 0)),
                  pl.BlockSpec((tm // CH, DIFF_V_COLS, CH), lambda i: (i, 0, 0)),
                  pl.BlockSpec((tm, 2 * D_MODEL), lambda i: (i, 0)),
                  pl.BlockSpec((tm, D_MODEL), lambda i: (i, 0)),
                  _layer_spec(wa, layer), _layer_spec(wb, layer), _layer_spec(wo, layer)],
        out_specs=pl.BlockSpec((tm, D_MODEL), lambda i: (i, 0)),
        out_shape=jax.ShapeDtypeStruct((n, D_MODEL), F32),
        compiler_params=_params(("parallel",)),
        name="merge_outproj",
    )(ya_t, yb_t, gates, h2d, wa, wb, wo)


def _first_argmax(x, rows, n):
    mx = jnp.max(x, axis=0, keepdims=True)
    idx = jnp.min(jnp.where(x == mx, rows, n), axis=0, keepdims=True)
    return mx, idx


def _moe_kernel(h_ref, g_ref, wr_ref, br_ref, eexp_ref, wg_ref, wu_ref, wd_ref, fg_ref, o_ref, *, final):
    hres = h_ref[...]
    xf = hres * lax.rsqrt(jnp.mean(hres * hres, axis=-1, keepdims=True) + RMS_EPS) * g_ref[...]
    xb = xf.astype(BF16)
    tm = hres.shape[0]

    x_lo = (xf - xb.astype(F32)).astype(BF16)
    n_r = wr_ref.shape[0] // 2
    part = lax.dot_general(wr_ref[...], xb, _NT, preferred_element_type=F32)
    logits = (part[0:n_r] + part[n_r:2 * n_r]
              + lax.dot_general(wr_ref[0:n_r, :], x_lo, _NT, preferred_element_type=F32)
              + br_ref[...])
    gl = logits[0:MOE_GROUPS]
    rows_g = lax.broadcasted_iota(jnp.int32, (MOE_GROUPS, tm), 0)
    gmax, gidx = _first_argmax(gl, rows_g, MOE_GROUPS)
    g_w = 1.0 / jnp.sum(jnp.exp(gl - gmax), axis=0, keepdims=True)
    esel = jnp.zeros((EXPERTS_PER_GROUP, tm), F32)
    for gg in range(MOE_GROUPS):
        lo = MOE_GROUPS + gg * EXPERTS_PER_GROUP
        esel = jnp.where(gidx == gg, logits[lo:lo + EXPERTS_PER_GROUP], esel)
    rows_e = lax.broadcasted_iota(jnp.int32, (EXPERTS_PER_GROUP, tm), 0)
    v1, i1 = _first_argmax(esel, rows_e, EXPERTS_PER_GROUP)
    rest = jnp.where(rows_e == i1, -jnp.inf, esel)
    v2, i2 = _first_argmax(rest, rows_e, EXPERTS_PER_GROUP)
    e21 = jnp.exp(v2 - v1)
    w1 = g_w / (1.0 + e21)
    w2 = g_w * e21 / (1.0 + e21)
    rows_c = lax.broadcasted_iota(jnp.int32, (N_EXPERTS, tm), 0)
    grp_c = jnp.right_shift(rows_c, EXPERTS_PER_GROUP.bit_length() - 1)
    exp_c = jnp.bitwise_and(rows_c, EXPERTS_PER_GROUP - 1)
    comb = jnp.where(grp_c == gidx,
                     jnp.where(exp_c == i1, w1, 0.0) + jnp.where(exp_c == i2, w2, 0.0), 0.0)
    comb_hi = comb.astype(BF16)
    comb_lo = (comb - comb_hi.astype(F32)).astype(BF16)

    comb_nat = (lax.dot_general(comb_hi, eexp_ref[...], _TN, preferred_element_type=F32)
                + lax.dot_general(comb_lo, eexp_ref[...], _TN, preferred_element_type=F32))

    n_ff = wg_ref.shape[1]
    step = 256
    parts = []
    for c in range(0, n_ff, step):
        cols = slice(c, c + step)
        hg = _dot(xb, wg_ref[:, cols])
        hu = _dot(xb, wu_ref[:, cols])
        act = jax.nn.silu(hg) * hu
        for e0 in range(0, step, EXPERT_FF):
            e = (c + e0) // EXPERT_FF
            parts.append((act[:, e0:e0 + EXPERT_FF] * comb_nat[:, e:e + 1]).astype(BF16))
    acc = hres + _dot(jnp.concatenate(parts, axis=1), wd_ref[...])
    if final:
        acc = acc * lax.rsqrt(jnp.mean(acc * acc, axis=-1, keepdims=True) + RMS_EPS) * fg_ref[...]
    o_ref[...] = acc


def _moe(h2d, g, wr_t, br, eexp, wg, wu, wd, fg, layer, final):
    n = h2d.shape[0]
    tm = ROW_TILE
    return pl.pallas_call(
        functools.partial(_moe_kernel, final=final),
        grid=(n // tm,),
        in_specs=[pl.BlockSpec((tm, D_MODEL), lambda i: (i, 0)),
                  _layer_spec(g, layer), _layer_spec(wr_t, layer), _layer_spec(br, layer),
                  _const_spec(eexp.shape), _layer_spec(wg, layer), _layer_spec(wu, layer),
                  _layer_spec(wd, layer), _const_spec(fg.shape)],
        out_specs=pl.BlockSpec((tm, D_MODEL), lambda i: (i, 0)),
        out_shape=jax.ShapeDtypeStruct((n, D_MODEL), F32),
        compiler_params=_params(("parallel",)),
        name="moe_final" if final else "moe",
    )(h2d, g, wr_t, br, eexp, wg, wu, wd, fg)


def _split_points():
    sizes = ([NSA_Q_COLS] + [NSA_KV_COLS] * 6
             + [NSA_GATE_COLS, DIFF_QK_COLS, DIFF_QK_COLS, DIFF_V_COLS, D_MODEL, D_MODEL])
    return [int(v) for v in np.cumsum(sizes)[:-1]]


def _cmp_to_sel_t(n_rows, nc, nb):
    c0 = np.arange(nc)[:, None] * CMP_STRIDE
    s0 = np.arange(nb)[None, :] * SEL_BLOCK
    ov = np.maximum(0, np.minimum(c0 + CMP_BLOCK, s0 + SEL_BLOCK) - np.maximum(c0, s0)) / CMP_BLOCK
    out = np.zeros((nb, n_rows), np.float32)
    out[:, :nc] = ov.T
    return out


def kernel(x, norm1_g, w_in, cmp_pe, cmp_w1, cmp_b1, cmp_w2, cmp_b2, diff_lambda, diff_subln_g, w_branch_a, w_branch_b, w_out, norm2_g, router_grp_w, router_grp_b, router_exp_w, router_exp_b, exp_w_gate, exp_w_up, exp_w_down, final_norm_g):
    bsz, seq, d = x.shape
    depth = w_in.shape[0]
    n = bsz * seq
    nq = seq // CH
    n_half = seq // CMP_STRIDE
    nc = (seq - CMP_BLOCK) // CMP_STRIDE + 1
    nb = seq // SEL_BLOCK
    assert d == D_MODEL and seq % ROW_TILE == 0 and seq >= WINDOW and WINDOW % CH == 0
    assert nb % 16 == 0 and MASK_ROW0 + nb <= AUG_LANES and n_half <= 256
    g_kv = NSA_KV_GROUPS
    eye_g = jnp.eye(g_kv, dtype=F32)

    (nq_w, kc_w, vc_w, ks_w, vs_w, kw_w, vw_w, ng_w, dq_w, dk_w, dv_w, ga_w, gb_w) = jnp.split(
        w_in, _split_points(), axis=-1)
    ng_w = ng_w.reshape(depth, d, g_kv, NSA_GROUP_SIZE, 3).transpose(0, 1, 2, 4, 3)
    ng_w = ng_w.reshape(depth, d, g_kv, 3 * NSA_GROUP_SIZE)
    ng_w = jnp.pad(ng_w, ((0, 0), (0, 0), (0, 0), (0, GATE_ROWS_PER_GROUP - 3 * NSA_GROUP_SIZE)))
    ng_w = ng_w.reshape(depth, d, g_kv * GATE_ROWS_PER_GROUP)
    wn_all = jnp.concatenate([ks_w, kw_w, dk_w, kc_w, vc_w, ga_w, gb_w], axis=-1).astype(BF16)
    wt_all = jnp.concatenate([nq_w, dq_w, vs_w, vw_w, dv_w, ng_w], axis=-1)
    wt_all = jnp.swapaxes(wt_all, 1, 2).astype(BF16)

    w1r = cmp_w1.reshape(depth, 2, 2, CMP_STRIDE, HEAD_DIM, CMP_HIDDEN).astype(BF16)
    w1_big = jnp.zeros((depth, 2, CMP_STRIDE, 2 * NSA_KV_COLS, 2 * g_kv * CMP_HIDDEN), BF16)
    for c in range(2):
        for g in range(g_kv):
            r0 = c * NSA_KV_COLS + g * HEAD_DIM
            c0 = (c * g_kv + g) * CMP_HIDDEN
            w1_big = w1_big.at[:, :, :, r0:r0 + HEAD_DIM, c0:c0 + CMP_HIDDEN].set(w1r[:, c])
    w1_big = w1_big.reshape(depth, 2, CMP_STRIDE * 2 * NSA_KV_COLS, 2 * g_kv * CMP_HIDDEN)
    per = cmp_pe.reshape(depth, 2, 2, CMP_STRIDE, HEAD_DIM)
    pe_hb = jnp.einsum('Lchld,g->Lhlcgd', per, jnp.ones((g_kv,), F32))
    pe_hb = pe_hb.reshape(depth, 2, 1, CMP_STRIDE * 2 * NSA_KV_COLS)
    b1p = jnp.broadcast_to(cmp_b1[:, :, None, :], (depth, 2, g_kv, CMP_HIDDEN)).reshape(depth, 1, -1)
    w2k = jnp.einsum('Lfd,gG->LgfGd', cmp_w2[:, 0], eye_g).reshape(depth, g_kv * CMP_HIDDEN, NSA_KV_COLS)
    w2vt = jnp.einsum('Lfd,gG->LGdgf', cmp_w2[:, 1], eye_g).reshape(depth, NSA_KV_COLS, g_kv * CMP_HIDDEN)
    w2k = w2k.astype(BF16)
    w2vt = w2vt.astype(BF16)
    b2k = jnp.tile(cmp_b2[:, 0], (1, g_kv))[:, None, :]
    b2v = jnp.tile(cmp_b2[:, 1], (1, g_kv))[:, :, None]

    slopes = _alibi_slopes()
    nsa_srows = _slope_rows(slopes[:NSA_HEADS], NSA_HEADS, CH)[0]
    diff_srows = _slope_rows(np.repeat(slopes[NSA_HEADS:], 2), 2, DIFF_TILE_CHUNKS * CH)
    kaug = _key_aug_tables(seq, nb)
    kcaug = _cmp_aug_table(n_half)
    asel_t = jnp.asarray(_cmp_to_sel_t(n_half, nc, nb), BF16)

    wa_all = w_branch_a.astype(BF16)
    wb_all = w_branch_b.astype(BF16)
    wo_all = w_out.astype(BF16)

    wr = jnp.concatenate([router_grp_w, router_exp_w.reshape(depth, d, N_EXPERTS)], axis=-1)
    n_r = MOE_GROUPS + N_EXPERTS
    wr_t = jnp.pad(jnp.swapaxes(wr, 1, 2), ((0, 0), (0, 32 - n_r), (0, 0)))
    wr_hi = wr_t.astype(BF16)
    wr_t = jnp.concatenate([wr_hi, (wr_t - wr_hi.astype(F32)).astype(BF16)], axis=1)
    br = jnp.concatenate([router_grp_b, router_exp_b.reshape(depth, N_EXPERTS)], axis=-1)
    br = jnp.pad(br, ((0, 0), (0, 32 - n_r)))[:, :, None]
    eexp = jnp.asarray(np.eye(N_EXPERTS, 128, dtype=np.float32), BF16)
    wg_all = jnp.swapaxes(exp_w_gate, 1, 2).reshape(depth, d, N_EXPERTS * EXPERT_FF).astype(BF16)
    wu_all = jnp.swapaxes(exp_w_up, 1, 2).reshape(depth, d, N_EXPERTS * EXPERT_FF).astype(BF16)
    wd_all = exp_w_down.reshape(depth, N_EXPERTS * EXPERT_FF, d).astype(BF16)

    h = x.reshape(n, d)
    for l in range(depth):
        (ks, kw, dk, kcvc_hb, gates, nq_t, dq_t, vs_t, vw_t, dv_t, ng_t) = _inproj(
            h, norm1_g[:, None, :], wn_all, wt_all, l)
        hb = kcvc_hb.reshape(bsz, n_half, CMP_STRIDE * 2 * NSA_KV_COLS)
        kc, vc_t = _compress(hb, pe_hb, w1_big, b1p, w2k, b2k, w2vt, b2v, l)
        ya_t = _nsa(nq_t, nsa_srows, ks.reshape(n // CH, CH, NSA_KV_COLS), kw.reshape(n // CH, CH, NSA_KV_COLS),
                    vs_t, vw_t, kaug, kc, kcaug, vc_t, ng_t, asel_t, bsz, nq)
        lam_init = 0.8 - 0.6 * float(np.exp(-0.3 * l))
        scal = jnp.asarray([lam_init, 1.0 - lam_init], F32)
        yb_t = _diff(scal, dq_t, diff_srows, dk.reshape(n // CH, CH, DIFF_QK_COLS), dv_t, kaug[1],
                     diff_lambda[l], diff_subln_g[l][:, None], bsz, nq)
        h = _merge(ya_t, yb_t, gates, h, wa_all, wb_all, wo_all, l)
        h = _moe(h, norm2_g[:, None, :], wr_t, br, eexp, wg_all, wu_all, wd_all,
                 final_norm_g[None, :], l, final=(l == depth - 1))
    return h.reshape(bsz, seq, d)
```
